```python
import math
import jax, jax.numpy as jnp
from jax import lax
import numpy as np

D_MODEL = 1024
BATCH = 8
SEQ = 2048
DEPTH = 4
DEC_BATCH = 128
DEC_SEQ = 1
PAST_LEN = 16384
PAGE_SIZE = 128

MIX_WIDTH = D_MODEL // 2
N_BRANCH = 3
R_HEADS = 4
R_HEAD_DIM = MIX_WIDTH // R_HEADS
M_HEADS = 4
M_HEAD_DIM = MIX_WIDTH // M_HEADS
S5_GROUP = 16
S5_GROUPS = MIX_WIDTH // S5_GROUP
S5_STATE = 64
D_FF = 2816
CHUNK = 128
ROPE_THETA = 10000.0
EPS = 1e-6
IN_SIZES = (MIX_WIDTH,) * 8 + (M_HEADS, M_HEADS, MIX_WIDTH, N_BRANCH * D_MODEL)
IN_SPLITS = tuple(int(s) for s in np.cumsum(IN_SIZES)[:-1])
N_IN = int(sum(IN_SIZES))

kernel_name = "hybrid_retention_mlstm_s5_decode_step"

F32 = jnp.float32


def rmsnorm(x, g):
    xf = x.astype(F32)
    y = xf * lax.rsqrt(jnp.mean(xf * xf, axis=-1, keepdims=True) + EPS)
    return (y * g.astype(F32)).astype(x.dtype)


def head_norm(x):
    xf = x.astype(F32)
    mu = jnp.mean(xf, axis=-1, keepdims=True)
    var = jnp.mean(jnp.square(xf - mu), axis=-1, keepdims=True)
    return (xf - mu) * lax.rsqrt(var + EPS)


def swiglu(x, w_gu, w_down):
    g, u = jnp.split(x @ w_gu, 2, axis=-1)
    return (jax.nn.silu(g) * u) @ w_down


def rope(x, pos):
    d = x.shape[-1]
    inv = ROPE_THETA ** (-jnp.arange(0, d, 2, dtype=F32) / d)
    ang = pos.astype(F32)[:, None] * inv[None, :]
    cos, sin = jnp.cos(ang), jnp.sin(ang)
    x1, x2 = x[..., : d // 2], x[..., d // 2:]
    return jnp.concatenate([x1 * cos - x2 * sin, x2 * cos + x1 * sin], axis=-1)


def _heads(t, n):
    b, l, w = t.shape
    return t.reshape(b, l, n, w // n).transpose(0, 2, 1, 3)


def _merge_heads(t):
    b, h, l, d = t.shape
    return t.transpose(0, 2, 1, 3).reshape(b, l, h * d)


def _to_chunks(t, c):
    b, h, l = t.shape[:3]
    return jnp.moveaxis(t.reshape((b, h, l // c, c) + t.shape[3:]), 2, 0)


def _from_chunks(t):
    nc, b, h, c, d = t.shape
    return jnp.moveaxis(t, 0, 2).reshape(b, h, nc * c, d)


def _chunk_len(l):
    return CHUNK if l % CHUNK == 0 else l


def retention(q, k, v, s0):
    l = q.shape[2]
    c = _chunk_len(l)
    lg = jnp.log1p(-jnp.exp2(-5.0 - jnp.arange(R_HEADS, dtype=F32)))
    idx = jnp.arange(c, dtype=F32)
    diff = idx[:, None] - idx[None, :]
    dmat = jnp.where(diff >= 0, jnp.exp(lg[:, None, None] * jnp.maximum(diff, 0.0)), 0.0)
    q_dec = jnp.exp(lg[:, None] * (idx + 1.0))[..., None]
    k_dec = jnp.exp(lg[:, None] * (c - 1.0 - idx))[..., None]
    c_dec = jnp.exp(lg * c)[:, None, None]

    def step(s, xs):
        qc, kc, vc = xs
        att = jnp.einsum('bhid,bhjd->bhij', qc, kc) * dmat
        o = jnp.einsum('bhij,bhje->bhie', att, vc) + jnp.einsum('bhid,bhde->bhie', qc * q_dec, s)
        s = s * c_dec + jnp.einsum('bhjd,bhje->bhde', kc * k_dec, vc)
        return s, o

    xs = (_to_chunks(q.astype(F32), c), _to_chunks(k.astype(F32), c), _to_chunks(v.astype(F32), c))
    s, o = lax.scan(step, s0.astype(F32), xs)
    return _from_chunks(o), s


def mlstm(q, k, v, ig, lf, c0, n0, m0):
    l = q.shape[2]
    c = _chunk_len(l)
    causal = jnp.tril(jnp.ones((c, c), dtype=bool))

    def step(carry, xs):
        cm, nv, m = carry
        qc, kc, vc, ic, fc = xs
        b = jnp.cumsum(fc, axis=-1)
        dlog = b[..., :, None] - b[..., None, :] + ic[..., None, :]
        dlog = jnp.where(causal, dlog, -jnp.inf)
        inter = b + m[..., None]
        m_t = jnp.maximum(inter, jnp.max(dlog, axis=-1))
        w_intra = jnp.exp(dlog - m_t[..., None])
        w_inter = jnp.exp(inter - m_t)
        s = jnp.einsum('bhid,bhjd->bhij', qc, kc) * w_intra
        num = jnp.einsum('bhij,bhje->bhie', s, vc) + w_inter[..., None] * jnp.einsum('bhid,bhde->bhie', qc, cm)
        den = jnp.sum(s, axis=-1) + w_inter * jnp.einsum('bhid,bhd->bhi', qc, nv)
        h = num / jnp.maximum(jnp.abs(den), jnp.exp(-m_t))[..., None]
        b_last = b[..., -1]
        wk_log = b_last[..., None] - b + ic
        m_new = jnp.maximum(b_last + m, jnp.max(wk_log, axis=-1))
        wk = jnp.exp(wk_log - m_new[..., None])[..., None]
        decay = jnp.exp(b_last + m - m_new)
        cm = decay[..., None, None] * cm + jnp.einsum('bhjd,bhje->bhde', kc * wk, vc)
        nv = decay[..., None] * nv + jnp.sum(kc * wk, axis=2)
        return (cm, nv, m_new), h

    xs = (_to_chunks(q.astype(F32), c), _to_chunks(k.astype(F32), c), _to_chunks(v.astype(F32), c),
          _to_chunks(ig, c), _to_chunks(lf, c))
    (cm, nv, m), h = lax.scan(step, (c0.astype(F32), n0.astype(F32), m0.astype(F32)), xs)
    return _from_chunks(h), cm, nv, m


def s5(u, x0_re, x0_im, a_re, a_im, b_re, b_im, c_re, c_im, d_skip, log_dt):
    bsz, l, _ = u.shape
    uf = u.astype(F32)
    ug = uf.reshape(bsz, l, S5_GROUPS, S5_GROUP)
    a = lax.complex(a_re.astype(F32), a_im.astype(F32))
    dt = jnp.exp(log_dt.astype(F32))[:, None]
    a_bar = jnp.exp(a * dt)
    b_mat = lax.complex(b_re.astype(F32), b_im.astype(F32))
    c_mat = lax.complex(c_re.astype(F32), c_im.astype(F32))
    b_bar = ((a_bar - 1.0) / a)[..., None] * b_mat
    bu = jnp.einsum('gpc,blgc->blgp', b_bar, ug)
    x0 = lax.complex(x0_re.astype(F32), x0_im.astype(F32))
    bu = bu.at[:, 0].add(a_bar * x0)
    a_seq = jnp.broadcast_to(a_bar, bu.shape)

    def op(e1, e2):
        a1, b1 = e1
        a2, b2 = e2
        return a1 * a2, a2 * b1 + b2

    _, xs = lax.associative_scan(op, (a_seq, bu), axis=1)
    y = jnp.real(jnp.einsum('gcp,blgp->blgc', c_mat, xs)).reshape(bsz, l, MIX_WIDTH) + d_skip.astype(F32) * uf
    x_last = xs[:, -1]
    return y, jnp.real(x_last), jnp.imag(x_last)


def token_mixers(h, pos, st_ret, st_mc, st_mn, st_mm, st_sre, st_sim,
                 w_in, b_gates, a_re, a_im, b_re, b_im, c_re, c_im, d_skip, log_dt, w_glu, w_branch, w_out):
    bsz, l, _ = h.shape
    proj = h @ w_in
    (rq, rk, rv, rg, mq, mk, mv, mo, mi, mf, su, gates) = jnp.split(proj, IN_SPLITS, axis=-1)
    rq = rope(_heads(rq, R_HEADS), pos)
    rk = rope(_heads(rk, R_HEADS), pos) * (R_HEAD_DIM ** -0.5)
    ro, new_ret = retention(rq, rk, _heads(rv, R_HEADS), st_ret)
    ret_out = _merge_heads(head_norm(ro)) * jax.nn.silu(rg.astype(F32))
    ig = (mi + b_gates[:M_HEADS]).astype(F32).transpose(0, 2, 1)
    lf = jax.nn.log_sigmoid((mf + b_gates[M_HEADS:]).astype(F32)).transpose(0, 2, 1)
    mh, new_mc, new_mn, new_mm = mlstm(_heads(mq, M_HEADS), _heads(mk, M_HEADS) * (M_HEAD_DIM ** -0.5),
                                      _heads(mv, M_HEADS), ig, lf, st_mc, st_mn, st_mm)
    m_out = _merge_heads(head_norm(mh)) * jax.nn.sigmoid(mo.astype(F32))
    sy, new_sre, new_sim = s5(su, st_sre, st_sim, a_re, a_im, b_re, b_im, c_re, c_im, d_skip, log_dt)
    sy = jax.nn.gelu(sy)
    s_out = sy * jax.nn.sigmoid(sy @ w_glu.astype(F32))
    br = jnp.stack([ret_out, m_out, s_out], axis=0).astype(h.dtype)
    up = jnp.einsum('nblw,nwd->nbld', br, w_branch)
    g = jax.nn.sigmoid(gates.astype(F32)).reshape(bsz, l, N_BRANCH, D_MODEL).astype(h.dtype)
    merged = jnp.einsum('blnd,nbld->bld', g, up)
    return merged @ w_out, (new_ret, new_mc, new_mn, new_mm, new_sre, new_sim)


def run_trunk(x, pos, states, weights):
    (st_ret, st_mc, st_mn, st_mm, st_sre, st_sim) = states
    (ln_ffn1, w_ffn1_gu, w_ffn1_down, ln_mix, w_in, b_gates, s5_a_re, s5_a_im, s5_b_re, s5_b_im,
     s5_c_re, s5_c_im, s5_d, s5_log_dt, w_s5_glu, w_branch, w_out, ln_ffn2, w_ffn2_gu, w_ffn2_down,
     ln_final) = weights
    outs = ([], [], [], [], [], [])
    for l in range(DEPTH):
        x = x + 0.5 * swiglu(rmsnorm(x, ln_ffn1[l]), w_ffn1_gu[l], w_ffn1_down[l])
        mix, new = token_mixers(rmsnorm(x, ln_mix[l]), pos,
                                st_ret[l], st_mc[l], st_mn[l], st_mm[l], st_sre[l], st_sim[l],
                                w_in[l], b_gates[l], s5_a_re[l], s5_a_im[l], s5_b_re[l], s5_b_im[l],
                                s5_c_re[l], s5_c_im[l], s5_d[l], s5_log_dt[l], w_s5_glu[l], w_branch[l], w_out[l])
        x = x + mix
        x = x + 0.5 * swiglu(rmsnorm(x, ln_ffn2[l]), w_ffn2_gu[l], w_ffn2_down[l])
        for acc, s in zip(outs, new):
            acc.append(s)
    y = rmsnorm(x, ln_final)
    return y, tuple(jnp.stack(acc, axis=0) for acc in outs)


def setup_inputs(seed: int = 0) -> dict:
    key = jax.random.key(seed)
    ks = iter(jax.random.split(key, 40))

    def normal(shape, scale):
        return scale * jax.random.normal(next(ks), shape, F32)

    W, G, P, GC = MIX_WIDTH, S5_GROUPS, S5_STATE, S5_GROUP
    d = {}
    d["x_prompt"] = normal((BATCH, SEQ, D_MODEL), 1.0)
    d["x_sample"] = normal((DEC_BATCH, DEC_SEQ, D_MODEL), 1.0)
    d["state_ret"] = normal((DEPTH, DEC_BATCH, R_HEADS, R_HEAD_DIM, R_HEAD_DIM), 0.1)
    d["state_mlstm_c"] = normal((DEPTH, DEC_BATCH, M_HEADS, M_HEAD_DIM, M_HEAD_DIM), 0.1)
    d["state_mlstm_n"] = normal((DEPTH, DEC_BATCH, M_HEADS, M_HEAD_DIM), 0.1)
    d["state_mlstm_m"] = normal((DEPTH, DEC_BATCH, M_HEADS), 1.0)
    d["state_s5_re"] = normal((DEPTH, DEC_BATCH, G, P), 0.5)
    d["state_s5_im"] = normal((DEPTH, DEC_BATCH, G, P), 0.5)
    d["ln_ffn1"] = 1.0 + normal((DEPTH, D_MODEL), 0.02)
    d["w_ffn1_gu"] = normal((DEPTH, D_MODEL, 2 * D_FF), D_MODEL ** -0.5)
    d["w_ffn1_down"] = normal((DEPTH, D_FF, D_MODEL), D_FF ** -0.5)
    d["ln_mix"] = 1.0 + normal((DEPTH, D_MODEL), 0.02)
    d["w_in"] = normal((DEPTH, D_MODEL, N_IN), D_MODEL ** -0.5)
    d["b_gates"] = jnp.concatenate(
        [normal((DEPTH, M_HEADS), 0.01),
         jnp.linspace(3.0, 6.0, M_HEADS, dtype=F32)[None, :] + normal((DEPTH, M_HEADS), 0.01)], axis=-1)
    d["s5_a_re"] = -0.5 + normal((DEPTH, G, P), 0.01)
    d["s5_a_im"] = math.pi * jnp.arange(P, dtype=F32) + normal((DEPTH, G, P), 0.01)
    d["s5_b_re"] = normal((DEPTH, G, P, GC), (2 * GC) ** -0.5)
    d["s5_b_im"] = normal((DEPTH, G, P, GC), (2 * GC) ** -0.5)
    d["s5_c_re"] = normal((DEPTH, G, GC, P), P ** -0.5)
    d["s5_c_im"] = normal((DEPTH, G, GC, P), P ** -0.5)
    d["s5_d"] = normal((DEPTH, W), 0.5)
    d["s5_log_dt"] = jax.random.uniform(next(ks), (DEPTH, G), F32, minval=math.log(1e-3), maxval=math.log(1e-1))
    d["w_s5_glu"] = normal((DEPTH, W, W), W ** -0.5)
    d["w_branch"] = normal((DEPTH, N_BRANCH, W, D_MODEL), W ** -0.5)
    d["w_out"] = normal((DEPTH, D_MODEL, D_MODEL), D_MODEL ** -0.5)
    d["ln_ffn2"] = 1.0 + normal((DEPTH, D_MODEL), 0.02)
    d["w_ffn2_gu"] = normal((DEPTH, D_MODEL, 2 * D_FF), D_MODEL ** -0.5)
    d["w_ffn2_down"] = normal((DEPTH, D_FF, D_MODEL), D_FF ** -0.5)
    d["ln_final"] = 1.0 + normal((D_MODEL,), 0.02)
    return d


def reference(x_prompt, x_sample, state_ret, state_mlstm_c, state_mlstm_n, state_mlstm_m, state_s5_re, state_s5_im,
              ln_ffn1, w_ffn1_gu, w_ffn1_down, ln_mix, w_in, b_gates, s5_a_re, s5_a_im, s5_b_re, s5_b_im,
              s5_c_re, s5_c_im, s5_d, s5_log_dt, w_s5_glu, w_branch, w_out, ln_ffn2, w_ffn2_gu, w_ffn2_down,
              ln_final):
    weights = (ln_ffn1, w_ffn1_gu, w_ffn1_down, ln_mix, w_in, b_gates, s5_a_re, s5_a_im, s5_b_re, s5_b_im,
               s5_c_re, s5_c_im, s5_d, s5_log_dt, w_s5_glu, w_branch, w_out, ln_ffn2, w_ffn2_gu, w_ffn2_down,
               ln_final)
    pb = x_prompt.shape[0]
    zero_states = (
        jnp.zeros((DEPTH, pb, R_HEADS, R_HEAD_DIM, R_HEAD_DIM), F32),
        jnp.zeros((DEPTH, pb, M_HEADS, M_HEAD_DIM, M_HEAD_DIM), F32),
        jnp.zeros((DEPTH, pb, M_HEADS, M_HEAD_DIM), F32),
        jnp.zeros((DEPTH, pb, M_HEADS), F32),
        jnp.zeros((DEPTH, pb, S5_GROUPS, S5_STATE), F32),
        jnp.zeros((DEPTH, pb, S5_GROUPS, S5_STATE), F32),
    )
    pos_p = jnp.arange(x_prompt.shape[1], dtype=jnp.int32)
    pos_s = PAST_LEN + jnp.arange(x_sample.shape[1], dtype=jnp.int32)
    y_prompt, (ret_p, mc_p, mn_p, mm_p, sre_p, sim_p) = run_trunk(x_prompt, pos_p, zero_states, weights)
    sample_states = (state_ret, state_mlstm_c, state_mlstm_n, state_mlstm_m, state_s5_re, state_s5_im)
    y_sample, (ret_s, mc_s, mn_s, mm_s, sre_s, sim_s) = run_trunk(x_sample, pos_s, sample_states, weights)
    return (y_prompt, y_sample, ret_p, mc_p, mn_p, mm_p, sre_p, sim_p, ret_s, mc_s, mn_s, mm_s, sre_s, sim_s)
```

```python
import functools
import math

import jax
import jax.numpy as jnp
from jax import lax
from jax.experimental import pallas as pl
from jax.experimental.pallas import tpu as pltpu

F32 = jnp.float32
BF16 = jnp.bfloat16

D_MODEL = 1024
DEPTH = 4
MIX_WIDTH = D_MODEL // 2
N_BRANCH = 3
HEADS = 4
HEAD_DIM = MIX_WIDTH // HEADS
S5_GROUP = 16
S5_GROUPS = MIX_WIDTH // S5_GROUP
S5_STATE = 64
S5_N = S5_GROUPS * S5_STATE
D_FF = 2816
CHUNK = 128
PAST_LEN = 16384
ROPE_THETA = 10000.0
EPS = 1e-6

N_MAIN = 8 * MIX_WIDTH
N_GATES = N_BRANCH * D_MODEL
N_GIF = 128
N_IN_PADDED = N_MAIN + N_GATES + MIX_WIDTH + N_GIF
OFF_GATES = N_MAIN
OFF_SU = N_MAIN + N_GATES
OFF_GIF = OFF_SU + MIX_WIDTH

V7X_LANES = 128
V7X_MXU_DIM = 256
V7X_VMEM_BYTES = 64 * 1024 * 1024

FF_CHUNK = V7X_MXU_DIM
PROJ_CHUNK = 2 * V7X_MXU_DIM
S5_HALF = MIX_WIDTH // 2
S5_HALF_N = S5_N // 2
LOG_GAMMA = tuple(math.log1p(-(2.0 ** (-5.0 - h))) for h in range(HEADS))
QK_SCALE = HEAD_DIM ** -0.5


def _vmem_limit(block_bytes):
    want = int(block_bytes * 1.25) + (8 << 20)
    return min(want, V7X_VMEM_BYTES - (6 << 20))


def _params(block_bytes, n_grid):
    return pltpu.CompilerParams(
        dimension_semantics=("arbitrary",) * n_grid,
        vmem_limit_bytes=_vmem_limit(block_bytes))


def _resident(shape, index_map):
    return pl.BlockSpec(shape, index_map, pipeline_mode=pl.Buffered(1))


def _dot(a, b):
    return jnp.dot(a, b, preferred_element_type=F32)


def _dot_nt(a, b):
    return lax.dot_general(a, b, (((1,), (1,)), ((), ())), preferred_element_type=F32)


def _dot_tn(a, b):
    return lax.dot_general(a, b, (((0,), (0,)), ((), ())), preferred_element_type=F32)


def _rms(x, g):
    return x * lax.rsqrt(jnp.mean(x * x, axis=-1, keepdims=True) + EPS) * g


def _head_norm(x):
    mu = jnp.mean(x, axis=-1, keepdims=True)
    xc = x - mu
    var = jnp.mean(xc * xc, axis=-1, keepdims=True)
    return xc * lax.rsqrt(var + EPS)


def _rope(x, cos, sin_signed):
    return x * cos + pltpu.roll(x, HEAD_DIM // 2, 1) * sin_signed


def _ffn_kernel(x_ref, ln_ref, wgu_ref, wd_ref, lnf_ref, o_ref, acc_ref, *, final):
    x = x_ref[...]
    xn = _rms(x, ln_ref[...]).astype(BF16)
    for c in range(D_FF // FF_CHUNK):
        lo = c * FF_CHUNK
        g = _dot(xn, wgu_ref[:, lo:lo + FF_CHUNK])
        u = _dot(xn, wgu_ref[:, D_FF + lo:D_FF + lo + FF_CHUNK])
        a = (g * jax.nn.sigmoid(g) * u).astype(BF16)
        d = _dot(a, wd_ref[lo:lo + FF_CHUNK, :])
        if c == 0:
            acc_ref[...] = d
        else:
            acc_ref[...] += d
    y = x + 0.5 * acc_ref[...]
    if final:
        y = _rms(y, lnf_ref[...])
    o_ref[...] = y


def _ffn(x, ln, wgu, wd, lnf, layer, tm, final):
    t = x.shape[0]
    blocks = (4 * tm * D_MODEL * 4 + tm * D_MODEL * 4 + D_MODEL * 2 * D_FF * 2 + D_FF * D_MODEL * 2
              + 4 * tm * FF_CHUNK * 4)
    return pl.pallas_call(
        functools.partial(_ffn_kernel, final=final),
        out_shape=jax.ShapeDtypeStruct((t, D_MODEL), F32),
        grid=(t // tm,),
        in_specs=[
            pl.BlockSpec((tm, D_MODEL), lambda i: (i, 0)),
            _resident((None, 1, D_MODEL), lambda i: (layer, 0, 0)),
            _resident((None, D_MODEL, 2 * D_FF), lambda i: (layer, 0, 0)),
            _resident((None, D_FF, D_MODEL), lambda i: (layer, 0, 0)),
            _resident((1, D_MODEL), lambda i: (0, 0)),
        ],
        out_specs=pl.BlockSpec((tm, D_MODEL), lambda i: (i, 0)),
        scratch_shapes=[pltpu.VMEM((tm, D_MODEL), F32)],
        compiler_params=_params(blocks, 1),
        name="ffn",
    )(x, ln, wgu, wd, lnf)


def _inproj_kernel(x_ref, ln_ref, w_ref, main_ref, gates_ref, su_ref, gif_ref):
    xn = _rms(x_ref[...], ln_ref[...]).astype(BF16)
    for c in range(N_MAIN // PROJ_CHUNK):
        lo = c * PROJ_CHUNK
        main_ref[:, lo:lo + PROJ_CHUNK] = _dot(xn, w_ref[:, lo:lo + PROJ_CHUNK]).astype(main_ref.dtype)
    for c in range(N_GATES // PROJ_CHUNK):
        lo = c * PROJ_CHUNK
        gates_ref[:, lo:lo + PROJ_CHUNK] = _dot(
            xn, w_ref[:, OFF_GATES + lo:OFF_GATES + lo + PROJ_CHUNK]).astype(gates_ref.dtype)
    su_ref[...] = _dot(xn, w_ref[:, OFF_SU:OFF_SU + MIX_WIDTH])
    gif_ref[...] = _dot(xn, w_ref[:, OFF_GIF:OFF_GIF + N_GIF])


def _inproj(x, ln, w, layer, tm, act_dtype):
    t = x.shape[0]
    ab = jnp.dtype(act_dtype).itemsize
    blocks = (2 * tm * D_MODEL * 4 + D_MODEL * N_IN_PADDED * 2
              + 2 * tm * (N_MAIN + N_GATES) * ab + 2 * tm * (MIX_WIDTH + N_GIF) * 4
              + 4 * tm * PROJ_CHUNK * 4)
    return pl.pallas_call(
        _inproj_kernel,
        out_shape=(jax.ShapeDtypeStruct((t, N_MAIN), act_dtype),
                   jax.ShapeDtypeStruct((t, N_GATES), act_dtype),
                   jax.ShapeDtypeStruct((t, MIX_WIDTH), F32),
                   jax.ShapeDtypeStruct((t, N_GIF), F32)),
        grid=(t // tm,),
        in_specs=[
            pl.BlockSpec((tm, D_MODEL), lambda i: (i, 0)),
            _resident((None, 1, D_MODEL), lambda i: (layer, 0, 0)),
            _resident((None, D_MODEL, N_IN_PADDED), lambda i: (layer, 0, 0)),
        ],
        out_specs=(pl.BlockSpec((tm, N_MAIN), lambda i: (i, 0)),
                   pl.BlockSpec((tm, N_GATES), lambda i: (i, 0)),
                   pl.BlockSpec((tm, MIX_WIDTH), lambda i: (i, 0)),
                   pl.BlockSpec((tm, N_GIF), lambda i: (i, 0))),
        compiler_params=_params(blocks, 1),
        name="inproj",
    )(x, ln, w)


def _merge_kernel(x_ref, r_ref, m_ref, s_ref, g_ref, wb_ref, wo_ref, o_ref):
    merged = None
    for n, b_ref in enumerate((r_ref, m_ref, s_ref)):
        up = _dot(b_ref[...].astype(BF16), wb_ref[n])
        gate = jax.nn.sigmoid(g_ref[:, n * D_MODEL:(n + 1) * D_MODEL].astype(F32))
        merged = gate * up if merged is None else merged + gate * up
    o_ref[...] = x_ref[...] + _dot(merged.astype(BF16), wo_ref[...])


def _merge(x, r, m, s, gates, wb, wo, layer, tm):
    t = x.shape[0]
    blocks = (4 * tm * D_MODEL * 4 + 6 * tm * MIX_WIDTH * 4 + 2 * tm * N_GATES * 4
              + N_BRANCH * MIX_WIDTH * D_MODEL * 2 + D_MODEL * D_MODEL * 2 + 4 * tm * D_MODEL * 4)
    row = lambda w: pl.BlockSpec((tm, w), lambda i: (i, 0))
    return pl.pallas_call(
        _merge_kernel,
        out_shape=jax.ShapeDtypeStruct((t, D_MODEL), F32),
        grid=(t // tm,),
        in_specs=[row(D_MODEL), row(MIX_WIDTH), row(MIX_WIDTH), row(MIX_WIDTH), row(N_GATES),
                  _resident((None, N_BRANCH, MIX_WIDTH, D_MODEL), lambda i: (layer, 0, 0, 0)),
                  _resident((None, D_MODEL, D_MODEL), lambda i: (layer, 0, 0))],
        out_specs=row(D_MODEL),
        compiler_params=_params(blocks, 1),
        name="merge",
    )(x, r, m, s, gates, wb, wo)


def _s5_disc_kernel(are_ref, aim_ref, ldt_ref, bre_ref, bim_ref, abr_ref, abi_ref, bbr_ref, bbi_ref):
    ar, ai = are_ref[...], aim_ref[...]
    dt = jnp.exp(ldt_ref[...])
    mag = jnp.exp(ar * dt)
    abr = mag * jnp.cos(ai * dt)
    abi = mag * jnp.sin(ai * dt)
    nr, ni = abr - 1.0, abi
    den = ar * ar + ai * ai
    cr = (nr * ar + ni * ai) / den
    ci = (ni * ar - nr * ai) / den
    br, bi = bre_ref[...], bim_ref[...]
    abr_ref[...] = abr
    abi_ref[...] = abi
    bbr_ref[...] = cr * br - ci * bi
    bbi_ref[...] = cr * bi + ci * br


def _s5_discretise(a_re, a_im, log_dt, b_re, b_im):
    rows = DEPTH * S5_N
    col = lambda a: a.reshape(rows, 1)
    ldt = jnp.broadcast_to(log_dt[:, :, None], (DEPTH, S5_GROUPS, S5_STATE))
    outs = pl.pallas_call(
        _s5_disc_kernel,
        out_shape=(jax.ShapeDtypeStruct((rows, 1), F32), jax.ShapeDtypeStruct((rows, 1), F32),
                   jax.ShapeDtypeStruct((rows, S5_GROUP), F32), jax.ShapeDtypeStruct((rows, S5_GROUP), F32)),
        name="s5_disc",
    )(col(a_re), col(a_im), col(ldt), b_re.reshape(rows, S5_GROUP), b_im.reshape(rows, S5_GROUP))
    abr, abi, bbr, bbi = outs
    a_row = jnp.concatenate([abr.reshape(DEPTH, 1, S5_N), abi.reshape(DEPTH, 1, S5_N)], axis=-1)
    return a_row, bbr.reshape(DEPTH, S5_GROUPS, S5_STATE, S5_GROUP), bbi.reshape(DEPTH, S5_GROUPS, S5_STATE, S5_GROUP)


def _block_diag_in(b):
    gh = S5_GROUPS // 2
    eye = jnp.eye(gh, dtype=b.dtype)
    m = jnp.einsum('lkgpc,gh->lkgchp', b.reshape(DEPTH, 2, gh, S5_STATE, S5_GROUP), eye)
    return m.reshape(DEPTH, 2, S5_HALF, S5_HALF_N).astype(BF16)


def _block_diag_out(c):
    gh = S5_GROUPS // 2
    eye = jnp.eye(gh, dtype=c.dtype)
    m = jnp.einsum('lkgcp,gh->lkhpgc', c.reshape(DEPTH, 2, gh, S5_GROUP, S5_STATE), eye)
    return m.reshape(DEPTH, 2, S5_HALF_N, S5_HALF).astype(BF16)


def _s5_kernel(u_ref, x0_ref, a_ref, bre_ref, bim_ref, cre_ref, cimn_ref, d_ref, wglu_ref,
               o_ref, xl_ref, xs_scr, st_scr, *, nb, tl):
    i = pl.program_id(0)

    @pl.when(i == 0)
    def _():
        st_scr[...] = x0_ref[...]

    u = u_ref[...]
    ub = u.astype(BF16)
    for k in range(2):
        uk = ub[:, k * S5_HALF:(k + 1) * S5_HALF]
        xs_scr[:, k * S5_HALF_N:(k + 1) * S5_HALF_N] = _dot(uk, bre_ref[k])
        xs_scr[:, S5_N + k * S5_HALF_N:S5_N + (k + 1) * S5_HALF_N] = _dot(uk, bim_ref[k])

    are = a_ref[:, :S5_N]
    aim = a_ref[:, S5_N:]

    def step(t, carry):
        xre, xim = carry
        r = pl.ds(pl.multiple_of(t * nb, nb), nb)
        nre = are * xre - aim * xim + xs_scr[r, :S5_N]
        nim = are * xim + aim * xre + xs_scr[r, S5_N:]
        xs_scr[r, :S5_N] = nre
        xs_scr[r, S5_N:] = nim
        return nre, nim

    xre, xim = lax.fori_loop(0, tl, step, (st_scr[:, :S5_N], st_scr[:, S5_N:]))
    st_scr[:, :S5_N] = xre
    st_scr[:, S5_N:] = xim

    ys = []
    for k in range(2):
        xr = xs_scr[:, k * S5_HALF_N:(k + 1) * S5_HALF_N].astype(BF16)
        xi = xs_scr[:, S5_N + k * S5_HALF_N:S5_N + (k + 1) * S5_HALF_N].astype(BF16)
        ys.append(_dot(xr, cre_ref[k]) + _dot(xi, cimn_ref[k]))
    y = jnp.concatenate(ys, axis=1) + d_ref[...] * u
    y = jax.nn.gelu(y)
    o_ref[...] = (y * jax.nn.sigmoid(_dot(y.astype(BF16), wglu_ref[...]))).astype(o_ref.dtype)
    xl_ref[...] = st_scr[...]


def _s5(u, x0, a_row, bre, bim, cre, cimn, d_skip, wglu, layer, nb, tl, out_dtype):
    rows = u.shape[0]
    m = nb * tl
    blocks = (2 * m * MIX_WIDTH * 4 + 2 * m * MIX_WIDTH * 4 + 4 * nb * 2 * S5_N * 4 + m * 2 * S5_N * 4
              + 4 * S5_HALF * S5_HALF_N * 2 * 2 + MIX_WIDTH * MIX_WIDTH * 2 + m * 2 * S5_N * 2 + 4 * m * MIX_WIDTH * 4)
    lay3 = lambda i: (layer, 0, 0)
    lay4 = lambda i: (layer, 0, 0, 0)
    return pl.pallas_call(
        functools.partial(_s5_kernel, nb=nb, tl=tl),
        out_shape=(jax.ShapeDtypeStruct((rows, MIX_WIDTH), out_dtype),
                   jax.ShapeDtypeStruct((nb, 2 * S5_N), F32)),
        grid=(rows // m,),
        in_specs=[
            pl.BlockSpec((m, MIX_WIDTH), lambda i: (i, 0)),
            _resident((nb, 2 * S5_N), lambda i: (0, 0)),
            _resident((None, 1, 2 * S5_N), lay3),
            _resident((None, 2, S5_HALF, S5_HALF_N), lay4),
            _resident((None, 2, S5_HALF, S5_HALF_N), lay4),
            _resident((None, 2, S5_HALF_N, S5_HALF), lay4),
            _resident((None, 2, S5_HALF_N, S5_HALF), lay4),
            _resident((None, 1, MIX_WIDTH), lay3),
            _resident((None, MIX_WIDTH, MIX_WIDTH), lay3),
        ],
        out_specs=(pl.BlockSpec((m, MIX_WIDTH), lambda i: (i, 0)),
                   pl.BlockSpec((nb, 2 * S5_N), lambda i: (0, 0))),
        scratch_shapes=[pltpu.VMEM((m, 2 * S5_N), F32), pltpu.VMEM((nb, 2 * S5_N), F32)],
        compiler_params=_params(blocks, 1),
        name="s5",
    )(u, x0, a_row, bre, bim, cre, cimn, d_skip, wglu)


def _ret_kernel(p_ref, cos_ref, sin_ref, o_ref, s_out_ref, s_scr, *, seq):
    c = CHUNK
    row = lax.broadcasted_iota(jnp.int32, (c, c), 0)
    col = lax.broadcasted_iota(jnp.int32, (c, c), 1)
    diff = (row - col).astype(F32)
    ivec = lax.broadcasted_iota(jnp.int32, (c, 1), 0).astype(F32)
    dmats = [jnp.where(diff >= 0, jnp.exp(lg * jnp.maximum(diff, 0.0)), 0.0) for lg in LOG_GAMMA]
    q_decs = [jnp.exp(lg * (ivec + 1.0)) for lg in LOG_GAMMA]
    k_decs = [jnp.exp(lg * (c - 1.0 - ivec)) for lg in LOG_GAMMA]
    s_scr[...] = jnp.zeros_like(s_scr)

    def chunk(j, carry):
        r = pl.ds(pl.multiple_of(j * c, c), c)
        cs = cos_ref[r, :]
        sn = sin_ref[r, :]
        for h in range(HEADS):
            lo = h * HEAD_DIM
            q = _rope(p_ref[r, lo:lo + HEAD_DIM].astype(F32), cs, sn)
            k = _rope(p_ref[r, MIX_WIDTH + lo:MIX_WIDTH + lo + HEAD_DIM].astype(F32), cs, sn) * QK_SCALE
            v = p_ref[r, 2 * MIX_WIDTH + lo:2 * MIX_WIDTH + lo + HEAD_DIM].astype(BF16)
            g = p_ref[r, 3 * MIX_WIDTH + lo:3 * MIX_WIDTH + lo + HEAD_DIM].astype(F32)
            att = _dot_nt(q.astype(BF16), k.astype(BF16)) * dmats[h]
            s = s_scr[h]
            o = _dot(att.astype(BF16), v) + _dot((q * q_decs[h]).astype(BF16), s.astype(BF16))
            s_scr[h] = s * math.exp(LOG_GAMMA[h] * c) + _dot_tn((k * k_decs[h]).astype(BF16), v)
            o_ref[r, lo:lo + HEAD_DIM] = (_head_norm(o) * (g * jax.nn.sigmoid(g))).astype(o_ref.dtype)
        return carry

    lax.fori_loop(0, seq // c, chunk, 0)
    s_out_ref[...] = s_scr[...]


def _retention_prompt(p3, cos, sin):
    b, seq, _ = p3.shape
    half = 4 * MIX_WIDTH
    blocks = 2 * seq * half * 2 + 2 * seq * V7X_LANES * 4 + 2 * seq * MIX_WIDTH * 2 + 3 * HEADS * HEAD_DIM * HEAD_DIM * 4
    return pl.pallas_call(
        functools.partial(_ret_kernel, seq=seq),
        out_shape=(jax.ShapeDtypeStruct((b, seq, MIX_WIDTH), BF16),
                   jax.ShapeDtypeStruct((b, HEADS, HEAD_DIM, HEAD_DIM), F32)),
        grid=(b,),
        in_specs=[pl.BlockSpec((None, seq, half), lambda i: (i, 0, 0)),
                  _resident((seq, HEAD_DIM), lambda i: (0, 0)),
                  _resident((seq, HEAD_DIM), lambda i: (0, 0))],
        out_specs=(pl.BlockSpec((None, seq, MIX_WIDTH), lambda i: (i, 0, 0)),
                   pl.BlockSpec((None, HEADS, HEAD_DIM, HEAD_DIM), lambda i: (i, 0, 0, 0))),
        scratch_shapes=[pltpu.VMEM((HEADS, HEAD_DIM, HEAD_DIM), F32)],
        compiler_params=_params(blocks, 1),
        name="retention",
    )(p3, cos, sin)


def _mlstm_kernel(p_ref, gif_ref, bias_ref, o_ref, c_out_ref, n_out_ref, m_out_ref, cn_scr, m_scr, *, seq):
    c = CHUNK
    row = lax.broadcasted_iota(jnp.int32, (c, c), 0)
    col = lax.broadcasted_iota(jnp.int32, (c, c), 1)
    causal = row >= col
    tril = jnp.where(causal, 1.0, 0.0).astype(F32)
    triu = jnp.where(row <= col, 1.0, 0.0).astype(F32)
    lane = lax.broadcasted_iota(jnp.int32, (c, V7X_LANES), 1)
    ones_col = jnp.where(lane == 0, 1.0, 0.0).astype(BF16)
    cn_scr[...] = jnp.zeros_like(cn_scr)
    m_scr[...] = jnp.zeros_like(m_scr)

    def chunk(j, carry):
        r = pl.ds(pl.multiple_of(j * c, c), c)
        fb = gif_ref[r, :] + bias_ref[...]
        gl = jnp.where(lane < HEADS, fb, jax.nn.log_sigmoid(fb))
        glt = gl.T
        cum_col = jnp.dot(tril, gl, precision=lax.Precision.HIGHEST, preferred_element_type=F32)
        cum_row = jnp.dot(glt, triu, precision=lax.Precision.HIGHEST, preferred_element_type=F32)
        for h in range(HEADS):
            lo = h * HEAD_DIM
            q = p_ref[r, lo:lo + HEAD_DIM].astype(BF16)
            kf = p_ref[r, MIX_WIDTH + lo:MIX_WIDTH + lo + HEAD_DIM].astype(F32) * QK_SCALE
            v = p_ref[r, 2 * MIX_WIDTH + lo:2 * MIX_WIDTH + lo + HEAD_DIM].astype(BF16)
            og = p_ref[r, 3 * MIX_WIDTH + lo:3 * MIX_WIDTH + lo + HEAD_DIM].astype(F32)
            b_col = cum_col[:, HEADS + h:HEADS + h + 1]
            b_row = cum_row[HEADS + h:HEADS + h + 1, :]
            i_col = gl[:, h:h + 1]
            i_row = glt[h:h + 1, :]
            m_prev = m_scr[h][:, :1]
            dlog = jnp.where(causal, b_col - b_row + i_row, -jnp.inf)
            inter = b_col + m_prev
            m_t = jnp.maximum(inter, jnp.max(dlog, axis=-1, keepdims=True))
            w_intra = jnp.exp(dlog - m_t)
            w_inter = jnp.exp(inter - m_t)
            s = _dot_nt(q, kf.astype(BF16)) * w_intra
            cn = cn_scr[h]
            qcn = _dot(q, cn.astype(BF16))
            num = _dot(s.astype(BF16), v) + w_inter * qcn[:, :HEAD_DIM]
            den = jnp.sum(s, axis=-1, keepdims=True) + w_inter * qcn[:, HEAD_DIM:HEAD_DIM + 1]
            hh = num / jnp.maximum(jnp.abs(den), jnp.exp(-m_t))
            b_last = b_col[c - 1:c, :]
            wk_log = b_last - b_col + i_col
            m_new = jnp.maximum(b_last + m_prev, jnp.max(wk_log, axis=0, keepdims=True))
            wk = jnp.exp(wk_log - m_new)
            decay = jnp.exp(b_last + m_prev - m_new)
            kw = (kf * wk).astype(BF16)
            v1 = jnp.concatenate([v, ones_col], axis=1)
            cn_scr[h] = decay * cn + _dot_tn(kw, v1)
            m_scr[h] = jnp.broadcast_to(m_new, (1, V7X_LANES))
            o_ref[r, lo:lo + HEAD_DIM] = (_head_norm(hh) * jax.nn.sigmoid(og)).astype(o_ref.dtype)
        return carry

    lax.fori_loop(0, seq // c, chunk, 0)
    for h in range(HEADS):
        cn = cn_scr[h]
        c_out_ref[h] = cn[:, :HEAD_DIM]
        n_out_ref[h:h + 1, :] = cn[:, HEAD_DIM:].T[0:1, :]
        m_out_ref[h:h + 1, :] = m_scr[h]


def _mlstm_prompt(p3, gif3, bias):
    b, seq, _ = p3.shape
    half = 4 * MIX_WIDTH
    blocks = (2 * seq * half * 2 + 2 * seq * N_GIF * 4 + 2 * seq * MIX_WIDTH * 2
              + 5 * HEADS * HEAD_DIM * 2 * HEAD_DIM * 4)
    return pl.pallas_call(
        functools.partial(_mlstm_kernel, seq=seq),
        out_shape=(jax.ShapeDtypeStruct((b, seq, MIX_WIDTH), BF16),
                   jax.ShapeDtypeStruct((b, HEADS, HEAD_DIM, HEAD_DIM), F32),
                   jax.ShapeDtypeStruct((b, HEADS, HEAD_DIM), F32),
                   jax.ShapeDtypeStruct((b, HEADS, V7X_LANES), F32)),
        grid=(b,),
        in_specs=[pl.BlockSpec((None, seq, half), lambda i: (i, 0, 1)),
                  pl.BlockSpec((None, seq, N_GIF), lambda i: (i, 0, 0)),
                  _resident((1, N_GIF), lambda i: (0, 0))],
        out_specs=(pl.BlockSpec((None, seq, MIX_WIDTH), lambda i: (i, 0, 0)),
                   pl.BlockSpec((None, HEADS, HEAD_DIM, HEAD_DIM), lambda i: (i, 0, 0, 0)),
                   pl.BlockSpec((None, HEADS, HEAD_DIM), lambda i: (i, 0, 0)),
                   pl.BlockSpec((None, HEADS, V7X_LANES), lambda i: (i, 0, 0))),
        scratch_shapes=[pltpu.VMEM((HEADS, HEAD_DIM, 2 * HEAD_DIM), F32),
                        pltpu.VMEM((HEADS, 1, V7X_LANES), F32)],
        compiler_params=_params(blocks, 1),
        name="mlstm",
    )(p3, gif3, bias)


SEQ_BLOCK = 8


def _columns(x):
    return jnp.concatenate([x, jnp.zeros((HEAD_DIM - SEQ_BLOCK, HEAD_DIM), F32)], axis=0).T


def _ret_step_kernel(p_ref, cos_ref, sin_ref, s_ref, o_ref, s_out_ref, row_scr):
    cs = cos_ref[...]
    sn = sin_ref[...]
    for h in range(HEADS):
        lo = h * HEAD_DIM
        gamma = math.exp(LOG_GAMMA[h])
        q = _rope(p_ref[:, lo:lo + HEAD_DIM], cs, sn)
        k = _rope(p_ref[:, MIX_WIDTH + lo:MIX_WIDTH + lo + HEAD_DIM], cs, sn) * QK_SCALE
        v = p_ref[:, 2 * MIX_WIDTH + lo:2 * MIX_WIDTH + lo + HEAD_DIM]
        g = p_ref[:, 3 * MIX_WIDTH + lo:3 * MIX_WIDTH + lo + HEAD_DIM]
        qt = _columns(q * gamma)
        kt = _columns(k)
        for b in range(SEQ_BLOCK):
            s = s_ref[b, h]
            row_scr[b:b + 1, :] = jnp.sum(qt[:, b:b + 1] * s, axis=0, keepdims=True)
            s_out_ref[b, h] = gamma * s + kt[:, b:b + 1] * v[b:b + 1, :]
        o = jnp.sum(q * k, axis=-1, keepdims=True) * v + row_scr[...]
        o_ref[:, lo:lo + HEAD_DIM] = _head_norm(o) * (g * jax.nn.sigmoid(g))


def _retention_step(p, cos, sin, state, layer):
    nseq = p.shape[0]
    half = 4 * MIX_WIDTH
    sblk = SEQ_BLOCK * HEADS * HEAD_DIM * HEAD_DIM * 4
    blocks = 4 * sblk + 2 * SEQ_BLOCK * half * 4 + 2 * SEQ_BLOCK * MIX_WIDTH * 4
    return pl.pallas_call(
        _ret_step_kernel,
        out_shape=(jax.ShapeDtypeStruct((nseq, MIX_WIDTH), F32),
                   jax.ShapeDtypeStruct((nseq, HEADS, HEAD_DIM, HEAD_DIM), F32)),
        grid=(nseq // SEQ_BLOCK,),
        in_specs=[pl.BlockSpec((SEQ_BLOCK, half), lambda i: (i, 0)),
                  _resident((1, HEAD_DIM), lambda i: (0, 0)),
                  _resident((1, HEAD_DIM), lambda i: (0, 0)),
                  pl.BlockSpec((None, SEQ_BLOCK, HEADS, HEAD_DIM, HEAD_DIM), lambda i: (layer, i, 0, 0, 0))],
        out_specs=(pl.BlockSpec((SEQ_BLOCK, MIX_WIDTH), lambda i: (i, 0)),
                   pl.BlockSpec((SEQ_BLOCK, HEADS, HEAD_DIM, HEAD_DIM), lambda i: (i, 0, 0, 0))),
        scratch_shapes=[pltpu.VMEM((SEQ_BLOCK, HEAD_DIM), F32)],
        compiler_params=_params(blocks, 1),
        name="retention_step",
    )(p, cos, sin, state)


def _mlstm_step_kernel(p_ref, gif_ref, bias_ref, c_ref, n_ref, m_ref, o_ref, c_out_ref, n_out_ref, m_out_ref, row_scr):
    fb = gif_ref[...] + bias_ref[...]
    lf_all = jax.nn.log_sigmoid(fb)
    lane = lax.broadcasted_iota(jnp.int32, (SEQ_BLOCK, V7X_LANES), 1)
    m_all = m_ref[...]
    m_acc = jnp.zeros((SEQ_BLOCK, V7X_LANES), F32)
    for h in range(HEADS):
        lo = h * HEAD_DIM
        q = p_ref[:, lo:lo + HEAD_DIM]
        k = p_ref[:, MIX_WIDTH + lo:MIX_WIDTH + lo + HEAD_DIM] * QK_SCALE
        v = p_ref[:, 2 * MIX_WIDTH + lo:2 * MIX_WIDTH + lo + HEAD_DIM]
        og = p_ref[:, 3 * MIX_WIDTH + lo:3 * MIX_WIDTH + lo + HEAD_DIM]
        ig = fb[:, h:h + 1]
        inter = lf_all[:, HEADS + h:HEADS + h + 1] + m_all[:, h:h + 1]
        m_t = jnp.maximum(inter, ig)
        w_intra = jnp.exp(ig - m_t)
        w_inter = jnp.exp(inter - m_t)
        nv = n_ref[:, lo:lo + HEAD_DIM]
        kw = k * w_intra
        qt = _columns(q)
        kt = _columns(kw)
        for b in range(SEQ_BLOCK):
            cm = c_ref[b, h]
            row_scr[b:b + 1, :] = jnp.sum(qt[:, b:b + 1] * cm, axis=0, keepdims=True)
            c_out_ref[b, h] = w_inter[b:b + 1, :] * cm + kt[:, b:b + 1] * v[b:b + 1, :]
        s = jnp.sum(q * k, axis=-1, keepdims=True) * w_intra
        num = s * v + w_inter * row_scr[...]
        den = s + w_inter * jnp.sum(q * nv, axis=-1, keepdims=True)
        hh = num / jnp.maximum(jnp.abs(den), jnp.exp(-m_t))
        n_out_ref[:, lo:lo + HEAD_DIM] = w_inter * nv + kw
        m_acc = jnp.where(lane == h, m_t, m_acc)
        o_ref[:, lo:lo + HEAD_DIM] = _head_norm(hh) * jax.nn.sigmoid(og)
    m_out_ref[...] = m_acc


def _mlstm_step(p, gif, bias, c_state, n_state, m_state, layer):
    nseq = p.shape[0]
    half = 4 * MIX_WIDTH
    sblk = SEQ_BLOCK * HEADS * HEAD_DIM * HEAD_DIM * 4
    blocks = 4 * sblk + 2 * SEQ_BLOCK * half * 4 + 8 * SEQ_BLOCK * MIX_WIDTH * 4
    rows = lambda w: pl.BlockSpec((SEQ_BLOCK, w), lambda i: (i, 0))
    return pl.pallas_call(
        _mlstm_step_kernel,
        out_shape=(jax.ShapeDtypeStruct((nseq, MIX_WIDTH), F32),
                   jax.ShapeDtypeStruct((nseq, HEADS, HEAD_DIM, HEAD_DIM), F32),
                   jax.ShapeDtypeStruct((nseq, MIX_WIDTH), F32),
                   jax.ShapeDtypeStruct((nseq, V7X_LANES), F32)),
        grid=(nseq // SEQ_BLOCK,),
        in_specs=[pl.BlockSpec((SEQ_BLOCK, half), lambda i: (i, 1)),
                  rows(N_GIF),
                  _resident((1, N_GIF), lambda i: (0, 0)),
                  pl.BlockSpec((None, SEQ_BLOCK, HEADS, HEAD_DIM, HEAD_DIM), lambda i: (layer, i, 0, 0, 0)),
                  rows(MIX_WIDTH),
                  rows(HEADS)],
        out_specs=(rows(MIX_WIDTH),
                   pl.BlockSpec((SEQ_BLOCK, HEADS, HEAD_DIM, HEAD_DIM), lambda i: (i, 0, 0, 0)),
                   rows(MIX_WIDTH),
                   rows(V7X_LANES)),
        scratch_shapes=[pltpu.VMEM((SEQ_BLOCK, HEAD_DIM), F32)],
        compiler_params=_params(blocks, 1),
        name="mlstm_step",
    )(p, gif, bias, c_state, n_state, m_state)


def _rope_tables(pos):
    inv = ROPE_THETA ** (-jnp.arange(0, HEAD_DIM, 2, dtype=F32) / HEAD_DIM)
    ang = pos.astype(F32)[:, None] * inv[None, :]
    cos, sin = jnp.cos(ang), jnp.sin(ang)
    return jnp.concatenate([cos, cos], axis=-1), jnp.concatenate([-sin, sin], axis=-1)


def _prepare_weights(w_ffn1_gu, w_ffn1_down, w_in, w_s5_glu, w_branch, w_out, w_ffn2_gu, w_ffn2_down):
    cast = lambda w: w.astype(BF16)
    gif = jnp.pad(w_in[..., N_MAIN:N_MAIN + 2 * HEADS], ((0, 0), (0, 0), (0, N_GIF - 2 * HEADS)))
    off_su = N_MAIN + 2 * HEADS
    w_in_p = jnp.concatenate([w_in[..., :N_MAIN], w_in[..., off_su + MIX_WIDTH:],
                              w_in[..., off_su:off_su + MIX_WIDTH], gif], axis=-1)
    return dict(gu1=cast(w_ffn1_gu), d1=cast(w_ffn1_down), w_in=cast(w_in_p), glu=cast(w_s5_glu),
                branch=cast(w_branch), out=cast(w_out), gu2=cast(w_ffn2_gu), d2=cast(w_ffn2_down))


def _run_trunk(x, nseq, seq, states, wts, norms, s5p, bias, ln_final, prompt):
    t = nseq * seq
    tm = 512 if t % 512 == 0 else t
    act_dtype = BF16 if prompt else F32
    if prompt:
        cos, sin = _rope_tables(jnp.arange(seq, dtype=jnp.int32))
    else:
        cos, sin = _rope_tables(PAST_LEN + jnp.arange(seq, dtype=jnp.int32))
    outs = ([], [], [], [], [], [])
    for l in range(DEPTH):
        x = _ffn(x, norms["ffn1"], wts["gu1"], wts["d1"], ln_final, l, tm, final=False)
        main, gates, su, gif = _inproj(x, norms["mix"], wts["w_in"], l, tm, act_dtype)
        if prompt:
            p3 = main.reshape(nseq, seq, N_MAIN)
            r_out, new_ret = _retention_prompt(p3, cos, sin)
            m_out, new_mc, new_mn, new_mm = _mlstm_prompt(p3, gif.reshape(nseq, seq, N_GIF), bias[l])
            r_out = r_out.reshape(t, MIX_WIDTH)
            m_out = m_out.reshape(t, MIX_WIDTH)
            new_mm = new_mm[:, :, 0]
            u = su.reshape(nseq, seq, MIX_WIDTH).transpose(1, 0, 2).reshape(t, MIX_WIDTH)
            x0 = jnp.zeros((nseq, 2 * S5_N), F32)
            s_out, x_last = _s5(u, x0, s5p["a"], s5p["bre"], s5p["bim"], s5p["cre"], s5p["cimn"], s5p["d"],
                                wts["glu"], l, nseq, 64, BF16)
            s_out = s_out.reshape(seq, nseq, MIX_WIDTH).transpose(1, 0, 2).reshape(t, MIX_WIDTH)
        else:
            st_ret, st_mc, st_mn, st_mm, st_sre, st_sim = states
            r_out, new_ret = _retention_step(main, cos, sin, st_ret, l)
            m_out, new_mc, new_mn, new_mm = _mlstm_step(
                main, gif, bias[l], st_mc, st_mn[l].reshape(nseq, MIX_WIDTH), st_mm[l], l)
            new_mn = new_mn.reshape(nseq, HEADS, HEAD_DIM)
            new_mm = new_mm[:, :HEADS]
            x0 = jnp.concatenate([st_sre[l].reshape(nseq, S5_N), st_sim[l].reshape(nseq, S5_N)], axis=-1)
            s_out, x_last = _s5(su, x0, s5p["a"], s5p["bre"], s5p["bim"], s5p["cre"], s5p["cimn"], s5p["d"],
                                wts["glu"], l, nseq, 1, F32)
        x = _merge(x, r_out, m_out, s_out, gates, wts["branch"], wts["out"], l, tm)
        x = _ffn(x, norms["ffn2"], wts["gu2"], wts["d2"], ln_final, l, tm, final=(l == DEPTH - 1))
        new = (new_ret, new_mc, new_mn, new_mm,
               x_last[:, :S5_N].reshape(nseq, S5_GROUPS, S5_STATE), x_last[:, S5_N:].reshape(nseq, S5_GROUPS, S5_STATE))
        for acc, s in zip(outs, new):
            acc.append(s)
    return x, tuple(jnp.stack(acc, axis=0) for acc in outs)


def kernel(x_prompt, x_sample, state_ret, state_mlstm_c, state_mlstm_n, state_mlstm_m, state_s5_re, state_s5_im, ln_ffn1, w_ffn1_gu, w_ffn1_down, ln_mix, w_in, b_gates, s5_a_re, s5_a_im, s5_b_re, s5_b_im, s5_c_re, s5_c_im, s5_d, s5_log_dt, w_s5_glu, w_branch, w_out, ln_ffn2, w_ffn2_gu, w_ffn2_down, ln_final):
    wts = _prepare_weights(w_ffn1_gu, w_ffn1_down, w_in, w_s5_glu, w_branch, w_out, w_ffn2_gu, w_ffn2_down)
    norms = dict(ffn1=ln_ffn1.reshape(DEPTH, 1, D_MODEL), mix=ln_mix.reshape(DEPTH, 1, D_MODEL),
                 ffn2=ln_ffn2.reshape(DEPTH, 1, D_MODEL))
    lnf = ln_final.reshape(1, D_MODEL)
    a_row, bbr, bbi = _s5_discretise(s5_a_re, s5_a_im, s5_log_dt, s5_b_re, s5_b_im)
    s5p = dict(a=a_row, bre=_block_diag_in(bbr), bim=_block_diag_in(bbi),
               cre=_block_diag_out(s5_c_re), cimn=_block_diag_out(-s5_c_im),
               d=s5_d.reshape(DEPTH, 1, MIX_WIDTH))
    bias = jnp.pad(b_gates, ((0, 0), (0, N_GIF - 2 * HEADS))).reshape(DEPTH, 1, N_GIF)

    pb, pl_len, _ = x_prompt.shape
    sb, sl, _ = x_sample.shape
    y_p, st_p = _run_trunk(x_prompt.reshape(pb * pl_len, D_MODEL), pb, pl_len, None, wts, norms, s5p, bias, lnf, True)
    sample_states = (state_ret, state_mlstm_c, state_mlstm_n, state_mlstm_m, state_s5_re, state_s5_im)
    y_s, st_s = _run_trunk(x_sample.reshape(sb * sl, D_MODEL), sb, sl, sample_states, wts, norms, s5p, bias, lnf, False)
    return (y_p.reshape(pb, pl_len, D_MODEL), y_s.reshape(sb, sl, D_MODEL)) + st_p + st_s
```

```python
import functools
import math

import jax
import jax.numpy as jnp
from jax import lax
from jax.experimental import pallas as pl
from jax.experimental.pallas import tpu as pltpu

F32 = jnp.float32
BF16 = jnp.bfloat16

D_MODEL = 1024
DEPTH = 4
MIX_WIDTH = D_MODEL // 2
N_BRANCH = 3
HEADS = 4
HEAD_DIM = MIX_WIDTH // HEADS
S5_GROUP = 16
S5_GROUPS = MIX_WIDTH // S5_GROUP
S5_STATE = 64
S5_N = S5_GROUPS * S5_STATE
D_FF = 2816
CHUNK = 128
PAST_LEN = 16384
ROPE_THETA = 10000.0
EPS = 1e-6

N_MAIN = 8 * MIX_WIDTH
N_GATES = N_BRANCH * D_MODEL
N_GIF = 128
N_IN_PADDED = N_MAIN + N_GATES + MIX_WIDTH + N_GIF
OFF_GATES = N_MAIN
OFF_SU = N_MAIN + N_GATES
OFF_GIF = OFF_SU + MIX_WIDTH

V7X_LANES = 128
V7X_MXU_DIM = 256
V7X_VMEM_BYTES = 64 * 1024 * 1024

FF_CHUNK = V7X_MXU_DIM
PROJ_CHUNK = 2 * V7X_MXU_DIM
S5_HALF = MIX_WIDTH // 2
S5_HALF_N = S5_N // 2
LOG_GAMMA = tuple(math.log1p(-(2.0 ** (-5.0 - h))) for h in range(HEADS))
QK_SCALE = HEAD_DIM ** -0.5


def _vmem_limit(block_bytes):
    want = int(block_bytes * 1.25) + (8 << 20)
    return min(want, V7X_VMEM_BYTES - (6 << 20))


def _params(block_bytes, n_grid):
    return pltpu.CompilerParams(
        dimension_semantics=("arbitrary",) * n_grid,
        vmem_limit_bytes=_vmem_limit(block_bytes))


def _resident(shape, index_map):
    return pl.BlockSpec(shape, index_map, pipeline_mode=pl.Buffered(1))


def _dot(a, b):
    return jnp.dot(a, b, preferred_element_type=F32)


def _dot_nt(a, b):
    return lax.dot_general(a, b, (((1,), (1,)), ((), ())), preferred_element_type=F32)


def _dot_tn(a, b):
    return lax.dot_general(a, b, (((0,), (0,)), ((), ())), preferred_element_type=F32)


def _rms(x, g):
    return x * lax.rsqrt(jnp.mean(x * x, axis=-1, keepdims=True) + EPS) * g


def _head_norm(x):
    mu = jnp.mean(x, axis=-1, keepdims=True)
    xc = x - mu
    var = jnp.mean(xc * xc, axis=-1, keepdims=True)
    return xc * lax.rsqrt(var + EPS)


def _rope(x, cos, sin_signed):
    return x * cos + pltpu.roll(x, HEAD_DIM // 2, 1) * sin_signed


def _ffn_kernel(x_ref, ln_ref, wgu_ref, wd_ref, lnf_ref, o_ref, acc_ref, *, final):
    x = x_ref[...]
    xn = _rms(x, ln_ref[...]).astype(BF16)
    for c in range(D_FF // FF_CHUNK):
        lo = c * FF_CHUNK
        g = _dot(xn, wgu_ref[:, lo:lo + FF_CHUNK])
        u = _dot(xn, wgu_ref[:, D_FF + lo:D_FF + lo + FF_CHUNK])
        a = (g * jax.nn.sigmoid(g) * u).astype(BF16)
        d = _dot(a, wd_ref[lo:lo + FF_CHUNK, :])
        if c == 0:
            acc_ref[...] = d
        else:
            acc_ref[...] += d
    y = x + 0.5 * acc_ref[...]
    if final:
        y = _rms(y, lnf_ref[...])
    o_ref[...] = y


def _ffn(x, ln, wgu, wd, lnf, layer, tm, final):
    t = x.shape[0]
    blocks = (4 * tm * D_MODEL * 4 + tm * D_MODEL * 4 + D_MODEL * 2 * D_FF * 2 + D_FF * D_MODEL * 2
              + 4 * tm * FF_CHUNK * 4)
    return pl.pallas_call(
        functools.partial(_ffn_kernel, final=final),
        out_shape=jax.ShapeDtypeStruct((t, D_MODEL), F32),
        grid=(t // tm,),
        in_specs=[
            pl.BlockSpec((tm, D_MODEL), lambda i: (i, 0)),
            _resident((None, 1, D_MODEL), lambda i: (layer, 0, 0)),
            _resident((None, D_MODEL, 2 * D_FF), lambda i: (layer, 0, 0)),
            _resident((None, D_FF, D_MODEL), lambda i: (layer, 0, 0)),
            _resident((1, D_MODEL), lambda i: (0, 0)),
        ],
        out_specs=pl.BlockSpec((tm, D_MODEL), lambda i: (i, 0)),
        scratch_shapes=[pltpu.VMEM((tm, D_MODEL), F32)],
        compiler_params=_params(blocks, 1),
        name="ffn",
    )(x, ln, wgu, wd, lnf)


def _inproj_kernel(x_ref, ln_ref, w_ref, main_ref, gates_ref, su_ref, gif_ref):
    xn = _rms(x_ref[...], ln_ref[...]).astype(BF16)
    for c in range(N_MAIN // PROJ_CHUNK):
        lo = c * PROJ_CHUNK
        main_ref[:, lo:lo + PROJ_CHUNK] = _dot(xn, w_ref[:, lo:lo + PROJ_CHUNK]).astype(main_ref.dtype)
    for c in range(N_GATES // PROJ_CHUNK):
        lo = c * PROJ_CHUNK
        gates_ref[:, lo:lo + PROJ_CHUNK] = _dot(
            xn, w_ref[:, OFF_GATES + lo:OFF_GATES + lo + PROJ_CHUNK]).astype(gates_ref.dtype)
    su_ref[...] = _dot(xn, w_ref[:, OFF_SU:OFF_SU + MIX_WIDTH])
    gif_ref[...] = _dot(xn, w_ref[:, OFF_GIF:OFF_GIF + N_GIF])


def _inproj(x, ln, w, layer, tm, act_dtype):
    t = x.shape[0]
    ab = jnp.dtype(act_dtype).itemsize
    blocks = (2 * tm * D_MODEL * 4 + D_MODEL * N_IN_PADDED * 2
              + 2 * tm * (N_MAIN + N_GATES) * ab + 2 * tm * (MIX_WIDTH + N_GIF) * 4
              + 4 * tm * PROJ_CHUNK * 4)
    return pl.pallas_call(
        _inproj_kernel,
        out_shape=(jax.ShapeDtypeStruct((t, N_MAIN), act_dtype),
                   jax.ShapeDtypeStruct((t, N_GATES), act_dtype),
                   jax.ShapeDtypeStruct((t, MIX_WIDTH), F32),
                   jax.ShapeDtypeStruct((t, N_GIF), F32)),
        grid=(t // tm,),
        in_specs=[
            pl.BlockSpec((tm, D_MODEL), lambda i: (i, 0)),
            _resident((None, 1, D_MODEL), lambda i: (layer, 0, 0)),
            _resident((None, D_MODEL, N_IN_PADDED), lambda i: (layer, 0, 0)),
        ],
        out_specs=(pl.BlockSpec((tm, N_MAIN), lambda i: (i, 0)),
                   pl.BlockSpec((tm, N_GATES), lambda i: (i, 0)),
                   pl.BlockSpec((tm, MIX_WIDTH), lambda i: (i, 0)),
                   pl.BlockSpec((tm, N_GIF), lambda i: (i, 0))),
        compiler_params=_params(blocks, 1),
        name="inproj",
    )(x, ln, w)


def _merge_kernel(x_ref, r_ref, m_ref, s_ref, g_ref, wb_ref, wo_ref, o_ref):
    merged = None
    for n, b_ref in enumerate((r_ref, m_ref, s_ref)):
        up = _dot(b_ref[...].astype(BF16), wb_ref[n])
        gate = jax.nn.sigmoid(g_ref[:, n * D_MODEL:(n + 1) * D_MODEL].astype(F32))
        merged = gate * up if merged is None else merged + gate * up
    o_ref[...] = x_ref[...] + _dot(merged.astype(BF16), wo_ref[...])


def _merge(x, r, m, s, gates, wb, wo, layer, tm):
    t = x.shape[0]
    blocks = (4 * tm * D_MODEL * 4 + 6 * tm * MIX_WIDTH * 4 + 2 * tm * N_GATES * 4
              + N_BRANCH * MIX_WIDTH * D_MODEL * 2 + D_MODEL * D_MODEL * 2 + 4 * tm * D_MODEL * 4)
    row = lambda w: pl.BlockSpec((tm, w), lambda i: (i, 0))
    return pl.pallas_call(
        _merge_kernel,
        out_shape=jax.ShapeDtypeStruct((t, D_MODEL), F32),
        grid=(t // tm,),
        in_specs=[row(D_MODEL), row(MIX_WIDTH), row(MIX_WIDTH), row(MIX_WIDTH), row(N_GATES),
                  _resident((None, N_BRANCH, MIX_WIDTH, D_MODEL), lambda i: (layer, 0, 0, 0)),
                  _resident((None, D_MODEL, D_MODEL), lambda i: (layer, 0, 0))],
        out_specs=row(D_MODEL),
        compiler_params=_params(blocks, 1),
        name="merge",
    )(x, r, m, s, gates, wb, wo)


def _s5_disc_kernel(are_ref, aim_ref, ldt_ref, bre_ref, bim_ref, abr_ref, abi_ref, bbr_ref, bbi_ref):
    ar, ai = are_ref[...], aim_ref[...]
    dt = jnp.exp(ldt_ref[...])
    mag = jnp.exp(ar * dt)
    abr = mag * jnp.cos(ai * dt)
    abi = mag * jnp.sin(ai * dt)
    nr, ni = abr - 1.0, abi
    den = ar * ar + ai * ai
    cr = (nr * ar + ni * ai) / den
    ci = (ni * ar - nr * ai) / den
    br, bi = bre_ref[...], bim_ref[...]
    abr_ref[...] = abr
    abi_ref[...] = abi
    bbr_ref[...] = cr * br - ci * bi
    bbi_ref[...] = cr * bi + ci * br


def _s5_discretise(a_re, a_im, log_dt, b_re, b_im):
    rows = DEPTH * S5_N
    col = lambda a: a.reshape(rows, 1)
    ldt = jnp.broadcast_to(log_dt[:, :, None], (DEPTH, S5_GROUPS, S5_STATE))
    outs = pl.pallas_call(
        _s5_disc_kernel,
        out_shape=(jax.ShapeDtypeStruct((rows, 1), F32), jax.ShapeDtypeStruct((rows, 1), F32),
                   jax.ShapeDtypeStruct((rows, S5_GROUP), F32), jax.ShapeDtypeStruct((rows, S5_GROUP), F32)),
        name="s5_disc",
    )(col(a_re), col(a_im), col(ldt), b_re.reshape(rows, S5_GROUP), b_im.reshape(rows, S5_GROUP))
    abr, abi, bbr, bbi = outs
    a_row = jnp.concatenate([abr.reshape(DEPTH, 1, S5_N), abi.reshape(DEPTH, 1, S5_N)], axis=-1)
    return a_row, bbr.reshape(DEPTH, S5_GROUPS, S5_STATE, S5_GROUP), bbi.reshape(DEPTH, S5_GROUPS, S5_STATE, S5_GROUP)


def _block_diag_in(b):
    gh = S5_GROUPS // 2
    eye = jnp.eye(gh, dtype=b.dtype)
    m = jnp.einsum('lkgpc,gh->lkgchp', b.reshape(DEPTH, 2, gh, S5_STATE, S5_GROUP), eye)
    return m.reshape(DEPTH, 2, S5_HALF, S5_HALF_N).astype(BF16)


def _block_diag_out(c):
    gh = S5_GROUPS // 2
    eye = jnp.eye(gh, dtype=c.dtype)
    m = jnp.einsum('lkgcp,gh->lkhpgc', c.reshape(DEPTH, 2, gh, S5_GROUP, S5_STATE), eye)
    return m.reshape(DEPTH, 2, S5_HALF_N, S5_HALF).astype(BF16)


def _s5_kernel(u_ref, x0_ref, a_ref, bre_ref, bim_ref, cre_ref, cimn_ref, d_ref, wglu_ref,
               o_ref, xl_ref, xs_scr, st_scr, *, nb, tl):
    i = pl.program_id(0)

    @pl.when(i == 0)
    def _():
        st_scr[...] = x0_ref[...]

    u = u_ref[...]
    ub = u.astype(BF16)
    for k in range(2):
        uk = ub[:, k * S5_HALF:(k + 1) * S5_HALF]
        xs_scr[:, k * S5_HALF_N:(k + 1) * S5_HALF_N] = _dot(uk, bre_ref[k])
        xs_scr[:, S5_N + k * S5_HALF_N:S5_N + (k + 1) * S5_HALF_N] = _dot(uk, bim_ref[k])

    are = a_ref[:, :S5_N]
    aim = a_ref[:, S5_N:]

    def step(t, carry):
        xre, xim = carry
        r = pl.ds(pl.multiple_of(t * nb, nb), nb)
        nre = are * xre - aim * xim + xs_scr[r, :S5_N]
        nim = are * xim + aim * xre + xs_scr[r, S5_N:]
        xs_scr[r, :S5_N] = nre
        xs_scr[r, S5_N:] = nim
        return nre, nim

    xre, xim = lax.fori_loop(0, tl, step, (st_scr[:, :S5_N], st_scr[:, S5_N:]))
    st_scr[:, :S5_N] = xre
    st_scr[:, S5_N:] = xim

    ys = []
    for k in range(2):
        xr = xs_scr[:, k * S5_HALF_N:(k + 1) * S5_HALF_N].astype(BF16)
        xi = xs_scr[:, S5_N + k * S5_HALF_N:S5_N + (k + 1) * S5_HALF_N].astype(BF16)
        ys.append(_dot(xr, cre_ref[k]) + _dot(xi, cimn_ref[k]))
    y = jnp.concatenate(ys, axis=1) + d_ref[...] * u
    y = jax.nn.gelu(y)
    o_ref[...] = (y * jax.nn.sigmoid(_dot(y.astype(BF16), wglu_ref[...]))).astype(o_ref.dtype)
    xl_ref[...] = st_scr[...]


def _s5(u, x0, a_row, bre, bim, cre, cimn, d_skip, wglu, layer, nb, tl, out_dtype):
    rows = u.shape[0]
    m = nb * tl
    blocks = (2 * m * MIX_WIDTH * 4 + 2 * m * MIX_WIDTH * 4 + 4 * nb * 2 * S5_N * 4 + m * 2 * S5_N * 4
              + 4 * S5_HALF * S5_HALF_N * 2 * 2 + MIX_WIDTH * MIX_WIDTH * 2 + m * 2 * S5_N * 2 + 4 * m * MIX_WIDTH * 4)
    lay3 = lambda i: (layer, 0, 0)
    lay4 = lambda i: (layer, 0, 0, 0)
    return pl.pallas_call(
        functools.partial(_s5_kernel, nb=nb, tl=tl),
        out_shape=(jax.ShapeDtypeStruct((rows, MIX_WIDTH), out_dtype),
                   jax.ShapeDtypeStruct((nb, 2 * S5_N), F32)),
        grid=(rows // m,),
        in_specs=[
            pl.BlockSpec((m, MIX_WIDTH), lambda i: (i, 0)),
            _resident((nb, 2 * S5_N), lambda i: (0, 0)),
            _resident((None, 1, 2 * S5_N), lay3),
            _resident((None, 2, S5_HALF, S5_HALF_N), lay4),
            _resident((None, 2, S5_HALF, S5_HALF_N), lay4),
            _resident((None, 2, S5_HALF_N, S5_HALF), lay4),
            _resident((None, 2, S5_HALF_N, S5_HALF), lay4),
            _resident((None, 1, MIX_WIDTH), lay3),
            _resident((None, MIX_WIDTH, MIX_WIDTH), lay3),
        ],
        out_specs=(pl.BlockSpec((m, MIX_WIDTH), lambda i: (i, 0)),
                   pl.BlockSpec((nb, 2 * S5_N), lambda i: (0, 0))),
        scratch_shapes=[pltpu.VMEM((m, 2 * S5_N), F32), pltpu.VMEM((nb, 2 * S5_N), F32)],
        compiler_params=_params(blocks, 1),
        name="s5",
    )(u, x0, a_row, bre, bim, cre, cimn, d_skip, wglu)


def _ret_kernel(p_ref, cos_ref, sin_ref, o_ref, s_out_ref, s_scr, *, seq):
    c = CHUNK
    heads = range(HEADS)
    row = lax.broadcasted_iota(jnp.int32, (c, c), 0)
    col = lax.broadcasted_iota(jnp.int32, (c, c), 1)
    diff = (row - col).astype(F32)
    ivec = lax.broadcasted_iota(jnp.int32, (c, HEAD_DIM), 0).astype(F32)
    dmats = [jnp.where(diff >= 0, jnp.exp(lg * jnp.maximum(diff, 0.0)), 0.0) * QK_SCALE for lg in LOG_GAMMA]
    q_decs = [jnp.exp(lg * (ivec + 1.0)) for lg in LOG_GAMMA]
    k_decs = [jnp.exp(lg * (c - 1.0 - ivec)) * QK_SCALE for lg in LOG_GAMMA]
    s_scr[...] = jnp.zeros_like(s_scr)

    def chunk(j, carry):
        r = pl.ds(pl.multiple_of(j * c, c), c)
        cs = cos_ref[r, :]
        sn = sin_ref[r, :]
        sl = lambda part, h: p_ref[r, part * MIX_WIDTH + h * HEAD_DIM:part * MIX_WIDTH + (h + 1) * HEAD_DIM]
        q = [_rope(sl(0, h).astype(F32), cs, sn) for h in heads]
        k = [_rope(sl(1, h).astype(F32), cs, sn) for h in heads]
        v = [sl(2, h).astype(BF16) for h in heads]
        s = [s_scr[h] for h in heads]
        qb = [q[h].astype(BF16) for h in heads]
        kb = [k[h].astype(BF16) for h in heads]
        att = [_dot_nt(qb[h], kb[h]) for h in heads]
        inter = [_dot((q[h] * q_decs[h]).astype(BF16), s[h].astype(BF16)) for h in heads]
        upd = [_dot_tn((k[h] * k_decs[h]).astype(BF16), v[h]) for h in heads]
        attb = [(att[h] * dmats[h]).astype(BF16) for h in heads]
        o = [_dot(attb[h], v[h]) + inter[h] for h in heads]
        for h in heads:
            s_scr[h] = s[h] * math.exp(LOG_GAMMA[h] * c) + upd[h]
        on = [_head_norm(o[h]) for h in heads]
        for h in heads:
            g = sl(3, h).astype(F32)
            o_ref[r, h * HEAD_DIM:(h + 1) * HEAD_DIM] = (on[h] * (g * jax.nn.sigmoid(g))).astype(o_ref.dtype)
        return carry

    lax.fori_loop(0, seq // c, chunk, 0, unroll=2)
    s_out_ref[...] = s_scr[...]


def _retention_prompt(p3, cos, sin):
    b, seq, _ = p3.shape
    half = 4 * MIX_WIDTH
    blocks = 2 * seq * half * 2 + 2 * seq * V7X_LANES * 4 + 2 * seq * MIX_WIDTH * 2 + 3 * HEADS * HEAD_DIM * HEAD_DIM * 4
    return pl.pallas_call(
        functools.partial(_ret_kernel, seq=seq),
        out_shape=(jax.ShapeDtypeStruct((b, seq, MIX_WIDTH), BF16),
                   jax.ShapeDtypeStruct((b, HEADS, HEAD_DIM, HEAD_DIM), F32)),
        grid=(b,),
        in_specs=[pl.BlockSpec((None, seq, half), lambda i: (i, 0, 0)),
                  _resident((seq, HEAD_DIM), lambda i: (0, 0)),
                  _resident((seq, HEAD_DIM), lambda i: (0, 0))],
        out_specs=(pl.BlockSpec((None, seq, MIX_WIDTH), lambda i: (i, 0, 0)),
                   pl.BlockSpec((None, HEADS, HEAD_DIM, HEAD_DIM), lambda i: (i, 0, 0, 0))),
        scratch_shapes=[pltpu.VMEM((HEADS, HEAD_DIM, HEAD_DIM), F32)],
        compiler_params=_params(blocks, 1),
        name="retention",
    )(p3, cos, sin)


N_REP = 3 * HEADS


def _split_dot(x, w):
    hi = x.astype(BF16)
    lo = (x - hi.astype(F32)).astype(BF16)
    return _dot(hi, w) + _dot(lo, w)


def _split_dot_tn(x, w):
    hi = x.astype(BF16)
    lo = (x - hi.astype(F32)).astype(BF16)
    return _dot_tn(hi, w) + _dot_tn(lo, w)


def _mlstm_kernel(p_ref, gif_ref, bias_ref, o_ref, c_out_ref, n_out_ref, m_out_ref,
                  cn_scr, m_scr, rows_scr, rep_scr, *, seq):
    c = CHUNK
    nch = seq // c
    nrow = 8 * nch
    heads = range(HEADS)
    neg_inf = float("-inf")
    row = lax.broadcasted_iota(jnp.int32, (c, c), 0)
    col = lax.broadcasted_iota(jnp.int32, (c, c), 1)
    causal = row >= col
    triu = jnp.where(row <= col, 1.0, 0.0).astype(BF16)
    lane = lax.broadcasted_iota(jnp.int32, (c, V7X_LANES), 1)
    ones = jnp.ones((c, V7X_LANES), BF16)
    er = lax.broadcasted_iota(jnp.int32, (16, N_REP * V7X_LANES), 0)
    ec = lax.broadcasted_iota(jnp.int32, (16, N_REP * V7X_LANES), 1)
    expand = jnp.where(ec // V7X_LANES == er, 1.0, 0.0).astype(BF16)
    cn_scr[...] = jnp.zeros_like(cn_scr)
    m_scr[...] = jnp.zeros_like(m_scr)

    pieces = []
    for j in range(nch):
        fb = gif_ref[j * c:(j + 1) * c, :] + bias_ref[...]
        gl = jnp.where(lane < HEADS, fb, jax.nn.log_sigmoid(fb))
        pieces.append(gl.T[0:8, :])
    x = jnp.concatenate(pieces, axis=0)
    cum = _split_dot(x, triu)
    sub = lax.broadcasted_iota(jnp.int32, (nrow, c), 0) % 8
    tok = lax.broadcasted_iota(jnp.int32, (nrow, c), 1)
    a = x - pltpu.roll(cum, nrow - HEADS, 0)
    d = 1
    while d < c:
        a = jnp.maximum(a, jnp.where(tok >= d, pltpu.roll(a, d, 1), neg_inf))
        d *= 2
    rows_scr[0] = jnp.where(sub < HEADS, x, cum)
    rows_scr[1] = a

    def replicate(j):
        r8 = pl.ds(pl.multiple_of(j * 8, 8), 8)
        r16 = jnp.concatenate([rows_scr[0, r8, :], rows_scr[1, r8, :]], axis=0)
        return _split_dot_tn(r16, expand)

    rep_scr[0] = replicate(0)

    def chunk(j, carry):
        r = pl.ds(pl.multiple_of(j * c, c), c)
        r8 = pl.ds(pl.multiple_of(j * 8, 8), 8)
        slot = j % 2
        rep_next = replicate(jnp.minimum(j + 1, nch - 1))
        sl = lambda part, h: p_ref[r, part * MIX_WIDTH + h * HEAD_DIM:part * MIX_WIDTH + (h + 1) * HEAD_DIM]
        blk = lambda n, h: rep_scr[slot, :, (n * HEADS + h) * V7X_LANES:(n * HEADS + h + 1) * V7X_LANES]
        y8 = rows_scr[0, r8, :]
        q = [sl(0, h).astype(BF16) for h in heads]
        k = [sl(1, h).astype(BF16) for h in heads]
        v1 = [jnp.concatenate([sl(2, h).astype(BF16), ones], axis=1) for h in heads]
        cn = [cn_scr[h] for h in heads]
        m_prev = [m_scr[h] for h in heads]
        ic = [blk(0, h) for h in heads]
        bc = [blk(1, h) for h in heads]
        cmc = [blk(2, h) for h in heads]
        qk = [_dot_nt(q[h], k[h]) for h in heads]
        qcn = [_dot(q[h], cn[h].astype(BF16)) for h in heads]
        b_last = [bc[h][c - 1:c, :] for h in heads]
        wk_log = [b_last[h] - bc[h] + ic[h] for h in heads]
        m_new = [jnp.maximum(b_last[h] + m_prev[h], jnp.max(wk_log[h], axis=0, keepdims=True)) for h in heads]
        wk = [jnp.exp(wk_log[h] - m_new[h]) * QK_SCALE for h in heads]
        decay = [jnp.exp(b_last[h] + m_prev[h] - m_new[h]) for h in heads]
        kw = [(k[h].astype(F32) * wk[h]).astype(BF16) for h in heads]
        upd = [_dot_tn(kw[h], v1[h]) for h in heads]
        m_t = [bc[h] + jnp.maximum(m_prev[h], cmc[h]) for h in heads]
        dlog = [jnp.where(causal, bc[h] - y8[HEADS + h:HEADS + h + 1, :] + y8[h:h + 1, :], neg_inf) for h in heads]
        w_intra = [jnp.exp(dlog[h] - m_t[h]) * QK_SCALE for h in heads]
        w_inter = [jnp.exp(bc[h] + m_prev[h] - m_t[h]) for h in heads]
        s = [(qk[h] * w_intra[h]).astype(BF16) for h in heads]
        sv = [_dot(s[h], v1[h]) for h in heads]
        hh = []
        for h in heads:
            num = sv[h][:, :HEAD_DIM] + w_inter[h] * qcn[h][:, :HEAD_DIM]
            den = sv[h][:, HEAD_DIM:] + w_inter[h] * qcn[h][:, HEAD_DIM:]
            hh.append(num / jnp.maximum(jnp.abs(den), jnp.exp(-m_t[h])))
        for h in heads:
            cn_scr[h] = jnp.concatenate([decay[h], decay[h]], axis=1) * cn[h] + upd[h]
            m_scr[h] = m_new[h]
        on = [_head_norm(hh[h]) for h in heads]
        for h in heads:
            o_ref[r, h * HEAD_DIM:(h + 1) * HEAD_DIM] = (on[h] * jax.nn.sigmoid(sl(3, h).astype(F32))).astype(o_ref.dtype)
        rep_scr[1 - slot] = rep_next
        return carry

    lax.fori_loop(0, nch, chunk, 0)
    for h in heads:
        cnh = cn_scr[h]
        c_out_ref[h] = cnh[:, :HEAD_DIM]
        n_out_ref[h:h + 1, :] = cnh[:, HEAD_DIM:].T[0:1, :]
        m_out_ref[h:h + 1, :] = m_scr[h]


def _mlstm_prompt(p3, gif3, bias):
    b, seq, _ = p3.shape
    half = 4 * MIX_WIDTH
    nch = seq // CHUNK
    blocks = (2 * seq * half * 2 + 2 * seq * N_GIF * 4 + 2 * seq * MIX_WIDTH * 2
              + 5 * HEADS * HEAD_DIM * 2 * HEAD_DIM * 4 + 4 * CHUNK * N_REP * V7X_LANES * 4)
    return pl.pallas_call(
        functools.partial(_mlstm_kernel, seq=seq),
        out_shape=(jax.ShapeDtypeStruct((b, seq, MIX_WIDTH), BF16),
                   jax.ShapeDtypeStruct((b, HEADS, HEAD_DIM, HEAD_DIM), F32),
                   jax.ShapeDtypeStruct((b, HEADS, HEAD_DIM), F32),
                   jax.ShapeDtypeStruct((b, HEADS, V7X_LANES), F32)),
        grid=(b,),
        in_specs=[pl.BlockSpec((None, seq, half), lambda i: (i, 0, 1)),
                  pl.BlockSpec((None, seq, N_GIF), lambda i: (i, 0, 0)),
                  _resident((1, N_GIF), lambda i: (0, 0))],
        out_specs=(pl.BlockSpec((None, seq, MIX_WIDTH), lambda i: (i, 0, 0)),
                   pl.BlockSpec((None, HEADS, HEAD_DIM, HEAD_DIM), lambda i: (i, 0, 0, 0)),
                   pl.BlockSpec((None, HEADS, HEAD_DIM), lambda i: (i, 0, 0)),
                   pl.BlockSpec((None, HEADS, V7X_LANES), lambda i: (i, 0, 0))),
        scratch_shapes=[pltpu.VMEM((HEADS, HEAD_DIM, 2 * HEAD_DIM), F32),
                        pltpu.VMEM((HEADS, 1, V7X_LANES), F32),
                        pltpu.VMEM((2, 8 * nch, CHUNK), F32),
                        pltpu.VMEM((2, CHUNK, N_REP * V7X_LANES), F32)],
        compiler_params=_params(blocks, 1),
        name="mlstm",
    )(p3, gif3, bias)


SEQ_BLOCK = 8


def _columns(x):
    return jnp.concatenate([x, jnp.zeros((HEAD_DIM - SEQ_BLOCK, HEAD_DIM), F32)], axis=0).T


def _ret_step_kernel(p_ref, cos_ref, sin_ref, s_ref, *rest):
    o_ref, s_out_ref, row_scr = rest[-3:]
    cs = cos_ref[...]
    sn = sin_ref[...]
    for h in range(HEADS):
        lo = h * HEAD_DIM
        gamma = math.exp(LOG_GAMMA[h])
        q = _rope(p_ref[:, lo:lo + HEAD_DIM], cs, sn)
        k = _rope(p_ref[:, MIX_WIDTH + lo:MIX_WIDTH + lo + HEAD_DIM], cs, sn) * QK_SCALE
        v = p_ref[:, 2 * MIX_WIDTH + lo:2 * MIX_WIDTH + lo + HEAD_DIM]
        g = p_ref[:, 3 * MIX_WIDTH + lo:3 * MIX_WIDTH + lo + HEAD_DIM]
        qt = _columns(q * gamma)
        kt = _columns(k)
        for b in range(SEQ_BLOCK):
            s = s_ref[b, h]
            row_scr[b:b + 1, :] = jnp.sum(qt[:, b:b + 1] * s, axis=0, keepdims=True)
            s_out_ref[b, h] = gamma * s + kt[:, b:b + 1] * v[b:b + 1, :]
        o = jnp.sum(q * k, axis=-1, keepdims=True) * v + row_scr[...]
        o_ref[:, lo:lo + HEAD_DIM] = _head_norm(o) * (g * jax.nn.sigmoid(g))


def _stacked_state(acc):
    if acc is None:
        return (), []
    return (acc,), [pl.BlockSpec(memory_space=pl.ANY)]


def _retention_step(p, cos, sin, state, layer, acc):
    nseq = p.shape[0]
    half = 4 * MIX_WIDTH
    sblk = SEQ_BLOCK * HEADS * HEAD_DIM * HEAD_DIM * 4
    blocks = 4 * sblk + 2 * SEQ_BLOCK * half * 4 + 2 * SEQ_BLOCK * MIX_WIDTH * 4
    state_blk = pl.BlockSpec((None, SEQ_BLOCK, HEADS, HEAD_DIM, HEAD_DIM), lambda i: (layer, i, 0, 0, 0))
    extra, extra_specs = _stacked_state(acc)
    return pl.pallas_call(
        _ret_step_kernel,
        out_shape=(jax.ShapeDtypeStruct((nseq, MIX_WIDTH), F32),
                   jax.ShapeDtypeStruct(state.shape, F32)),
        grid=(nseq // SEQ_BLOCK,),
        in_specs=[pl.BlockSpec((SEQ_BLOCK, half), lambda i: (i, 0)),
                  _resident((1, HEAD_DIM), lambda i: (0, 0)),
                  _resident((1, HEAD_DIM), lambda i: (0, 0)),
                  state_blk] + extra_specs,
        out_specs=(pl.BlockSpec((SEQ_BLOCK, MIX_WIDTH), lambda i: (i, 0)), state_blk),
        input_output_aliases={4: 1} if extra else {},
        scratch_shapes=[pltpu.VMEM((SEQ_BLOCK, HEAD_DIM), F32)],
        compiler_params=_params(blocks, 1),
        name="retention_step",
    )(p, cos, sin, state, *extra)


def _mlstm_step_kernel(p_ref, gif_ref, bias_ref, c_ref, n_ref, m_ref, *rest):
    o_ref, c_out_ref, n_out_ref, m_out_ref, row_scr = rest[-5:]
    fb = gif_ref[...] + bias_ref[...]
    lf_all = jax.nn.log_sigmoid(fb)
    lane = lax.broadcasted_iota(jnp.int32, (SEQ_BLOCK, V7X_LANES), 1)
    m_all = m_ref[...]
    m_acc = jnp.zeros((SEQ_BLOCK, V7X_LANES), F32)
    for h in range(HEADS):
        lo = h * HEAD_DIM
        q = p_ref[:, lo:lo + HEAD_DIM]
        k = p_ref[:, MIX_WIDTH + lo:MIX_WIDTH + lo + HEAD_DIM] * QK_SCALE
        v = p_ref[:, 2 * MIX_WIDTH + lo:2 * MIX_WIDTH + lo + HEAD_DIM]
        og = p_ref[:, 3 * MIX_WIDTH + lo:3 * MIX_WIDTH + lo + HEAD_DIM]
        ig = fb[:, h:h + 1]
        inter = lf_all[:, HEADS + h:HEADS + h + 1] + m_all[:, h:h + 1]
        m_t = jnp.maximum(inter, ig)
        w_intra = jnp.exp(ig - m_t)
        w_inter = jnp.exp(inter - m_t)
        nv = n_ref[:, lo:lo + HEAD_DIM]
        kw = k * w_intra
        qt = _columns(q)
        kt = _columns(kw)
        for b in range(SEQ_BLOCK):
            cm = c_ref[b, h]
            row_scr[b:b + 1, :] = jnp.sum(qt[:, b:b + 1] * cm, axis=0, keepdims=True)
            c_out_ref[b, h] = w_inter[b:b + 1, :] * cm + kt[:, b:b + 1] * v[b:b + 1, :]
        s = jnp.sum(q * k, axis=-1, keepdims=True) * w_intra
        num = s * v + w_inter * row_scr[...]
        den = s + w_inter * jnp.sum(q * nv, axis=-1, keepdims=True)
        hh = num / jnp.maximum(jnp.abs(den), jnp.exp(-m_t))
        n_out_ref[:, lo:lo + HEAD_DIM] = w_inter * nv + kw
        m_acc = jnp.where(lane == h, m_t, m_acc)
        o_ref[:, lo:lo + HEAD_DIM] = _head_norm(hh) * jax.nn.sigmoid(og)
    m_out_ref[...] = m_acc


def _mlstm_step(p, gif, bias, c_state, n_state, m_state, layer, acc):
    nseq = p.shape[0]
    half = 4 * MIX_WIDTH
    sblk = SEQ_BLOCK * HEADS * HEAD_DIM * HEAD_DIM * 4
    blocks = 4 * sblk + 2 * SEQ_BLOCK * half * 4 + 8 * SEQ_BLOCK * MIX_WIDTH * 4
    rows = lambda w: pl.BlockSpec((SEQ_BLOCK, w), lambda i: (i, 0))
    state_blk = pl.BlockSpec((None, SEQ_BLOCK, HEADS, HEAD_DIM, HEAD_DIM), lambda i: (layer, i, 0, 0, 0))
    extra, extra_specs = _stacked_state(acc)
    return pl.pallas_call(
        _mlstm_step_kernel,
        out_shape=(jax.ShapeDtypeStruct((nseq, MIX_WIDTH), F32),
                   jax.ShapeDtypeStruct(c_state.shape, F32),
                   jax.ShapeDtypeStruct((nseq, MIX_WIDTH), F32),
                   jax.ShapeDtypeStruct((nseq, V7X_LANES), F32)),
        grid=(nseq // SEQ_BLOCK,),
        in_specs=[pl.BlockSpec((SEQ_BLOCK, half), lambda i: (i, 1)),
                  rows(N_GIF),
                  _resident((1, N_GIF), lambda i: (0, 0)),
                  state_blk,
                  rows(MIX_WIDTH),
                  rows(HEADS)] + extra_specs,
        out_specs=(rows(MIX_WIDTH), state_blk, rows(MIX_WIDTH), rows(V7X_LANES)),
        input_output_aliases={6: 1} if extra else {},
        scratch_shapes=[pltpu.VMEM((SEQ_BLOCK, HEAD_DIM), F32)],
        compiler_params=_params(blocks, 1),
        name="mlstm_step",
    )(p, gif, bias, c_state, n_state, m_state, *extra)


def _rope_tables(pos):
    inv = ROPE_THETA ** (-jnp.arange(0, HEAD_DIM, 2, dtype=F32) / HEAD_DIM)
    ang = pos.astype(F32)[:, None] * inv[None, :]
    cos, sin = jnp.cos(ang), jnp.sin(ang)
    return jnp.concatenate([cos, cos], axis=-1), jnp.concatenate([-sin, sin], axis=-1)


def _prepare_weights(w_ffn1_gu, w_ffn1_down, w_in, w_s5_glu, w_branch, w_out, w_ffn2_gu, w_ffn2_down):
    cast = lambda w: w.astype(BF16)
    gif = jnp.pad(w_in[..., N_MAIN:N_MAIN + 2 * HEADS], ((0, 0), (0, 0), (0, N_GIF - 2 * HEADS)))
    off_su = N_MAIN + 2 * HEADS
    w_in_p = jnp.concatenate([w_in[..., :N_MAIN], w_in[..., off_su + MIX_WIDTH:],
                              w_in[..., off_su:off_su + MIX_WIDTH], gif], axis=-1)
    return dict(gu1=cast(w_ffn1_gu), d1=cast(w_ffn1_down), w_in=cast(w_in_p), glu=cast(w_s5_glu),
                branch=cast(w_branch), out=cast(w_out), gu2=cast(w_ffn2_gu), d2=cast(w_ffn2_down))


def _run_trunk(x, nseq, seq, states, wts, norms, s5p, bias, ln_final, prompt):
    t = nseq * seq
    tm = 512 if t % 512 == 0 else t
    act_dtype = BF16 if prompt else F32
    if prompt:
        cos, sin = _rope_tables(jnp.arange(seq, dtype=jnp.int32))
    else:
        cos, sin = _rope_tables(PAST_LEN + jnp.arange(seq, dtype=jnp.int32))
    outs = ([], [], [], [], [], [])
    ret_acc = mc_acc = None
    for l in range(DEPTH):
        x = _ffn(x, norms["ffn1"], wts["gu1"], wts["d1"], ln_final, l, tm, final=False)
        main, gates, su, gif = _inproj(x, norms["mix"], wts["w_in"], l, tm, act_dtype)
        if prompt:
            p3 = main.reshape(nseq, seq, N_MAIN)
            r_out, new_ret = _retention_prompt(p3, cos, sin)
            m_out, new_mc, new_mn, new_mm = _mlstm_prompt(p3, gif.reshape(nseq, seq, N_GIF), bias[l])
            r_out = r_out.reshape(t, MIX_WIDTH)
            m_out = m_out.reshape(t, MIX_WIDTH)
            new_mm = new_mm[:, :, 0]
            u = su.reshape(nseq, seq, MIX_WIDTH).transpose(1, 0, 2).reshape(t, MIX_WIDTH)
            x0 = jnp.zeros((nseq, 2 * S5_N), F32)
            s_out, x_last = _s5(u, x0, s5p["a"], s5p["bre"], s5p["bim"], s5p["cre"], s5p["cimn"], s5p["d"],
                                wts["glu"], l, nseq, 64, BF16)
            s_out = s_out.reshape(seq, nseq, MIX_WIDTH).transpose(1, 0, 2).reshape(t, MIX_WIDTH)
        else:
            st_ret, st_mc, st_mn, st_mm, st_sre, st_sim = states
            r_out, ret_acc = _retention_step(main, cos, sin, st_ret, l, ret_acc)
            m_out, mc_acc, new_mn, new_mm = _mlstm_step(
                main, gif, bias[l], st_mc, st_mn[l].reshape(nseq, MIX_WIDTH), st_mm[l], l, mc_acc)
            new_ret = new_mc = None
            new_mn = new_mn.reshape(nseq, HEADS, HEAD_DIM)
            new_mm = new_mm[:, :HEADS]
            x0 = jnp.concatenate([st_sre[l].reshape(nseq, S5_N), st_sim[l].reshape(nseq, S5_N)], axis=-1)
            s_out, x_last = _s5(su, x0, s5p["a"], s5p["bre"], s5p["bim"], s5p["cre"], s5p["cimn"], s5p["d"],
                                wts["glu"], l, nseq, 1, F32)
        x = _merge(x, r_out, m_out, s_out, gates, wts["branch"], wts["out"], l, tm)
        x = _ffn(x, norms["ffn2"], wts["gu2"], wts["d2"], ln_final, l, tm, final=(l == DEPTH - 1))
        new = (new_ret, new_mc, new_mn, new_mm,
               x_last[:, :S5_N].reshape(nseq, S5_GROUPS, S5_STATE), x_last[:, S5_N:].reshape(nseq, S5_GROUPS, S5_STATE))
        for acc, s in zip(outs, new):
            acc.append(s)
    stacked = [None if acc[0] is None else jnp.stack(acc, axis=0) for acc in outs]
    if not prompt:
        stacked[0], stacked[1] = ret_acc, mc_acc
    return x, tuple(stacked)


def kernel(x_prompt, x_sample, state_ret, state_mlstm_c, state_mlstm_n, state_mlstm_m, state_s5_re, state_s5_im, ln_ffn1, w_ffn1_gu, w_ffn1_down, ln_mix, w_in, b_gates, s5_a_re, s5_a_im, s5_b_re, s5_b_im, s5_c_re, s5_c_im, s5_d, s5_log_dt, w_s5_glu, w_branch, w_out, ln_ffn2, w_ffn2_gu, w_ffn2_down, ln_final):
    wts = _prepare_weights(w_ffn1_gu, w_ffn1_down, w_in, w_s5_glu, w_branch, w_out, w_ffn2_gu, w_ffn2_down)
    norms = dict(ffn1=ln_ffn1.reshape(DEPTH, 1, D_MODEL), mix=ln_mix.reshape(DEPTH, 1, D_MODEL),
                 ffn2=ln_ffn2.reshape(DEPTH, 1, D_MODEL))
    lnf = ln_final.reshape(1, D_MODEL)
    a_row, bbr, bbi = _s5_discretise(s5_a_re, s5_a_im, s5_log_dt, s5_b_re, s5_b_im)
    s5p = dict(a=a_row, bre=_block_diag_in(bbr), bim=_block_diag_in(bbi),
               cre=_block_diag_out(s5_c_re), cimn=_block_diag_out(-s5_c_im),
               d=s5_d.reshape(DEPTH, 1, MIX_WIDTH))
    bias = jnp.pad(b_gates, ((0, 0), (0, N_GIF - 2 * HEADS))).reshape(DEPTH, 1, N_GIF)

    pb, pl_len, _ = x_prompt.shape
    sb, sl, _ = x_sample.shape
    y_p, st_p = _run_trunk(x_prompt.reshape(pb * pl_len, D_MODEL), pb, pl_len, None, wts, norms, s5p, bias, lnf, True)
    sample_states = (state_ret, state_mlstm_c, state_mlstm_n, state_mlstm_m, state_s5_re, state_s5_im)
    y_s, st_s = _run_trunk(x_sample.reshape(sb * sl, D_MODEL), sb, sl, sample_states, wts, norms, s5p, bias, lnf, False)
    return (y_p.reshape(pb, pl_len, D_MODEL), y_s.reshape(sb, sl, D_MODEL)) + st_p + st_s
```

```python
import functools
import math

import jax
import jax.numpy as jnp
from jax import lax
from jax.experimental import pallas as pl
from jax.experimental.pallas import tpu as pltpu

F32 = jnp.float32
BF16 = jnp.bfloat16

D_MODEL = 1024
DEPTH = 4
MIX_WIDTH = D_MODEL // 2
N_BRANCH = 3
HEADS = 4
HEAD_DIM = MIX_WIDTH // HEADS
S5_GROUP = 16
S5_GROUPS = MIX_WIDTH // S5_GROUP
S5_STATE = 64
S5_N = S5_GROUPS * S5_STATE
D_FF = 2816
CHUNK = 128
PAST_LEN = 16384
ROPE_THETA = 10000.0
EPS = 1e-6

N_MAIN = 8 * MIX_WIDTH
N_GATES = N_BRANCH * D_MODEL
N_GIF = 128
N_IN_PADDED = N_MAIN + N_GATES + MIX_WIDTH + N_GIF
OFF_GATES = N_MAIN
OFF_SU = N_MAIN + N_GATES
OFF_GIF = OFF_SU + MIX_WIDTH

V7X_LANES = 128
V7X_MXU_DIM = 256
V7X_VMEM_BYTES = 64 * 1024 * 1024

FF_CHUNK = V7X_MXU_DIM
PROJ_CHUNK = 2 * V7X_MXU_DIM
S5_HALF = MIX_WIDTH // 2
S5_HALF_N = S5_N // 2
S5_TOKENS = 64
LOG_GAMMA = tuple(math.log1p(-(2.0 ** (-5.0 - h))) for h in range(HEADS))
QK_SCALE = HEAD_DIM ** -0.5


def _vmem_limit(block_bytes):
    want = int(block_bytes * 1.25) + (8 << 20)
    return min(want, V7X_VMEM_BYTES - (6 << 20))


def _params(block_bytes, n_grid):
    return pltpu.CompilerParams(
        dimension_semantics=("arbitrary",) * n_grid,
        vmem_limit_bytes=_vmem_limit(block_bytes))


def _resident(shape, index_map):
    return pl.BlockSpec(shape, index_map, pipeline_mode=pl.Buffered(1))


def _dot(a, b):
    return jnp.dot(a, b, preferred_element_type=F32)


def _dot_nt(a, b):
    return lax.dot_general(a, b, (((1,), (1,)), ((), ())), preferred_element_type=F32)


def _dot_tn(a, b):
    return lax.dot_general(a, b, (((0,), (0,)), ((), ())), preferred_element_type=F32)


def _rms(x, g):
    return x * lax.rsqrt(jnp.mean(x * x, axis=-1, keepdims=True) + EPS) * g


def _head_norm(x):
    mu = jnp.mean(x, axis=-1, keepdims=True)
    xc = x - mu
    var = jnp.mean(xc * xc, axis=-1, keepdims=True)
    return xc * lax.rsqrt(var + EPS)


def _rope(x, cos, sin_signed):
    return x * cos + pltpu.roll(x, HEAD_DIM // 2, 1) * sin_signed


def _ffn_kernel(x_ref, ln_ref, wgu_ref, wd_ref, lnf_ref, o_ref, acc_ref, *, final):
    x = x_ref[...]
    xn = _rms(x, ln_ref[...]).astype(BF16)
    for c in range(D_FF // FF_CHUNK):
        lo = c * FF_CHUNK
        g = _dot(xn, wgu_ref[:, lo:lo + FF_CHUNK])
        u = _dot(xn, wgu_ref[:, D_FF + lo:D_FF + lo + FF_CHUNK])
        a = (g * jax.nn.sigmoid(g) * u).astype(BF16)
        d = _dot(a, wd_ref[lo:lo + FF_CHUNK, :])
        if c == 0:
            acc_ref[...] = d
        else:
            acc_ref[...] += d
    y = x + 0.5 * acc_ref[...]
    if final:
        y = _rms(y, lnf_ref[...])
    o_ref[...] = y


def _ffn(x, ln, wgu, wd, lnf, layer, tm, final):
    t = x.shape[0]
    blocks = (4 * tm * D_MODEL * 4 + tm * D_MODEL * 4 + D_MODEL * 2 * D_FF * 2 + D_FF * D_MODEL * 2
              + 4 * tm * FF_CHUNK * 4)
    return pl.pallas_call(
        functools.partial(_ffn_kernel, final=final),
        out_shape=jax.ShapeDtypeStruct((t, D_MODEL), F32),
        grid=(t // tm,),
        in_specs=[
            pl.BlockSpec((tm, D_MODEL), lambda i: (i, 0)),
            _resident((None, 1, D_MODEL), lambda i: (layer, 0, 0)),
            _resident((None, D_MODEL, 2 * D_FF), lambda i: (layer, 0, 0)),
            _resident((None, D_FF, D_MODEL), lambda i: (layer, 0, 0)),
            _resident((1, D_MODEL), lambda i: (0, 0)),
        ],
        out_specs=pl.BlockSpec((tm, D_MODEL), lambda i: (i, 0)),
        scratch_shapes=[pltpu.VMEM((tm, D_MODEL), F32)],
        compiler_params=_params(blocks, 1),
        name="ffn",
    )(x, ln, wgu, wd, lnf)


def _inproj_kernel(x_ref, ln_ref, w_ref, main_ref, gates_ref, su_ref, gif_ref):
    xn = _rms(x_ref[...], ln_ref[...]).astype(BF16)
    for c in range(N_MAIN // PROJ_CHUNK):
        lo = c * PROJ_CHUNK
        main_ref[:, lo:lo + PROJ_CHUNK] = _dot(xn, w_ref[:, lo:lo + PROJ_CHUNK]).astype(main_ref.dtype)
    for c in range(N_GATES // PROJ_CHUNK):
        lo = c * PROJ_CHUNK
        gates_ref[:, lo:lo + PROJ_CHUNK] = _dot(
            xn, w_ref[:, OFF_GATES + lo:OFF_GATES + lo + PROJ_CHUNK]).astype(gates_ref.dtype)
    su_ref[...] = _dot(xn, w_ref[:, OFF_SU:OFF_SU + MIX_WIDTH])
    gif_ref[...] = _dot(xn, w_ref[:, OFF_GIF:OFF_GIF + N_GIF])


def _inproj(x, ln, w, layer, tm, act_dtype):
    t = x.shape[0]
    ab = jnp.dtype(act_dtype).itemsize
    blocks = (2 * tm * D_MODEL * 4 + D_MODEL * N_IN_PADDED * 2
              + 2 * tm * (N_MAIN + N_GATES) * ab + 2 * tm * (MIX_WIDTH + N_GIF) * 4
              + 4 * tm * PROJ_CHUNK * 4)
    return pl.pallas_call(
        _inproj_kernel,
        out_shape=(jax.ShapeDtypeStruct((t, N_MAIN), act_dtype),
                   jax.ShapeDtypeStruct((t, N_GATES), act_dtype),
                   jax.ShapeDtypeStruct((t, MIX_WIDTH), F32),
                   jax.ShapeDtypeStruct((t, N_GIF), F32)),
        grid=(t // tm,),
        in_specs=[
            pl.BlockSpec((tm, D_MODEL), lambda i: (i, 0)),
            _resident((None, 1, D_MODEL), lambda i: (layer, 0, 0)),
            _resident((None, D_MODEL, N_IN_PADDED), lambda i: (layer, 0, 0)),
        ],
        out_specs=(pl.BlockSpec((tm, N_MAIN), lambda i: (i, 0)),
                   pl.BlockSpec((tm, N_GATES), lambda i: (i, 0)),
                   pl.BlockSpec((tm, MIX_WIDTH), lambda i: (i, 0)),
                   pl.BlockSpec((tm, N_GIF), lambda i: (i, 0))),
        compiler_params=_params(blocks, 1),
        name="inproj",
    )(x, ln, w)


def _merge_kernel(x_ref, r_ref, m_ref, s_ref, g_ref, wb_ref, wo_ref, o_ref):
    merged = None
    for n, b_ref in enumerate((r_ref, m_ref, s_ref)):
        up = _dot(b_ref[...].astype(BF16), wb_ref[n])
        gate = jax.nn.sigmoid(g_ref[:, n * D_MODEL:(n + 1) * D_MODEL].astype(F32))
        merged = gate * up if merged is None else merged + gate * up
    o_ref[...] = x_ref[...] + _dot(merged.astype(BF16), wo_ref[...])


def _merge(x, r, m, s, gates, wb, wo, layer, tm):
    t = x.shape[0]
    blocks = (4 * tm * D_MODEL * 4 + 6 * tm * MIX_WIDTH * 4 + 2 * tm * N_GATES * 4
              + N_BRANCH * MIX_WIDTH * D_MODEL * 2 + D_MODEL * D_MODEL * 2 + 4 * tm * D_MODEL * 4)
    row = lambda w: pl.BlockSpec((tm, w), lambda i: (i, 0))
    return pl.pallas_call(
        _merge_kernel,
        out_shape=jax.ShapeDtypeStruct((t, D_MODEL), F32),
        grid=(t // tm,),
        in_specs=[row(D_MODEL), row(MIX_WIDTH), row(MIX_WIDTH), row(MIX_WIDTH), row(N_GATES),
                  _resident((None, N_BRANCH, MIX_WIDTH, D_MODEL), lambda i: (layer, 0, 0, 0)),
                  _resident((None, D_MODEL, D_MODEL), lambda i: (layer, 0, 0))],
        out_specs=row(D_MODEL),
        compiler_params=_params(blocks, 1),
        name="merge",
    )(x, r, m, s, gates, wb, wo)


def _s5_disc_kernel(are_ref, aim_ref, ldt_ref, bre_ref, bim_ref, abr_ref, abi_ref, bbr_ref, bbi_ref):
    ar, ai = are_ref[...], aim_ref[...]
    dt = jnp.exp(ldt_ref[...])
    mag = jnp.exp(ar * dt)
    abr = mag * jnp.cos(ai * dt)
    abi = mag * jnp.sin(ai * dt)
    nr, ni = abr - 1.0, abi
    den = ar * ar + ai * ai
    cr = (nr * ar + ni * ai) / den
    ci = (ni * ar - nr * ai) / den
    br, bi = bre_ref[...], bim_ref[...]
    abr_ref[...] = abr
    abi_ref[...] = abi
    bbr_ref[...] = cr * br - ci * bi
    bbi_ref[...] = cr * bi + ci * br


def _s5_discretise(a_re, a_im, log_dt, b_re, b_im):
    rows = DEPTH * S5_N
    col = lambda a: a.reshape(rows, 1)
    ldt = jnp.broadcast_to(log_dt[:, :, None], (DEPTH, S5_GROUPS, S5_STATE))
    outs = pl.pallas_call(
        _s5_disc_kernel,
        out_shape=(jax.ShapeDtypeStruct((rows, 1), F32), jax.ShapeDtypeStruct((rows, 1), F32),
                   jax.ShapeDtypeStruct((rows, S5_GROUP), F32), jax.ShapeDtypeStruct((rows, S5_GROUP), F32)),
        name="s5_disc",
    )(col(a_re), col(a_im), col(ldt), b_re.reshape(rows, S5_GROUP), b_im.reshape(rows, S5_GROUP))
    abr, abi, bbr, bbi = outs
    a_row = jnp.concatenate([abr.reshape(DEPTH, 1, S5_N), abi.reshape(DEPTH, 1, S5_N)], axis=-1)
    return a_row, bbr.reshape(DEPTH, S5_GROUPS, S5_STATE, S5_GROUP), bbi.reshape(DEPTH, S5_GROUPS, S5_STATE, S5_GROUP)


def _block_diag_in(b):
    gh = S5_GROUPS // 2
    eye = jnp.eye(gh, dtype=b.dtype)
    m = jnp.einsum('lkgpc,gh->lkgchp', b.reshape(DEPTH, 2, gh, S5_STATE, S5_GROUP), eye)
    return m.reshape(DEPTH, 2, S5_HALF, S5_HALF_N).astype(BF16)


def _block_diag_out(c):
    gh = S5_GROUPS // 2
    eye = jnp.eye(gh, dtype=c.dtype)
    m = jnp.einsum('lkgcp,gh->lkhpgc', c.reshape(DEPTH, 2, gh, S5_GROUP, S5_STATE), eye)
    return m.reshape(DEPTH, 2, S5_HALF_N, S5_HALF).astype(BF16)


def _s5_kernel(u_ref, x0_ref, a_ref, bre_ref, bim_ref, cre_ref, cimn_ref, d_ref, wglu_ref,
               o_ref, xl_ref, xs_scr, st_scr, ab_scr, *perm_scr, nb, tl):
    i = pl.program_id(0)
    m = nb * tl
    permute = tl > 1

    @pl.when(i == 0)
    def _():
        st_scr[...] = x0_ref[...]
        ab_scr[...] = jnp.broadcast_to(a_ref[...], ab_scr.shape)
        if permute:
            shift = nb.bit_length() - 1
            ro = lax.broadcasted_iota(jnp.int32, (m, m), 0)
            ci = lax.broadcasted_iota(jnp.int32, (m, m), 1)
            seq_major = lambda r: (r & (nb - 1)) * tl + (r >> shift)
            perm_scr[0][...] = jnp.where(ci == seq_major(ro), 1.0, 0.0).astype(BF16)
            perm_scr[1][...] = jnp.where(ro == seq_major(ci), 1.0, 0.0).astype(BF16)

    if permute:
        us = u_ref[...].reshape(m, MIX_WIDTH)
        hi = us.astype(BF16)
        lo = (us - hi.astype(F32)).astype(BF16)
        to_token_major = perm_scr[0][...]
        u = _dot(to_token_major, hi) + _dot(to_token_major, lo)
    else:
        u = u_ref[...]
    ub = u.astype(BF16)
    for k in range(2):
        uk = ub[:, k * S5_HALF:(k + 1) * S5_HALF]
        xs_scr[:, k * S5_HALF_N:(k + 1) * S5_HALF_N] = _dot(uk, bre_ref[k])
        xs_scr[:, S5_N + k * S5_HALF_N:S5_N + (k + 1) * S5_HALF_N] = _dot(uk, bim_ref[k])

    def step(t, carry):
        xre, xim = carry
        r = pl.ds(pl.multiple_of(t * nb, nb), nb)
        are = ab_scr[:, :S5_N]
        aim = ab_scr[:, S5_N:]
        nre = are * xre - aim * xim + xs_scr[r, :S5_N]
        nim = are * xim + aim * xre + xs_scr[r, S5_N:]
        xs_scr[r, :S5_N] = nre
        xs_scr[r, S5_N:] = nim
        return nre, nim

    xre, xim = lax.fori_loop(0, tl, step, (st_scr[:, :S5_N], st_scr[:, S5_N:]))
    st_scr[:, :S5_N] = xre
    st_scr[:, S5_N:] = xim

    ys = []
    for k in range(2):
        xr = xs_scr[:, k * S5_HALF_N:(k + 1) * S5_HALF_N].astype(BF16)
        xi = xs_scr[:, S5_N + k * S5_HALF_N:S5_N + (k + 1) * S5_HALF_N].astype(BF16)
        ys.append(_dot(xr, cre_ref[k]) + _dot(xi, cimn_ref[k]))
    y = jnp.concatenate(ys, axis=1) + d_ref[...] * u
    y = jax.nn.gelu(y)
    out = y * jax.nn.sigmoid(_dot(y.astype(BF16), wglu_ref[...]))
    if permute:
        out = _dot(perm_scr[1][...], out.astype(BF16))
        o_ref[...] = out.astype(o_ref.dtype).reshape(nb, tl, MIX_WIDTH)
    else:
        o_ref[...] = out.astype(o_ref.dtype)
    xl_ref[...] = st_scr[...]


def _s5(u, x0, a_row, bre, bim, cre, cimn, d_skip, wglu, layer, tl, out_dtype):
    nb = u.shape[0]
    m = nb * tl
    if tl > 1:
        assert out_dtype == BF16 and nb & (nb - 1) == 0
        steps = u.shape[1] // tl
        io_spec = pl.BlockSpec((nb, tl, MIX_WIDTH), lambda i: (0, i, 0))
        perm_scratch = [pltpu.VMEM((m, m), BF16), pltpu.VMEM((m, m), BF16)]
    else:
        steps = 1
        io_spec = pl.BlockSpec((nb, MIX_WIDTH), lambda i: (0, 0))
        perm_scratch = []
    blocks = (2 * m * MIX_WIDTH * 4 + 2 * m * MIX_WIDTH * 4 + 6 * nb * 2 * S5_N * 4 + m * 2 * S5_N * 4
              + 4 * S5_HALF * S5_HALF_N * 2 * 2 + MIX_WIDTH * MIX_WIDTH * 2 + m * 2 * S5_N * 2 + 4 * m * MIX_WIDTH * 4
              + len(perm_scratch) * m * m * 2)
    lay3 = lambda i: (layer, 0, 0)
    lay4 = lambda i: (layer, 0, 0, 0)
    return pl.pallas_call(
        functools.partial(_s5_kernel, nb=nb, tl=tl),
        out_shape=(jax.ShapeDtypeStruct(u.shape, out_dtype),
                   jax.ShapeDtypeStruct((nb, 2 * S5_N), F32)),
        grid=(steps,),
        in_specs=[
            io_spec,
            _resident((nb, 2 * S5_N), lambda i: (0, 0)),
            _resident((None, 1, 2 * S5_N), lay3),
            _resident((None, 2, S5_HALF, S5_HALF_N), lay4),
            _resident((None, 2, S5_HALF, S5_HALF_N), lay4),
            _resident((None, 2, S5_HALF_N, S5_HALF), lay4),
            _resident((None, 2, S5_HALF_N, S5_HALF), lay4),
            _resident((None, 1, MIX_WIDTH), lay3),
            _resident((None, MIX_WIDTH, MIX_WIDTH), lay3),
        ],
        out_specs=(io_spec, pl.BlockSpec((nb, 2 * S5_N), lambda i: (0, 0))),
        scratch_shapes=[pltpu.VMEM((m, 2 * S5_N), F32), pltpu.VMEM((nb, 2 * S5_N), F32),
                        pltpu.VMEM((nb, 2 * S5_N), F32)] + perm_scratch,
        compiler_params=_params(blocks, 1),
        name="s5",
    )(u, x0, a_row, bre, bim, cre, cimn, d_skip, wglu)


def _ret_kernel(p_ref, cos_ref, sin_ref, o_ref, s_out_ref, s_scr, *, seq):
    c = CHUNK
    heads = range(HEADS)
    row = lax.broadcasted_iota(jnp.int32, (c, c), 0)
    col = lax.broadcasted_iota(jnp.int32, (c, c), 1)
    diff = (row - col).astype(F32)
    ivec = lax.broadcasted_iota(jnp.int32, (c, HEAD_DIM), 0).astype(F32)
    dmats = [jnp.where(diff >= 0, jnp.exp(lg * jnp.maximum(diff, 0.0)), 0.0) * QK_SCALE for lg in LOG_GAMMA]
    q_decs = [jnp.exp(lg * (ivec + 1.0)) for lg in LOG_GAMMA]
    k_decs = [jnp.exp(lg * (c - 1.0 - ivec)) * QK_SCALE for lg in LOG_GAMMA]
    s_scr[...] = jnp.zeros_like(s_scr)

    def chunk(j, carry):
        r = pl.ds(pl.multiple_of(j * c, c), c)
        cs = cos_ref[r, :]
        sn = sin_ref[r, :]
        sl = lambda part, h: p_ref[r, part * MIX_WIDTH + h * HEAD_DIM:part * MIX_WIDTH + (h + 1) * HEAD_DIM]
        q = [_rope(sl(0, h).astype(F32), cs, sn) for h in heads]
        k = [_rope(sl(1, h).astype(F32), cs, sn) for h in heads]
        v = [sl(2, h).astype(BF16) for h in heads]
        s = [s_scr[h] for h in heads]
        qb = [q[h].astype(BF16) for h in heads]
        kb = [k[h].astype(BF16) for h in heads]
        att = [_dot_nt(qb[h], kb[h]) for h in heads]
        inter = [_dot((q[h] * q_decs[h]).astype(BF16), s[h].astype(BF16)) for h in heads]
        upd = [_dot_tn((k[h] * k_decs[h]).astype(BF16), v[h]) for h in heads]
        attb = [(att[h] * dmats[h]).astype(BF16) for h in heads]
        o = [_dot(attb[h], v[h]) + inter[h] for h in heads]
        for h in heads:
            s_scr[h] = s[h] * math.exp(LOG_GAMMA[h] * c) + upd[h]
        on = [_head_norm(o[h]) for h in heads]
        for h in heads:
            g = sl(3, h).astype(F32)
            o_ref[r, h * HEAD_DIM:(h + 1) * HEAD_DIM] = (on[h] * (g * jax.nn.sigmoid(g))).astype(o_ref.dtype)
        return carry

    lax.fori_loop(0, seq // c, chunk, 0, unroll=2)
    s_out_ref[...] = s_scr[...]


def _retention_prompt(p3, cos, sin):
    b, seq, _ = p3.shape
    half = 4 * MIX_WIDTH
    blocks = 2 * seq * half * 2 + 2 * seq * V7X_LANES * 4 + 2 * seq * MIX_WIDTH * 2 + 3 * HEADS * HEAD_DIM * HEAD_DIM * 4
    return pl.pallas_call(
        functools.partial(_ret_kernel, seq=seq),
        out_shape=(jax.ShapeDtypeStruct((b, seq, MIX_WIDTH), BF16),
                   jax.ShapeDtypeStruct((b, HEADS, HEAD_DIM, HEAD_DIM), F32)),
        grid=(b,),
        in_specs=[pl.BlockSpec((None, seq, half), lambda i: (i, 0, 0)),
                  _resident((seq, HEAD_DIM), lambda i: (0, 0)),
                  _resident((seq, HEAD_DIM), lambda i: (0, 0))],
        out_specs=(pl.BlockSpec((None, seq, MIX_WIDTH), lambda i: (i, 0, 0)),
                   pl.BlockSpec((None, HEADS, HEAD_DIM, HEAD_DIM), lambda i: (i, 0, 0, 0))),
        scratch_shapes=[pltpu.VMEM((HEADS, HEAD_DIM, HEAD_DIM), F32)],
        compiler_params=_params(blocks, 1),
        name="retention",
    )(p3, cos, sin)


N_REP = 3 * HEADS


def _split_dot(x, w):
    hi = x.astype(BF16)
    lo = (x - hi.astype(F32)).astype(BF16)
    return _dot(hi, w) + _dot(lo, w)


def _split_dot_tn(x, w):
    hi = x.astype(BF16)
    lo = (x - hi.astype(F32)).astype(BF16)
    return _dot_tn(hi, w) + _dot_tn(lo, w)


def _mlstm_kernel(p_ref, gif_ref, bias_ref, o_ref, c_out_ref, n_out_ref, m_out_ref,
                  cn_scr, m_scr, rows_scr, rep_scr, *, seq):
    c = CHUNK
    nch = seq // c
    nrow = 8 * nch
    heads = range(HEADS)
    neg_inf = float("-inf")
    row = lax.broadcasted_iota(jnp.int32, (c, c), 0)
    col = lax.broadcasted_iota(jnp.int32, (c, c), 1)
    causal = row >= col
    triu = jnp.where(row <= col, 1.0, 0.0).astype(BF16)
    lane = lax.broadcasted_iota(jnp.int32, (c, V7X_LANES), 1)
    ones = jnp.ones((c, V7X_LANES), BF16)
    er = lax.broadcasted_iota(jnp.int32, (16, N_REP * V7X_LANES), 0)
    ec = lax.broadcasted_iota(jnp.int32, (16, N_REP * V7X_LANES), 1)
    expand = jnp.where(ec // V7X_LANES == er, 1.0, 0.0).astype(BF16)
    cn_scr[...] = jnp.zeros_like(cn_scr)
    m_scr[...] = jnp.zeros_like(m_scr)

    pieces = []
    for j in range(nch):
        fb = gif_ref[j * c:(j + 1) * c, :] + bias_ref[...]
        gl = jnp.where(lane < HEADS, fb, jax.nn.log_sigmoid(fb))
        pieces.append(gl.T[0:8, :])
    x = jnp.concatenate(pieces, axis=0)
    cum = _split_dot(x, triu)
    sub = lax.broadcasted_iota(jnp.int32, (nrow, c), 0) % 8
    tok = lax.broadcasted_iota(jnp.int32, (nrow, c), 1)
    a = x - pltpu.roll(cum, nrow - HEADS, 0)
    d = 1
    while d < c:
        a = jnp.maximum(a, jnp.where(tok >= d, pltpu.roll(a, d, 1), neg_inf))
        d *= 2
    rows_scr[0] = jnp.where(sub < HEADS, x, cum)
    rows_scr[1] = a

    def replicate(j):
        r8 = pl.ds(pl.multiple_of(j * 8, 8), 8)
        r16 = jnp.concatenate([rows_scr[0, r8, :], rows_scr[1, r8, :]], axis=0)
        return _split_dot_tn(r16, expand)

    rep_scr[0] = replicate(0)

    def chunk(j, carry):
        r = pl.ds(pl.multiple_of(j * c, c), c)
        r8 = pl.ds(pl.multiple_of(j * 8, 8), 8)
        slot = j % 2
        rep_next = replicate(jnp.minimum(j + 1, nch - 1))
        sl = lambda part, h: p_ref[r, part * MIX_WIDTH + h * HEAD_DIM:part * MIX_WIDTH + (h + 1) * HEAD_DIM]
        blk = lambda n, h: rep_scr[slot, :, (n * HEADS + h) * V7X_LANES:(n * HEADS + h + 1) * V7X_LANES]
        y8 = rows_scr[0, r8, :]
        q = [sl(0, h).astype(BF16) for h in heads]
        k = [sl(1, h).astype(BF16) for h in heads]
        v1 = [jnp.concatenate([sl(2, h).astype(BF16), ones], axis=1) for h in heads]
        cn = [cn_scr[h] for h in heads]
        m_prev = [m_scr[h] for h in heads]
        ic = [blk(0, h) for h in heads]
        bc = [blk(1, h) for h in heads]
        cmc = [blk(2, h) for h in heads]
        qk = [_dot_nt(q[h], k[h]) for h in heads]
        qcn = [_dot(q[h], cn[h].astype(BF16)) for h in heads]
        b_last = [bc[h][c - 1:c, :] for h in heads]
        wk_log = [b_last[h] - bc[h] + ic[h] for h in heads]
        m_new = [jnp.maximum(b_last[h] + m_prev[h], jnp.max(wk_log[h], axis=0, keepdims=True)) for h in heads]
        wk = [jnp.exp(wk_log[h] - m_new[h]) * QK_SCALE for h in heads]
        decay = [jnp.exp(b_last[h] + m_prev[h] - m_new[h]) for h in heads]
        kw = [(k[h].astype(F32) * wk[h]).astype(BF16) for h in heads]
        upd = [_dot_tn(kw[h], v1[h]) for h in heads]
        m_t = [bc[h] + jnp.maximum(m_prev[h], cmc[h]) for h in heads]
        dlog = [jnp.where(causal, bc[h] - y8[HEADS + h:HEADS + h + 1, :] + y8[h:h + 1, :], neg_inf) for h in heads]
        w_intra = [jnp.exp(dlog[h] - m_t[h]) * QK_SCALE for h in heads]
        w_inter = [jnp.exp(bc[h] + m_prev[h] - m_t[h]) for h in heads]
        s = [(qk[h] * w_intra[h]).astype(BF16) for h in heads]
        sv = [_dot(s[h], v1[h]) for h in heads]
        hh = []
        for h in heads:
            num = sv[h][:, :HEAD_DIM] + w_inter[h] * qcn[h][:, :HEAD_DIM]
            den = sv[h][:, HEAD_DIM:] + w_inter[h] * qcn[h][:, HEAD_DIM:]
            hh.append(num / jnp.maximum(jnp.abs(den), jnp.exp(-m_t[h])))
        for h in heads:
            cn_scr[h] = jnp.concatenate([decay[h], decay[h]], axis=1) * cn[h] + upd[h]
            m_scr[h] = m_new[h]
        on = [_head_norm(hh[h]) for h in heads]
        for h in heads:
            o_ref[r, h * HEAD_DIM:(h + 1) * HEAD_DIM] = (on[h] * jax.nn.sigmoid(sl(3, h).astype(F32))).astype(o_ref.dtype)
        rep_scr[1 - slot] = rep_next
        return carry

    lax.fori_loop(0, nch, chunk, 0, unroll=2)
    for h in heads:
        cnh = cn_scr[h]
        c_out_ref[h] = cnh[:, :HEAD_DIM]
        n_out_ref[h:h + 1, :] = cnh[:, HEAD_DIM:].T[0:1, :]
        m_out_ref[h:h + 1, :] = m_scr[h]


def _mlstm_prompt(p3, gif3, bias):
    b, seq, _ = p3.shape
    half = 4 * MIX_WIDTH
    nch = seq // CHUNK
    blocks = (2 * seq * half * 2 + 2 * seq * N_GIF * 4 + 2 * seq * MIX_WIDTH * 2
              + 5 * HEADS * HEAD_DIM * 2 * HEAD_DIM * 4 + 4 * CHUNK * N_REP * V7X_LANES * 4)
    return pl.pallas_call(
        functools.partial(_mlstm_kernel, seq=seq),
        out_shape=(jax.ShapeDtypeStruct((b, seq, MIX_WIDTH), BF16),
                   jax.ShapeDtypeStruct((b, HEADS, HEAD_DIM, HEAD_DIM), F32),
                   jax.ShapeDtypeStruct((b, HEADS, HEAD_DIM), F32),
                   jax.ShapeDtypeStruct((b, HEADS, V7X_LANES), F32)),
        grid=(b,),
        in_specs=[pl.BlockSpec((None, seq, half), lambda i: (i, 0, 1)),
                  pl.BlockSpec((None, seq, N_GIF), lambda i: (i, 0, 0)),
                  _resident((1, N_GIF), lambda i: (0, 0))],
        out_specs=(pl.BlockSpec((None, seq, MIX_WIDTH), lambda i: (i, 0, 0)),
                   pl.BlockSpec((None, HEADS, HEAD_DIM, HEAD_DIM), lambda i: (i, 0, 0, 0)),
                   pl.BlockSpec((None, HEADS, HEAD_DIM), lambda i: (i, 0, 0)),
                   pl.BlockSpec((None, HEADS, V7X_LANES), lambda i: (i, 0, 0))),
        scratch_shapes=[pltpu.VMEM((HEADS, HEAD_DIM, 2 * HEAD_DIM), F32),
                        pltpu.VMEM((HEADS, 1, V7X_LANES), F32),
                        pltpu.VMEM((2, 8 * nch, CHUNK), F32),
                        pltpu.VMEM((2, CHUNK, N_REP * V7X_LANES), F32)],
        compiler_params=_params(blocks, 1),
        name="mlstm",
    )(p3, gif3, bias)


SEQ_BLOCK = 8


def _column_expander():
    er = lax.broadcasted_iota(jnp.int32, (SEQ_BLOCK, SEQ_BLOCK * V7X_LANES), 0)
    ec = lax.broadcasted_iota(jnp.int32, (SEQ_BLOCK, SEQ_BLOCK * V7X_LANES), 1)
    return jnp.where(ec // V7X_LANES == er, 1.0, 0.0).astype(BF16)


def _columns(x, expander):
    return _dot_tn(x.astype(BF16), expander)


def _ret_step_kernel(p_ref, cos_ref, sin_ref, s_ref, *rest, layer, first):
    o_ref, s_out_ref, row_scr = rest[-3:]
    s_out_ref = _own_layer(s_out_ref, layer, first)
    cs = cos_ref[...]
    sn = sin_ref[...]
    expander = _column_expander()
    for h in range(HEADS):
        lo = h * HEAD_DIM
        gamma = math.exp(LOG_GAMMA[h])
        q = _rope(p_ref[:, lo:lo + HEAD_DIM], cs, sn)
        k = _rope(p_ref[:, MIX_WIDTH + lo:MIX_WIDTH + lo + HEAD_DIM], cs, sn) * QK_SCALE
        v = p_ref[:, 2 * MIX_WIDTH + lo:2 * MIX_WIDTH + lo + HEAD_DIM]
        g = p_ref[:, 3 * MIX_WIDTH + lo:3 * MIX_WIDTH + lo + HEAD_DIM]
        qt = _columns(q * gamma, expander)
        kt = _columns(k, expander)
        for b in range(SEQ_BLOCK):
            s = s_ref[b, h]
            blk = slice(b * V7X_LANES, (b + 1) * V7X_LANES)
            row_scr[b:b + 1, :] = jnp.sum(qt[:, blk] * s, axis=0, keepdims=True)
            s_out_ref[b, h] = gamma * s + kt[:, blk] * v[b:b + 1, :]
        o = jnp.sum(q * k, axis=-1, keepdims=True) * v + row_scr[...]
        o_ref[:, lo:lo + HEAD_DIM] = _head_norm(o) * (g * jax.nn.sigmoid(g))


def _stacked_state(acc, layer):
    tail = (HEADS, HEAD_DIM, HEAD_DIM)
    if acc is None:
        return (), [], pl.BlockSpec((DEPTH, SEQ_BLOCK) + tail, lambda i: (0, i, 0, 0, 0))
    return (acc,), [pl.BlockSpec(memory_space=pl.ANY)], pl.BlockSpec((None, SEQ_BLOCK) + tail, lambda i: (layer, i, 0, 0, 0))


def _own_layer(state_out_ref, layer, first):
    if not first:
        return state_out_ref
    for other in range(DEPTH):
        if other != layer:
            state_out_ref[other] = jnp.zeros(state_out_ref.shape[1:], F32)
    return state_out_ref.at[layer]


def _retention_step(p, cos, sin, state, layer, acc):
    nseq = p.shape[0]
    half = 4 * MIX_WIDTH
    sblk = SEQ_BLOCK * HEADS * HEAD_DIM * HEAD_DIM * 4
    blocks = (2 + 2 * (DEPTH if acc is None else 1)) * sblk + 2 * SEQ_BLOCK * half * 4 + 2 * SEQ_BLOCK * MIX_WIDTH * 4
    state_blk = pl.BlockSpec((None, SEQ_BLOCK, HEADS, HEAD_DIM, HEAD_DIM), lambda i: (layer, i, 0, 0, 0))
    extra, extra_specs, out_state_blk = _stacked_state(acc, layer)
    return pl.pallas_call(
        functools.partial(_ret_step_kernel, layer=layer, first=acc is None),
        out_shape=(jax.ShapeDtypeStruct((nseq, MIX_WIDTH), F32),
                   jax.ShapeDtypeStruct(state.shape, F32)),
        grid=(nseq // SEQ_BLOCK,),
        in_specs=[pl.BlockSpec((SEQ_BLOCK, half), lambda i: (i, 0)),
                  _resident((1, HEAD_DIM), lambda i: (0, 0)),
                  _resident((1, HEAD_DIM), lambda i: (0, 0)),
                  state_blk] + extra_specs,
        out_specs=(pl.BlockSpec((SEQ_BLOCK, MIX_WIDTH), lambda i: (i, 0)), out_state_blk),
        input_output_aliases={4: 1} if extra else {},
        scratch_shapes=[pltpu.VMEM((SEQ_BLOCK, HEAD_DIM), F32)],
        compiler_params=_params(blocks, 1),
        name="retention_step",
    )(p, cos, sin, state, *extra)


def _mlstm_step_kernel(p_ref, gif_ref, bias_ref, c_ref, n_ref, m_ref, *rest, layer, first):
    o_ref, c_out_ref, n_out_ref, m_out_ref, row_scr = rest[-5:]
    c_out_ref = _own_layer(c_out_ref, layer, first)
    fb = gif_ref[...] + bias_ref[...]
    lf_all = jax.nn.log_sigmoid(fb)
    lane = lax.broadcasted_iota(jnp.int32, (SEQ_BLOCK, V7X_LANES), 1)
    m_all = m_ref[...]
    m_acc = jnp.zeros((SEQ_BLOCK, V7X_LANES), F32)
    expander = _column_expander()
    for h in range(HEADS):
        lo = h * HEAD_DIM
        q = p_ref[:, lo:lo + HEAD_DIM]
        k = p_ref[:, MIX_WIDTH + lo:MIX_WIDTH + lo + HEAD_DIM] * QK_SCALE
        v = p_ref[:, 2 * MIX_WIDTH + lo:2 * MIX_WIDTH + lo + HEAD_DIM]
        og = p_ref[:, 3 * MIX_WIDTH + lo:3 * MIX_WIDTH + lo + HEAD_DIM]
        ig = fb[:, h:h + 1]
        inter = lf_all[:, HEADS + h:HEADS + h + 1] + m_all[:, h:h + 1]
        m_t = jnp.maximum(inter, ig)
        w_intra = jnp.exp(ig - m_t)
        w_inter = jnp.exp(inter - m_t)
        nv = n_ref[:, lo:lo + HEAD_DIM]
        kw = k * w_intra
        qt = _columns(q, expander)
        kt = _columns(kw, expander)
        for b in range(SEQ_BLOCK):
            cm = c_ref[b, h]
            blk = slice(b * V7X_LANES, (b + 1) * V7X_LANES)
            row_scr[b:b + 1, :] = jnp.sum(qt[:, blk] * cm, axis=0, keepdims=True)
            c_out_ref[b, h] = w_inter[b:b + 1, :] * cm + kt[:, blk] * v[b:b + 1, :]
        s = jnp.sum(q * k, axis=-1, keepdims=True) * w_intra
        num = s * v + w_inter * row_scr[...]
        den = s + w_inter * jnp.sum(q * nv, axis=-1, keepdims=True)
        hh = num / jnp.maximum(jnp.abs(den), jnp.exp(-m_t))
        n_out_ref[:, lo:lo + HEAD_DIM] = w_inter * nv + kw
        m_acc = jnp.where(lane == h, m_t, m_acc)
        o_ref[:, lo:lo + HEAD_DIM] = _head_norm(hh) * jax.nn.sigmoid(og)
    m_out_ref[...] = m_acc


def _mlstm_step(p, gif, bias, c_state, n_state, m_state, layer, acc):
    nseq = p.shape[0]
    half = 4 * MIX_WIDTH
    sblk = SEQ_BLOCK * HEADS * HEAD_DIM * HEAD_DIM * 4
    blocks = (2 + 2 * (DEPTH if acc is None else 1)) * sblk + 2 * SEQ_BLOCK * half * 4 + 8 * SEQ_BLOCK * MIX_WIDTH * 4
    rows = lambda w: pl.BlockSpec((SEQ_BLOCK, w), lambda i: (i, 0))
    state_blk = pl.BlockSpec((None, SEQ_BLOCK, HEADS, HEAD_DIM, HEAD_DIM), lambda i: (layer, i, 0, 0, 0))
    extra, extra_specs, out_state_blk = _stacked_state(acc, layer)
    return pl.pallas_call(
        functools.partial(_mlstm_step_kernel, layer=layer, first=acc is None),
        out_shape=(jax.ShapeDtypeStruct((nseq, MIX_WIDTH), F32),
                   jax.ShapeDtypeStruct(c_state.shape, F32),
                   jax.ShapeDtypeStruct((nseq, MIX_WIDTH), F32),
                   jax.ShapeDtypeStruct((nseq, V7X_LANES), F32)),
        grid=(nseq // SEQ_BLOCK,),
        in_specs=[pl.BlockSpec((SEQ_BLOCK, half), lambda i: (i, 1)),
                  rows(N_GIF),
                  _resident((1, N_GIF), lambda i: (0, 0)),
                  state_blk,
                  rows(MIX_WIDTH),
                  rows(HEADS)] + extra_specs,
        out_specs=(rows(MIX_WIDTH), out_state_blk, rows(MIX_WIDTH), rows(V7X_LANES)),
        input_output_aliases={6: 1} if extra else {},
        scratch_shapes=[pltpu.VMEM((SEQ_BLOCK, HEAD_DIM), F32)],
        compiler_params=_params(blocks, 1),
        name="mlstm_step",
    )(p, gif, bias, c_state, n_state, m_state, *extra)


def _rope_tables(pos):
    inv = ROPE_THETA ** (-jnp.arange(0, HEAD_DIM, 2, dtype=F32) / HEAD_DIM)
    ang = pos.astype(F32)[:, None] * inv[None, :]
    cos, sin = jnp.cos(ang), jnp.sin(ang)
    return jnp.concatenate([cos, cos], axis=-1), jnp.concatenate([-sin, sin], axis=-1)


def _prepare_weights(w_ffn1_gu, w_ffn1_down, w_in, w_s5_glu, w_branch, w_out, w_ffn2_gu, w_ffn2_down):
    cast = lambda w: w.astype(BF16)
    gif = jnp.pad(w_in[..., N_MAIN:N_MAIN + 2 * HEADS], ((0, 0), (0, 0), (0, N_GIF - 2 * HEADS)))
    off_su = N_MAIN + 2 * HEADS
    w_in_p = jnp.concatenate([w_in[..., :N_MAIN], w_in[..., off_su + MIX_WIDTH:],
                              w_in[..., off_su:off_su + MIX_WIDTH], gif], axis=-1)
    return dict(gu1=cast(w_ffn1_gu), d1=cast(w_ffn1_down), w_in=cast(w_in_p), glu=cast(w_s5_glu),
                branch=cast(w_branch), out=cast(w_out), gu2=cast(w_ffn2_gu), d2=cast(w_ffn2_down))


def _run_trunk(x, nseq, seq, states, wts, norms, s5p, bias, ln_final, prompt):
    t = nseq * seq
    tm = 512 if t % 512 == 0 else t
    act_dtype = BF16 if prompt else F32
    if prompt:
        cos, sin = _rope_tables(jnp.arange(seq, dtype=jnp.int32))
    else:
        cos, sin = _rope_tables(PAST_LEN + jnp.arange(seq, dtype=jnp.int32))
    outs = ([], [], [], [], [], [])
    ret_acc = mc_acc = None
    for l in range(DEPTH):
        x = _ffn(x, norms["ffn1"], wts["gu1"], wts["d1"], ln_final, l, tm, final=False)
        main, gates, su, gif = _inproj(x, norms["mix"], wts["w_in"], l, tm, act_dtype)
        if prompt:
            p3 = main.reshape(nseq, seq, N_MAIN)
            r_out, new_ret = _retention_prompt(p3, cos, sin)
            m_out, new_mc, new_mn, new_mm = _mlstm_prompt(p3, gif.reshape(nseq, seq, N_GIF), bias[l])
            r_out = r_out.reshape(t, MIX_WIDTH)
            m_out = m_out.reshape(t, MIX_WIDTH)
            new_mm = new_mm[:, :, 0]
            x0 = jnp.zeros((nseq, 2 * S5_N), F32)
            s_out, x_last = _s5(su.reshape(nseq, seq, MIX_WIDTH), x0, s5p["a"], s5p["bre"], s5p["bim"], s5p["cre"],
                                s5p["cimn"], s5p["d"], wts["glu"], l, S5_TOKENS, BF16)
            s_out = s_out.reshape(t, MIX_WIDTH)
        else:
            st_ret, st_mc, st_mn, st_mm, st_sre, st_sim = states
            r_out, ret_acc = _retention_step(main, cos, sin, st_ret, l, ret_acc)
            m_out, mc_acc, new_mn, new_mm = _mlstm_step(
                main, gif, bias[l], st_mc, st_mn[l].reshape(nseq, MIX_WIDTH), st_mm[l], l, mc_acc)
            new_ret = new_mc = None
            new_mn = new_mn.reshape(nseq, HEADS, HEAD_DIM)
            new_mm = new_mm[:, :HEADS]
            x0 = jnp.concatenate([st_sre[l].reshape(nseq, S5_N), st_sim[l].reshape(nseq, S5_N)], axis=-1)
            s_out, x_last = _s5(su, x0, s5p["a"], s5p["bre"], s5p["bim"], s5p["cre"], s5p["cimn"], s5p["d"],
                                wts["glu"], l, 1, F32)
        x = _merge(x, r_out, m_out, s_out, gates, wts["branch"], wts["out"], l, tm)
        x = _ffn(x, norms["ffn2"], wts["gu2"], wts["d2"], ln_final, l, tm, final=(l == DEPTH - 1))
        new = (new_ret, new_mc, new_mn, new_mm,
               x_last[:, :S5_N].reshape(nseq, S5_GROUPS, S5_STATE), x_last[:, S5_N:].reshape(nseq, S5_GROUPS, S5_STATE))
        for acc, s in zip(outs, new):
            acc.append(s)
    stacked = [None if acc[0] is None else jnp.stack(acc, axis=0) for acc in outs]
    if not prompt:
        stacked[0], stacked[1] = ret_acc, mc_acc
    return x, tuple(stacked)


def kernel(x_prompt, x_sample, state_ret, state_mlstm_c, state_mlstm_n, state_mlstm_m, state_s5_re, state_s5_im, ln_ffn1, w_ffn1_gu, w_ffn1_down, ln_mix, w_in, b_gates, s5_a_re, s5_a_im, s5_b_re, s5_b_im, s5_c_re, s5_c_im, s5_d, s5_log_dt, w_s5_glu, w_branch, w_out, ln_ffn2, w_ffn2_gu, w_ffn2_down, ln_final):
    wts = _prepare_weights(w_ffn1_gu, w_ffn1_down, w_in, w_s5_glu, w_branch, w_out, w_ffn2_gu, w_ffn2_down)
    norms = dict(ffn1=ln_ffn1.reshape(DEPTH, 1, D_MODEL), mix=ln_mix.reshape(DEPTH, 1, D_MODEL),
                 ffn2=ln_ffn2.reshape(DEPTH, 1, D_MODEL))
    lnf = ln_final.reshape(1, D_MODEL)
    a_row, bbr, bbi = _s5_discretise(s5_a_re, s5_a_im, s5_log_dt, s5_b_re, s5_b_im)
    s5p = dict(a=a_row, bre=_block_diag_in(bbr), bim=_block_diag_in(bbi),
               cre=_block_diag_out(s5_c_re), cimn=_block_diag_out(-s5_c_im),
               d=s5_d.reshape(DEPTH, 1, MIX_WIDTH))
    bias = jnp.pad(b_gates, ((0, 0), (0, N_GIF - 2 * HEADS))).reshape(DEPTH, 1, N_GIF)

    pb, pl_len, _ = x_prompt.shape
    sb, sl, _ = x_sample.shape
    y_p, st_p = _run_trunk(x_prompt.reshape(pb * pl_len, D_MODEL), pb, pl_len, None, wts, norms, s5p, bias, lnf, True)
    sample_states = (state_ret, state_mlstm_c, state_mlstm_n, state_mlstm_m, state_s5_re, state_s5_im)
    y_s, st_s = _run_trunk(x_sample.reshape(sb * sl, D_MODEL), sb, sl, sample_states, wts, norms, s5p, bias, lnf, False)
    return (y_p.reshape(pb, pl_len, D_MODEL), y_s.reshape(sb, sl, D_MODEL)) + st_p + st_s
```

```python
import functools
import math

import jax
import jax.numpy as jnp
from jax import lax
from jax.experimental import pallas as pl
from jax.experimental.pallas import tpu as pltpu

F32 = jnp.float32
BF16 = jnp.bfloat16

D_MODEL = 1024
DEPTH = 4
MIX_WIDTH = D_MODEL // 2
N_BRANCH = 3
HEADS = 4
HEAD_DIM = MIX_WIDTH // HEADS
S5_GROUP = 16
S5_GROUPS = MIX_WIDTH // S5_GROUP
S5_STATE = 64
S5_N = S5_GROUPS * S5_STATE
D_FF = 2816
CHUNK = 128
PAST_LEN = 16384
ROPE_THETA = 10000.0
EPS = 1e-6

N_MAIN = 8 * MIX_WIDTH
N_GATES = N_BRANCH * D_MODEL
N_GIF = 128
N_IN_PADDED = N_MAIN + N_GATES + MIX_WIDTH + N_GIF
OFF_GATES = N_MAIN
OFF_SU = N_MAIN + N_GATES
OFF_GIF = OFF_SU + MIX_WIDTH

V7X_LANES = 128
V7X_MXU_DIM = 256
V7X_VMEM_BYTES = 64 * 1024 * 1024

FF_CHUNK = V7X_MXU_DIM
PROJ_CHUNK = 2 * V7X_MXU_DIM
S5_HALF = MIX_WIDTH // 2
S5_HALF_N = S5_N // 2
S5_TOKENS = 64
LOG_GAMMA = tuple(math.log1p(-(2.0 ** (-5.0 - h))) for h in range(HEADS))
QK_SCALE = HEAD_DIM ** -0.5


def _vmem_limit(block_bytes):
    want = int(block_bytes * 1.25) + (8 << 20)
    return min(want, V7X_VMEM_BYTES - (6 << 20))


def _params(block_bytes, n_grid):
    return pltpu.CompilerParams(
        dimension_semantics=("arbitrary",) * n_grid,
        vmem_limit_bytes=_vmem_limit(block_bytes))


def _resident(shape, index_map):
    return pl.BlockSpec(shape, index_map, pipeline_mode=pl.Buffered(1))


def _dot(a, b):
    return jnp.dot(a, b, preferred_element_type=F32)


def _dot_nt(a, b):
    return lax.dot_general(a, b, (((1,), (1,)), ((), ())), preferred_element_type=F32)


def _dot_tn(a, b):
    return lax.dot_general(a, b, (((0,), (0,)), ((), ())), preferred_element_type=F32)


def _rms(x, g):
    return x * lax.rsqrt(jnp.mean(x * x, axis=-1, keepdims=True) + EPS) * g


def _head_norm(x):
    mu = jnp.mean(x, axis=-1, keepdims=True)
    xc = x - mu
    var = jnp.mean(xc * xc, axis=-1, keepdims=True)
    return xc * lax.rsqrt(var + EPS)


def _rope(x, cos, sin_signed):
    return x * cos + pltpu.roll(x, HEAD_DIM // 2, 1) * sin_signed


def _ffn_kernel(x_ref, ln_ref, wgu_ref, wd_ref, lnf_ref, o_ref, acc_ref, *, final):
    x = x_ref[...]
    xn = _rms(x, ln_ref[...]).astype(BF16)
    for c in range(D_FF // FF_CHUNK):
        lo = c * FF_CHUNK
        g = _dot(xn, wgu_ref[:, lo:lo + FF_CHUNK])
        u = _dot(xn, wgu_ref[:, D_FF + lo:D_FF + lo + FF_CHUNK])
        a = (g * jax.nn.sigmoid(g) * u).astype(BF16)
        d = _dot(a, wd_ref[lo:lo + FF_CHUNK, :])
        if c == 0:
            acc_ref[...] = d
        else:
            acc_ref[...] += d
    y = x + 0.5 * acc_ref[...]
    if final:
        y = _rms(y, lnf_ref[...])
    o_ref[...] = y


def _ffn(x, ln, wgu, wd, lnf, layer, tm, final):
    t = x.shape[0]
    blocks = (4 * tm * D_MODEL * 4 + tm * D_MODEL * 4 + D_MODEL * 2 * D_FF * 2 + D_FF * D_MODEL * 2
              + 4 * tm * FF_CHUNK * 4)
    return pl.pallas_call(
        functools.partial(_ffn_kernel, final=final),
        out_shape=jax.ShapeDtypeStruct((t, D_MODEL), F32),
        grid=(t // tm,),
        in_specs=[
            pl.BlockSpec((tm, D_MODEL), lambda i: (i, 0)),
            _resident((None, 1, D_MODEL), lambda i: (layer, 0, 0)),
            _resident((None, D_MODEL, 2 * D_FF), lambda i: (layer, 0, 0)),
            _resident((None, D_FF, D_MODEL), lambda i: (layer, 0, 0)),
            _resident((1, D_MODEL), lambda i: (0, 0)),
        ],
        out_specs=pl.BlockSpec((tm, D_MODEL), lambda i: (i, 0)),
        scratch_shapes=[pltpu.VMEM((tm, D_MODEL), F32)],
        compiler_params=_params(blocks, 1),
        name="ffn",
    )(x, ln, wgu, wd, lnf)


def _inproj_kernel(x_ref, ln_ref, w_ref, main_ref, gates_ref, su_ref, gif_ref):
    xn = _rms(x_ref[...], ln_ref[...]).astype(BF16)
    for c in range(N_MAIN // PROJ_CHUNK):
        lo = c * PROJ_CHUNK
        main_ref[:, lo:lo + PROJ_CHUNK] = _dot(xn, w_ref[:, lo:lo + PROJ_CHUNK]).astype(main_ref.dtype)
    for c in range(N_GATES // PROJ_CHUNK):
        lo = c * PROJ_CHUNK
        gates_ref[:, lo:lo + PROJ_CHUNK] = _dot(
            xn, w_ref[:, OFF_GATES + lo:OFF_GATES + lo + PROJ_CHUNK]).astype(gates_ref.dtype)
    su_ref[...] = _dot(xn, w_ref[:, OFF_SU:OFF_SU + MIX_WIDTH])
    gif_ref[...] = _dot(xn, w_ref[:, OFF_GIF:OFF_GIF + N_GIF])


def _inproj(x, ln, w, layer, tm, act_dtype):
    t = x.shape[0]
    ab = jnp.dtype(act_dtype).itemsize
    blocks = (2 * tm * D_MODEL * 4 + D_MODEL * N_IN_PADDED * 2
              + 2 * tm * (N_MAIN + N_GATES) * ab + 2 * tm * (MIX_WIDTH + N_GIF) * 4
              + 4 * tm * PROJ_CHUNK * 4)
    return pl.pallas_call(
        _inproj_kernel,
        out_shape=(jax.ShapeDtypeStruct((t, N_MAIN), act_dtype),
                   jax.ShapeDtypeStruct((t, N_GATES), act_dtype),
                   jax.ShapeDtypeStruct((t, MIX_WIDTH), F32),
                   jax.ShapeDtypeStruct((t, N_GIF), F32)),
        grid=(t // tm,),
        in_specs=[
            pl.BlockSpec((tm, D_MODEL), lambda i: (i, 0)),
            _resident((None, 1, D_MODEL), lambda i: (layer, 0, 0)),
            _resident((None, D_MODEL, N_IN_PADDED), lambda i: (layer, 0, 0)),
        ],
        out_specs=(pl.BlockSpec((tm, N_MAIN), lambda i: (i, 0)),
                   pl.BlockSpec((tm, N_GATES), lambda i: (i, 0)),
                   pl.BlockSpec((tm, MIX_WIDTH), lambda i: (i, 0)),
                   pl.BlockSpec((tm, N_GIF), lambda i: (i, 0))),
        compiler_params=_params(blocks, 1),
        name="inproj",
    )(x, ln, w)


def _merge_kernel(x_ref, r_ref, m_ref, s_ref, g_ref, wb_ref, wo_ref, o_ref):
    merged = None
    for n, b_ref in enumerate((r_ref, m_ref, s_ref)):
        up = _dot(b_ref[...].astype(BF16), wb_ref[n])
        gate = jax.nn.sigmoid(g_ref[:, n * D_MODEL:(n + 1) * D_MODEL].astype(F32))
        merged = gate * up if merged is None else merged + gate * up
    o_ref[...] = x_ref[...] + _dot(merged.astype(BF16), wo_ref[...])


def _merge(x, r, m, s, gates, wb, wo, layer, tm):
    t = x.shape[0]
    blocks = (4 * tm * D_MODEL * 4 + 6 * tm * MIX_WIDTH * 4 + 2 * tm * N_GATES * 4
              + N_BRANCH * MIX_WIDTH * D_MODEL * 2 + D_MODEL * D_MODEL * 2 + 4 * tm * D_MODEL * 4)
    row = lambda w: pl.BlockSpec((tm, w), lambda i: (i, 0))
    if s.ndim == 3:
        assert s.shape[1] % tm == 0
        per_seq = s.shape[1] // tm
        s_spec = pl.BlockSpec((None, tm, MIX_WIDTH), lambda i: (i // per_seq, i % per_seq, 0))
    else:
        s_spec = row(MIX_WIDTH)
    return pl.pallas_call(
        _merge_kernel,
        out_shape=jax.ShapeDtypeStruct((t, D_MODEL), F32),
        grid=(t // tm,),
        in_specs=[row(D_MODEL), row(MIX_WIDTH), row(MIX_WIDTH), s_spec, row(N_GATES),
                  _resident((None, N_BRANCH, MIX_WIDTH, D_MODEL), lambda i: (layer, 0, 0, 0)),
                  _resident((None, D_MODEL, D_MODEL), lambda i: (layer, 0, 0))],
        out_specs=row(D_MODEL),
        compiler_params=_params(blocks, 1),
        name="merge",
    )(x, r, m, s, gates, wb, wo)


def _s5_disc_kernel(are_ref, aim_ref, ldt_ref, bre_ref, bim_ref, abr_ref, abi_ref, bbr_ref, bbi_ref):
    ar, ai = are_ref[...], aim_ref[...]
    dt = jnp.exp(ldt_ref[...])
    mag = jnp.exp(ar * dt)
    abr = mag * jnp.cos(ai * dt)
    abi = mag * jnp.sin(ai * dt)
    nr, ni = abr - 1.0, abi
    den = ar * ar + ai * ai
    cr = (nr * ar + ni * ai) / den
    ci = (ni * ar - nr * ai) / den
    br, bi = bre_ref[...], bim_ref[...]
    abr_ref[...] = abr
    abi_ref[...] = abi
    bbr_ref[...] = cr * br - ci * bi
    bbi_ref[...] = cr * bi + ci * br


def _s5_discretise(a_re, a_im, log_dt, b_re, b_im):
    rows = DEPTH * S5_N
    col = lambda a: a.reshape(rows, 1)
    ldt = jnp.broadcast_to(log_dt[:, :, None], (DEPTH, S5_GROUPS, S5_STATE))
    outs = pl.pallas_call(
        _s5_disc_kernel,
        out_shape=(jax.ShapeDtypeStruct((rows, 1), F32), jax.ShapeDtypeStruct((rows, 1), F32),
                   jax.ShapeDtypeStruct((rows, S5_GROUP), F32), jax.ShapeDtypeStruct((rows, S5_GROUP), F32)),
        name="s5_disc",
    )(col(a_re), col(a_im), col(ldt), b_re.reshape(rows, S5_GROUP), b_im.reshape(rows, S5_GROUP))
    abr, abi, bbr, bbi = outs
    a_row = jnp.concatenate([abr.reshape(DEPTH, 1, S5_N), abi.reshape(DEPTH, 1, S5_N)], axis=-1)
    return a_row, bbr.reshape(DEPTH, S5_GROUPS, S5_STATE, S5_GROUP), bbi.reshape(DEPTH, S5_GROUPS, S5_STATE, S5_GROUP)


def _block_diag_in(b):
    gh = S5_GROUPS // 2
    eye = jnp.eye(gh, dtype=b.dtype)
    m = jnp.einsum('lkgpc,gh->lkgchp', b.reshape(DEPTH, 2, gh, S5_STATE, S5_GROUP), eye)
    return m.reshape(DEPTH, 2, S5_HALF, S5_HALF_N).astype(BF16)


def _block_diag_out(c):
    gh = S5_GROUPS // 2
    eye = jnp.eye(gh, dtype=c.dtype)
    m = jnp.einsum('lkgcp,gh->lkhpgc', c.reshape(DEPTH, 2, gh, S5_GROUP, S5_STATE), eye)
    return m.reshape(DEPTH, 2, S5_HALF_N, S5_HALF).astype(BF16)


def _s5_project_in(u, xs_ref, bre_ref, bim_ref):
    ub = u.astype(BF16)
    for k in range(2):
        uk = ub[:, k * S5_HALF:(k + 1) * S5_HALF]
        xs_ref[:, k * S5_HALF_N:(k + 1) * S5_HALF_N] = _dot(uk, bre_ref[k])
        xs_ref[:, S5_N + k * S5_HALF_N:S5_N + (k + 1) * S5_HALF_N] = _dot(uk, bim_ref[k])


def _s5_project_out(xs_ref, u, cre_ref, cimn_ref, d_ref, wglu_ref):
    ys = []
    for k in range(2):
        xr = xs_ref[:, k * S5_HALF_N:(k + 1) * S5_HALF_N].astype(BF16)
        xi = xs_ref[:, S5_N + k * S5_HALF_N:S5_N + (k + 1) * S5_HALF_N].astype(BF16)
        ys.append(_dot(xr, cre_ref[k]) + _dot(xi, cimn_ref[k]))
    y = jnp.concatenate(ys, axis=1) + d_ref[...] * u
    y = jax.nn.gelu(y)
    return y * jax.nn.sigmoid(_dot(y.astype(BF16), wglu_ref[...]))


def _s5_scan_step(ab_ref, xs_ref, rows, xre, xim):
    are = ab_ref[:, :S5_N]
    aim = ab_ref[:, S5_N:]
    nre = are * xre - aim * xim + xs_ref[rows, :S5_N]
    nim = are * xim + aim * xre + xs_ref[rows, S5_N:]
    xs_ref[rows, :S5_N] = nre
    xs_ref[rows, S5_N:] = nim
    return nre, nim


def _s5_step_kernel(u_ref, x0_ref, a_ref, bre_ref, bim_ref, cre_ref, cimn_ref, d_ref, wglu_ref,
                    o_ref, xl_ref, xs_scr):
    u = u_ref[...]
    _s5_project_in(u, xs_scr, bre_ref, bim_ref)
    xre, xim = _s5_scan_step(a_ref, xs_scr, slice(None), x0_ref[:, :S5_N], x0_ref[:, S5_N:])
    xl_ref[:, :S5_N] = xre
    xl_ref[:, S5_N:] = xim
    o_ref[...] = _s5_project_out(xs_scr, u, cre_ref, cimn_ref, d_ref, wglu_ref)


def _s5_prompt_kernel(u_ref, un_ref, x0_ref, a_ref, bre_ref, bim_ref, cre_ref, cimn_ref, d_ref, wglu_ref,
                      o_ref, xl_ref, xs0, xs1, ut0, ut1, st_scr, ab_scr, p_in, p_out, *, nb, tl):
    g = pl.program_id(0)
    m = nb * tl

    tg = V7X_MXU_DIM // nb
    rows_g = nb * tg

    def to_token_major(ref, first):
        pm = p_in[...]
        groups = []
        for lo_t in range(first, first + tl, tg):
            us = ref[:, lo_t:lo_t + tg, :].reshape(rows_g, MIX_WIDTH)
            hi = us.astype(BF16)
            lo = (us - hi.astype(F32)).astype(BF16)
            groups.append(_dot(pm, hi) + _dot(pm, lo))
        return jnp.concatenate(groups, axis=0)

    def store_sequence_major(out, first):
        pm = p_out[...]
        for gq in range(tl // tg):
            piece = _dot(pm, out[gq * rows_g:(gq + 1) * rows_g].astype(BF16)).astype(o_ref.dtype)
            o_ref[:, first + gq * tg:first + (gq + 1) * tg, :] = piece.reshape(nb, tg, MIX_WIDTH)

    def scan(xs, xre, xim):
        for t in range(tl):
            xre, xim = _s5_scan_step(ab_scr, xs, slice(t * nb, (t + 1) * nb), xre, xim)
        return xre, xim

    @pl.when(g == 0)
    def _():
        st_scr[...] = x0_ref[...]
        ab_scr[...] = jnp.broadcast_to(a_ref[...], ab_scr.shape)
        shift = nb.bit_length() - 1
        ro = lax.broadcasted_iota(jnp.int32, (rows_g, rows_g), 0)
        ci = lax.broadcasted_iota(jnp.int32, (rows_g, rows_g), 1)
        seq_major = lambda r: (r & (nb - 1)) * tg + (r >> shift)
        p_in[...] = jnp.where(ci == seq_major(ro), 1.0, 0.0).astype(BF16)
        p_out[...] = jnp.where(ro == seq_major(ci), 1.0, 0.0).astype(BF16)
        u0 = to_token_major(u_ref, 0)
        ut0[...] = u0
        _s5_project_in(u0, xs0, bre_ref, bim_ref)

    u1 = to_token_major(u_ref, tl)
    ut1[...] = u1
    _s5_project_in(u1, xs1, bre_ref, bim_ref)
    xre, xim = scan(xs0, st_scr[:, :S5_N], st_scr[:, S5_N:])
    store_sequence_major(_s5_project_out(xs0, ut0[...], cre_ref, cimn_ref, d_ref, wglu_ref), 0)
    xre, xim = scan(xs1, xre, xim)
    st_scr[:, :S5_N] = xre
    st_scr[:, S5_N:] = xim
    un = to_token_major(un_ref, 0)
    ut0[...] = un
    _s5_project_in(un, xs0, bre_ref, bim_ref)
    store_sequence_major(_s5_project_out(xs1, u1, cre_ref, cimn_ref, d_ref, wglu_ref), tl)
    xl_ref[...] = st_scr[...]


def _s5_weight_specs(layer):
    lay3 = lambda i: (layer, 0, 0)
    lay4 = lambda i: (layer, 0, 0, 0)
    return [_resident((None, 1, 2 * S5_N), lay3),
            _resident((None, 2, S5_HALF, S5_HALF_N), lay4),
            _resident((None, 2, S5_HALF, S5_HALF_N), lay4),
            _resident((None, 2, S5_HALF_N, S5_HALF), lay4),
            _resident((None, 2, S5_HALF_N, S5_HALF), lay4),
            _resident((None, 1, MIX_WIDTH), lay3),
            _resident((None, MIX_WIDTH, MIX_WIDTH), lay3)]


S5_WEIGHT_BYTES = 4 * S5_HALF * S5_HALF_N * 2 * 2 + MIX_WIDTH * MIX_WIDTH * 2 + (2 * S5_N + MIX_WIDTH) * 4


def _s5_step(u, x0, s5w, layer):
    nb = u.shape[0]
    blocks = 4 * nb * MIX_WIDTH * 4 + 5 * nb * 2 * S5_N * 4 + S5_WEIGHT_BYTES + 6 * nb * MIX_WIDTH * 4
    whole = lambda w: pl.BlockSpec((nb, w), lambda i: (0, 0))
    return pl.pallas_call(
        _s5_step_kernel,
        out_shape=(jax.ShapeDtypeStruct(u.shape, F32), jax.ShapeDtypeStruct((nb, 2 * S5_N), F32)),
        grid=(1,),
        in_specs=[whole(MIX_WIDTH), whole(2 * S5_N)] + _s5_weight_specs(layer),
        out_specs=(whole(MIX_WIDTH), whole(2 * S5_N)),
        scratch_shapes=[pltpu.VMEM((nb, 2 * S5_N), F32)],
        compiler_params=_params(blocks, 1),
        name="s5_step",
    )(u, x0, *s5w)


def _s5_prompt(u, x0, s5w, layer):
    nb, seq, _ = u.shape
    tl = S5_TOKENS
    m = nb * tl
    steps = seq // (2 * tl)
    perm_rows = V7X_MXU_DIM
    assert nb & (nb - 1) == 0 and seq % (2 * tl) == 0 and m % perm_rows == 0
    blocks = (2 * 2 * m * MIX_WIDTH * 4 + 2 * m * MIX_WIDTH * 4 + 2 * 2 * m * MIX_WIDTH * 2 + 2 * m * 2 * S5_N * 4
              + 2 * m * MIX_WIDTH * 4 + 6 * nb * 2 * S5_N * 4 + S5_WEIGHT_BYTES + 2 * perm_rows * perm_rows * 2
              + m * 2 * S5_N * 2 + 6 * m * MIX_WIDTH * 4)
    pair = pl.BlockSpec((nb, 2 * tl, MIX_WIDTH), lambda i: (0, i, 0))
    nxt = pl.BlockSpec((nb, tl, MIX_WIDTH), lambda i: (0, jnp.minimum(2 * i + 2, 2 * steps - 1), 0))
    state = pl.BlockSpec((nb, 2 * S5_N), lambda i: (0, 0))
    return pl.pallas_call(
        functools.partial(_s5_prompt_kernel, nb=nb, tl=tl),
        out_shape=(jax.ShapeDtypeStruct(u.shape, BF16), jax.ShapeDtypeStruct((nb, 2 * S5_N), F32)),
        grid=(steps,),
        in_specs=[pair, nxt, _resident((nb, 2 * S5_N), lambda i: (0, 0))] + _s5_weight_specs(layer),
        out_specs=(pair, state),
        scratch_shapes=[pltpu.VMEM((m, 2 * S5_N), F32), pltpu.VMEM((m, 2 * S5_N), F32),
                        pltpu.VMEM((m, MIX_WIDTH), F32), pltpu.VMEM((m, MIX_WIDTH), F32),
                        pltpu.VMEM((nb, 2 * S5_N), F32), pltpu.VMEM((nb, 2 * S5_N), F32),
                        pltpu.VMEM((perm_rows, perm_rows), BF16), pltpu.VMEM((perm_rows, perm_rows), BF16)],
        compiler_params=_params(blocks, 1),
        name="s5",
    )(u, u, x0, *s5w)


def _ret_kernel(p_ref, cos_ref, sin_ref, o_ref, s_out_ref, s_scr, *, seq):
    c = CHUNK
    heads = range(HEADS)
    row = lax.broadcasted_iota(jnp.int32, (c, c), 0)
    col = lax.broadcasted_iota(jnp.int32, (c, c), 1)
    diff = (row - col).astype(F32)
    ivec = lax.broadcasted_iota(jnp.int32, (c, HEAD_DIM), 0).astype(F32)
    dmats = [jnp.where(diff >= 0, jnp.exp(lg * jnp.maximum(diff, 0.0)), 0.0) * QK_SCALE for lg in LOG_GAMMA]
    q_decs = [jnp.exp(lg * (ivec + 1.0)) for lg in LOG_GAMMA]
    k_decs = [jnp.exp(lg * (c - 1.0 - ivec)) * QK_SCALE for lg in LOG_GAMMA]
    s_scr[...] = jnp.zeros_like(s_scr)

    def chunk(j, carry):
        r = pl.ds(pl.multiple_of(j * c, c), c)
        cs = cos_ref[r, :]
        sn = sin_ref[r, :]
        sl = lambda part, h: p_ref[r, part * MIX_WIDTH + h * HEAD_DIM:part * MIX_WIDTH + (h + 1) * HEAD_DIM]
        q = [_rope(sl(0, h).astype(F32), cs, sn) for h in heads]
        k = [_rope(sl(1, h).astype(F32), cs, sn) for h in heads]
        v = [sl(2, h).astype(BF16) for h in heads]
        s = [s_scr[h] for h in heads]
        qb = [q[h].astype(BF16) for h in heads]
        kb = [k[h].astype(BF16) for h in heads]
        att = [_dot_nt(qb[h], kb[h]) for h in heads]
        inter = [_dot((q[h] * q_decs[h]).astype(BF16), s[h].astype(BF16)) for h in heads]
        upd = [_dot_tn((k[h] * k_decs[h]).astype(BF16), v[h]) for h in heads]
        attb = [(att[h] * dmats[h]).astype(BF16) for h in heads]
        o = [_dot(attb[h], v[h]) + inter[h] for h in heads]
        for h in heads:
            s_scr[h] = s[h] * math.exp(LOG_GAMMA[h] * c) + upd[h]
        on = [_head_norm(o[h]) for h in heads]
        for h in heads:
            g = sl(3, h).astype(F32)
            o_ref[r, h * HEAD_DIM:(h + 1) * HEAD_DIM] = (on[h] * (g * jax.nn.sigmoid(g))).astype(o_ref.dtype)
        return carry

    lax.fori_loop(0, seq // c, chunk, 0, unroll=2)
    s_out_ref[...] = s_scr[...]


def _retention_prompt(p, cos, sin, b, seq):
    half = 4 * MIX_WIDTH
    blocks = 2 * seq * half * 2 + 2 * seq * V7X_LANES * 4 + 2 * seq * MIX_WIDTH * 2 + 3 * HEADS * HEAD_DIM * HEAD_DIM * 4
    return pl.pallas_call(
        functools.partial(_ret_kernel, seq=seq),
        out_shape=(jax.ShapeDtypeStruct((b * seq, MIX_WIDTH), BF16),
                   jax.ShapeDtypeStruct((b, HEADS, HEAD_DIM, HEAD_DIM), F32)),
        grid=(b,),
        in_specs=[pl.BlockSpec((seq, half), lambda i: (i, 0)),
                  _resident((seq, HEAD_DIM), lambda i: (0, 0)),
                  _resident((seq, HEAD_DIM), lambda i: (0, 0))],
        out_specs=(pl.BlockSpec((seq, MIX_WIDTH), lambda i: (i, 0)),
                   pl.BlockSpec((None, HEADS, HEAD_DIM, HEAD_DIM), lambda i: (i, 0, 0, 0))),
        scratch_shapes=[pltpu.VMEM((HEADS, HEAD_DIM, HEAD_DIM), F32)],
        compiler_params=_params(blocks, 1),
        name="retention",
    )(p, cos, sin)


N_REP = 3 * HEADS


def _split_dot(x, w):
    hi = x.astype(BF16)
    lo = (x - hi.astype(F32)).astype(BF16)
    return _dot(hi, w) + _dot(lo, w)


def _split_dot_tn(x, w):
    hi = x.astype(BF16)
    lo = (x - hi.astype(F32)).astype(BF16)
    return _dot_tn(hi, w) + _dot_tn(lo, w)


def _mlstm_kernel(p_ref, gif_ref, bias_ref, o_ref, c_out_ref, n_out_ref, m_out_ref,
                  cn_scr, m_scr, rows_scr, rep_scr, *, seq):
    c = CHUNK
    nch = seq // c
    nrow = 8 * nch
    heads = range(HEADS)
    neg_inf = float("-inf")
    row = lax.broadcasted_iota(jnp.int32, (c, c), 0)
    col = lax.broadcasted_iota(jnp.int32, (c, c), 1)
    causal = row >= col
    triu = jnp.where(row <= col, 1.0, 0.0).astype(BF16)
    lane = lax.broadcasted_iota(jnp.int32, (c, V7X_LANES), 1)
    ones = jnp.ones((c, V7X_LANES), BF16)
    er = lax.broadcasted_iota(jnp.int32, (16, N_REP * V7X_LANES), 0)
    ec = lax.broadcasted_iota(jnp.int32, (16, N_REP * V7X_LANES), 1)
    expand = jnp.where(ec // V7X_LANES == er, 1.0, 0.0).astype(BF16)
    cn_scr[...] = jnp.zeros_like(cn_scr)
    m_scr[...] = jnp.zeros_like(m_scr)

    pieces = []
    for j in range(nch):
        fb = gif_ref[j * c:(j + 1) * c, :] + bias_ref[...]
        gl = jnp.where(lane < HEADS, fb, jax.nn.log_sigmoid(fb))
        pieces.append(gl.T[0:8, :])
    x = jnp.concatenate(pieces, axis=0)
    cum = _split_dot(x, triu)
    sub = lax.broadcasted_iota(jnp.int32, (nrow, c), 0) % 8
    tok = lax.broadcasted_iota(jnp.int32, (nrow, c), 1)
    a = x - pltpu.roll(cum, nrow - HEADS, 0)
    d = 1
    while d < c:
        a = jnp.maximum(a, jnp.where(tok >= d, pltpu.roll(a, d, 1), neg_inf))
        d *= 2
    rows_scr[0] = jnp.where(sub < HEADS, x, cum)
    rows_scr[1] = a

    def replicate(j):
        r8 = pl.ds(pl.multiple_of(j * 8, 8), 8)
        r16 = jnp.concatenate([rows_scr[0, r8, :], rows_scr[1, r8, :]], axis=0)
        return _split_dot_tn(r16, expand)

    rep_scr[0] = replicate(0)

    def chunk(j, carry):
        r = pl.ds(pl.multiple_of(j * c, c), c)
        r8 = pl.ds(pl.multiple_of(j * 8, 8), 8)
        slot = j % 2
        rep_next = replicate(jnp.minimum(j + 1, nch - 1))
        sl = lambda part, h: p_ref[r, part * MIX_WIDTH + h * HEAD_DIM:part * MIX_WIDTH + (h + 1) * HEAD_DIM]
        blk = lambda n, h: rep_scr[slot, :, (n * HEADS + h) * V7X_LANES:(n * HEADS + h + 1) * V7X_LANES]
        y8 = rows_scr[0, r8, :]
        q = [sl(0, h).astype(BF16) for h in heads]
        k = [sl(1, h).astype(BF16) for h in heads]
        v1 = [jnp.concatenate([sl(2, h).astype(BF16), ones], axis=1) for h in heads]
        cn = [cn_scr[h] for h in heads]
        m_prev = [m_scr[h] for h in heads]
        ic = [blk(0, h) for h in heads]
        bc = [blk(1, h) for h in heads]
        cmc = [blk(2, h) for h in heads]
        qk = [_dot_nt(q[h], k[h]) for h in heads]
        qcn = [_dot(q[h], cn[h].astype(BF16)) for h in heads]
        b_last = [bc[h][c - 1:c, :] for h in heads]
        wk_log = [b_last[h] - bc[h] + ic[h] for h in heads]
        m_new = [jnp.maximum(b_last[h] + m_prev[h], jnp.max(wk_log[h], axis=0, keepdims=True)) for h in heads]
        wk = [jnp.exp(wk_log[h] - m_new[h]) * QK_SCALE for h in heads]
        decay = [jnp.exp(b_last[h] + m_prev[h] - m_new[h]) for h in heads]
        kw = [(k[h].astype(F32) * wk[h]).astype(BF16) for h in heads]
        upd = [_dot_tn(kw[h], v1[h]) for h in heads]
        m_t = [bc[h] + jnp.maximum(m_prev[h], cmc[h]) for h in heads]
        dlog = [jnp.where(causal, bc[h] - y8[HEADS + h:HEADS + h + 1, :] + y8[h:h + 1, :], neg_inf) for h in heads]
        w_intra = [jnp.exp(dlog[h] - m_t[h]) * QK_SCALE for h in heads]
        w_inter = [jnp.exp(bc[h] + m_prev[h] - m_t[h]) for h in heads]
        s = [(qk[h] * w_intra[h]).astype(BF16) for h in heads]
        sv = [_dot(s[h], v1[h]) for h in heads]
        hh = []
        for h in heads:
            num = sv[h][:, :HEAD_DIM] + w_inter[h] * qcn[h][:, :HEAD_DIM]
            den = sv[h][:, HEAD_DIM:] + w_inter[h] * qcn[h][:, HEAD_DIM:]
            hh.append(num / jnp.maximum(jnp.abs(den), jnp.exp(-m_t[h])))
        for h in heads:
            cn_scr[h] = jnp.concatenate([decay[h], decay[h]], axis=1) * cn[h] + upd[h]
            m_scr[h] = m_new[h]
        on = [_head_norm(hh[h]) for h in heads]
        for h in heads:
            o_ref[r, h * HEAD_DIM:(h + 1) * HEAD_DIM] = (on[h] * jax.nn.sigmoid(sl(3, h).astype(F32))).astype(o_ref.dtype)
        rep_scr[1 - slot] = rep_next
        return carry

    lax.fori_loop(0, nch, chunk, 0, unroll=2)
    for h in heads:
        cnh = cn_scr[h]
        c_out_ref[h] = cnh[:, :HEAD_DIM]
        n_out_ref[h:h + 1, :] = cnh[:, HEAD_DIM:].T[0:1, :]
        m_out_ref[h:h + 1, :] = m_scr[h]


def _mlstm_prompt(p, gif, bias, b, seq):
    half = 4 * MIX_WIDTH
    nch = seq // CHUNK
    blocks = (2 * seq * half * 2 + 2 * seq * N_GIF * 4 + 2 * seq * MIX_WIDTH * 2
              + 5 * HEADS * HEAD_DIM * 2 * HEAD_DIM * 4 + 4 * CHUNK * N_REP * V7X_LANES * 4)
    return pl.pallas_call(
        functools.partial(_mlstm_kernel, seq=seq),
        out_shape=(jax.ShapeDtypeStruct((b * seq, MIX_WIDTH), BF16),
                   jax.ShapeDtypeStruct((b, HEADS, HEAD_DIM, HEAD_DIM), F32),
                   jax.ShapeDtypeStruct((b, HEADS, HEAD_DIM), F32),
                   jax.ShapeDtypeStruct((b, HEADS, V7X_LANES), F32)),
        grid=(b,),
        in_specs=[pl.BlockSpec((seq, half), lambda i: (i, 1)),
                  pl.BlockSpec((seq, N_GIF), lambda i: (i, 0)),
                  _resident((1, N_GIF), lambda i: (0, 0))],
        out_specs=(pl.BlockSpec((seq, MIX_WIDTH), lambda i: (i, 0)),
                   pl.BlockSpec((None, HEADS, HEAD_DIM, HEAD_DIM), lambda i: (i, 0, 0, 0)),
                   pl.BlockSpec((None, HEADS, HEAD_DIM), lambda i: (i, 0, 0)),
                   pl.BlockSpec((None, HEADS, V7X_LANES), lambda i: (i, 0, 0))),
        scratch_shapes=[pltpu.VMEM((HEADS, HEAD_DIM, 2 * HEAD_DIM), F32),
                        pltpu.VMEM((HEADS, 1, V7X_LANES), F32),
                        pltpu.VMEM((2, 8 * nch, CHUNK), F32),
                        pltpu.VMEM((2, CHUNK, N_REP * V7X_LANES), F32)],
        compiler_params=_params(blocks, 1),
        name="mlstm",
    )(p, gif, bias)


SEQ_BLOCK = 8


def _column_expander():
    er = lax.broadcasted_iota(jnp.int32, (SEQ_BLOCK, SEQ_BLOCK * V7X_LANES), 0)
    ec = lax.broadcasted_iota(jnp.int32, (SEQ_BLOCK, SEQ_BLOCK * V7X_LANES), 1)
    return jnp.where(ec // V7X_LANES == er, 1.0, 0.0).astype(BF16)


def _columns(x, expander):
    return _dot_tn(x.astype(BF16), expander)


def _ret_step_kernel(p_ref, cos_ref, sin_ref, s_ref, *rest, layer, first):
    o_ref, s_out_ref, row_scr = rest[-3:]
    s_out_ref = _own_layer(s_out_ref, layer, first)
    cs = cos_ref[...]
    sn = sin_ref[...]
    expander = _column_expander()
    for h in range(HEADS):
        lo = h * HEAD_DIM
        gamma = math.exp(LOG_GAMMA[h])
        q = _rope(p_ref[:, lo:lo + HEAD_DIM], cs, sn)
        k = _rope(p_ref[:, MIX_WIDTH + lo:MIX_WIDTH + lo + HEAD_DIM], cs, sn) * QK_SCALE
        v = p_ref[:, 2 * MIX_WIDTH + lo:2 * MIX_WIDTH + lo + HEAD_DIM]
        g = p_ref[:, 3 * MIX_WIDTH + lo:3 * MIX_WIDTH + lo + HEAD_DIM]
        qt = _columns(q * gamma, expander)
        kt = _columns(k, expander)
        for b in range(SEQ_BLOCK):
            s = s_ref[b, h]
            blk = slice(b * V7X_LANES, (b + 1) * V7X_LANES)
            row_scr[b:b + 1, :] = jnp.sum(qt[:, blk] * s, axis=0, keepdims=True)
            s_out_ref[b, h] = gamma * s + kt[:, blk] * v[b:b + 1, :]
        o = jnp.sum(q * k, axis=-1, keepdims=True) * v + row_scr[...]
        o_ref[:, lo:lo + HEAD_DIM] = _head_norm(o) * (g * jax.nn.sigmoid(g))


def _stacked_state(acc, layer):
    tail = (HEADS, HEAD_DIM, HEAD_DIM)
    if acc is None:
        return (), [], pl.BlockSpec((DEPTH, SEQ_BLOCK) + tail, lambda i: (0, i, 0, 0, 0))
    return (acc,), [pl.BlockSpec(memory_space=pl.ANY)], pl.BlockSpec((None, SEQ_BLOCK) + tail, lambda i: (layer, i, 0, 0, 0))


def _own_layer(state_out_ref, layer, first):
    if not first:
        return state_out_ref
    for other in range(DEPTH):
        if other != layer:
            state_out_ref[other] = jnp.zeros(state_out_ref.shape[1:], F32)
    return state_out_ref.at[layer]


def _retention_step(p, cos, sin, state, layer, acc):
    nseq = p.shape[0]
    half = 4 * MIX_WIDTH
    sblk = SEQ_BLOCK * HEADS * HEAD_DIM * HEAD_DIM * 4
    blocks = (2 + 2 * (DEPTH if acc is None else 1)) * sblk + 2 * SEQ_BLOCK * half * 4 + 2 * SEQ_BLOCK * MIX_WIDTH * 4
    state_blk = pl.BlockSpec((None, SEQ_BLOCK, HEADS, HEAD_DIM, HEAD_DIM), lambda i: (layer, i, 0, 0, 0))
    extra, extra_specs, out_state_blk = _stacked_state(acc, layer)
    return pl.pallas_call(
        functools.partial(_ret_step_kernel, layer=layer, first=acc is None),
        out_shape=(jax.ShapeDtypeStruct((nseq, MIX_WIDTH), F32),
                   jax.ShapeDtypeStruct(state.shape, F32)),
        grid=(nseq // SEQ_BLOCK,),
        in_specs=[pl.BlockSpec((SEQ_BLOCK, half), lambda i: (i, 0)),
                  _resident((1, HEAD_DIM), lambda i: (0, 0)),
                  _resident((1, HEAD_DIM), lambda i: (0, 0)),
                  state_blk] + extra_specs,
        out_specs=(pl.BlockSpec((SEQ_BLOCK, MIX_WIDTH), lambda i: (i, 0)), out_state_blk),
        input_output_aliases={4: 1} if extra else {},
        scratch_shapes=[pltpu.VMEM((SEQ_BLOCK, HEAD_DIM), F32)],
        compiler_params=_params(blocks, 1),
        name="retention_step",
    )(p, cos, sin, state, *extra)


def _mlstm_step_kernel(p_ref, gif_ref, bias_ref, c_ref, n_ref, m_ref, *rest, layer, first):
    o_ref, c_out_ref, n_out_ref, m_out_ref, row_scr = rest[-5:]
    c_out_ref = _own_layer(c_out_ref, layer, first)
    fb = gif_ref[...] + bias_ref[...]
    lf_all = jax.nn.log_sigmoid(fb)
    lane = lax.broadcasted_iota(jnp.int32, (SEQ_BLOCK, V7X_LANES), 1)
    m_all = m_ref[...]
    m_acc = jnp.zeros((SEQ_BLOCK, V7X_LANES), F32)
    expander = _column_expander()
    for h in range(HEADS):
        lo = h * HEAD_DIM
        q = p_ref[:, lo:lo + HEAD_DIM]
        k = p_ref[:, MIX_WIDTH + lo:MIX_WIDTH + lo + HEAD_DIM] * QK_SCALE
        v = p_ref[:, 2 * MIX_WIDTH + lo:2 * MIX_WIDTH + lo + HEAD_DIM]
        og = p_ref[:, 3 * MIX_WIDTH + lo:3 * MIX_WIDTH + lo + HEAD_DIM]
        ig = fb[:, h:h + 1]
        inter = lf_all[:, HEADS + h:HEADS + h + 1] + m_all[:, h:h + 1]
        m_t = jnp.maximum(inter, ig)
        w_intra = jnp.exp(ig - m_t)
        w_inter = jnp.exp(inter - m_t)
        nv = n_ref[:, lo:lo + HEAD_DIM]
        kw = k * w_intra
        qt = _columns(q, expander)
        kt = _columns(kw, expander)
        for b in range(SEQ_BLOCK):
            cm = c_ref[b, h]
            blk = slice(b * V7X_LANES, (b + 1) * V7X_LANES)
            row_scr[b:b + 1, :] = jnp.sum(qt[:, blk] * cm, axis=0, keepdims=True)
            c_out_ref[b, h] = w_inter[b:b + 1, :] * cm + kt[:, blk] * v[b:b + 1, :]
        s = jnp.sum(q * k, axis=-1, keepdims=True) * w_intra
        num = s * v + w_inter * row_scr[...]
        den = s + w_inter * jnp.sum(q * nv, axis=-1, keepdims=True)
        hh = num / jnp.maximum(jnp.abs(den), jnp.exp(-m_t))
        n_out_ref[:, lo:lo + HEAD_DIM] = w_inter * nv + kw
        m_acc = jnp.where(lane == h, m_t, m_acc)
        o_ref[:, lo:lo + HEAD_DIM] = _head_norm(hh) * jax.nn.sigmoid(og)
    m_out_ref[...] = m_acc


def _mlstm_step(p, gif, bias, c_state, n_state, m_state, layer, acc):
    nseq = p.shape[0]
    half = 4 * MIX_WIDTH
    sblk = SEQ_BLOCK * HEADS * HEAD_DIM * HEAD_DIM * 4
    blocks = (2 + 2 * (DEPTH if acc is None else 1)) * sblk + 2 * SEQ_BLOCK * half * 4 + 8 * SEQ_BLOCK * MIX_WIDTH * 4
    rows = lambda w: pl.BlockSpec((SEQ_BLOCK, w), lambda i: (i, 0))
    state_blk = pl.BlockSpec((None, SEQ_BLOCK, HEADS, HEAD_DIM, HEAD_DIM), lambda i: (layer, i, 0, 0, 0))
    extra, extra_specs, out_state_blk = _stacked_state(acc, layer)
    return pl.pallas_call(
        functools.partial(_mlstm_step_kernel, layer=layer, first=acc is None),
        out_shape=(jax.ShapeDtypeStruct((nseq, MIX_WIDTH), F32),
                   jax.ShapeDtypeStruct(c_state.shape, F32),
                   jax.ShapeDtypeStruct((nseq, MIX_WIDTH), F32),
                   jax.ShapeDtypeStruct((nseq, V7X_LANES), F32)),
        grid=(nseq // SEQ_BLOCK,),
        in_specs=[pl.BlockSpec((SEQ_BLOCK, half), lambda i: (i, 1)),
                  rows(N_GIF),
                  _resident((1, N_GIF), lambda i: (0, 0)),
                  state_blk,
                  rows(MIX_WIDTH),
                  rows(HEADS)] + extra_specs,
        out_specs=(rows(MIX_WIDTH), out_state_blk, rows(MIX_WIDTH), rows(V7X_LANES)),
        input_output_aliases={6: 1} if extra else {},
        scratch_shapes=[pltpu.VMEM((SEQ_BLOCK, HEAD_DIM), F32)],
        compiler_params=_params(blocks, 1),
        name="mlstm_step",
    )(p, gif, bias, c_state, n_state, m_state, *extra)


def _rope_tables(pos):
    inv = ROPE_THETA ** (-jnp.arange(0, HEAD_DIM, 2, dtype=F32) / HEAD_DIM)
    ang = pos.astype(F32)[:, None] * inv[None, :]
    cos, sin = jnp.cos(ang), jnp.sin(ang)
    return jnp.concatenate([cos, cos], axis=-1), jnp.concatenate([-sin, sin], axis=-1)


def _prepare_weights(w_ffn1_gu, w_ffn1_down, w_in, w_s5_glu, w_branch, w_out, w_ffn2_gu, w_ffn2_down):
    cast = lambda w: w.astype(BF16)
    gif = jnp.pad(w_in[..., N_MAIN:N_MAIN + 2 * HEADS], ((0, 0), (0, 0), (0, N_GIF - 2 * HEADS)))
    off_su = N_MAIN + 2 * HEADS
    w_in_p = jnp.concatenate([w_in[..., :N_MAIN], w_in[..., off_su + MIX_WIDTH:],
                              w_in[..., off_su:off_su + MIX_WIDTH], gif], axis=-1)
    return dict(gu1=cast(w_ffn1_gu), d1=cast(w_ffn1_down), w_in=cast(w_in_p), glu=cast(w_s5_glu),
                branch=cast(w_branch), out=cast(w_out), gu2=cast(w_ffn2_gu), d2=cast(w_ffn2_down))


def _run_trunk(x, nseq, seq, states, wts, norms, s5w, bias, ln_final, prompt):
    t = nseq * seq
    tm = 512 if t % 512 == 0 else t
    act_dtype = BF16 if prompt else F32
    if prompt:
        cos, sin = _rope_tables(jnp.arange(seq, dtype=jnp.int32))
    else:
        cos, sin = _rope_tables(PAST_LEN + jnp.arange(seq, dtype=jnp.int32))
    outs = ([], [], [], [], [], [])
    ret_acc = mc_acc = None
    for l in range(DEPTH):
        x = _ffn(x, norms["ffn1"], wts["gu1"], wts["d1"], ln_final, l, tm, final=False)
        main, gates, su, gif = _inproj(x, norms["mix"], wts["w_in"], l, tm, act_dtype)
        if prompt:
            r_out, new_ret = _retention_prompt(main, cos, sin, nseq, seq)
            m_out, new_mc, new_mn, new_mm = _mlstm_prompt(main, gif, bias[l], nseq, seq)
            new_mm = new_mm[:, :, 0]
            x0 = jnp.zeros((nseq, 2 * S5_N), F32)
            s_out, x_last = _s5_prompt(su.reshape(nseq, seq, MIX_WIDTH), x0, s5w, l)
        else:
            st_ret, st_mc, st_mn, st_mm, st_sre, st_sim = states
            r_out, ret_acc = _retention_step(main, cos, sin, st_ret, l, ret_acc)
            m_out, mc_acc, new_mn, new_mm = _mlstm_step(
                main, gif, bias[l], st_mc, st_mn[l].reshape(nseq, MIX_WIDTH), st_mm[l], l, mc_acc)
            new_ret = new_mc = None
            new_mn = new_mn.reshape(nseq, HEADS, HEAD_DIM)
            new_mm = new_mm[:, :HEADS]
            x0 = jnp.concatenate([st_sre[l].reshape(nseq, S5_N), st_sim[l].reshape(nseq, S5_N)], axis=-1)
            s_out, x_last = _s5_step(su, x0, s5w, l)
        x = _merge(x, r_out, m_out, s_out, gates, wts["branch"], wts["out"], l, tm)
        x = _ffn(x, norms["ffn2"], wts["gu2"], wts["d2"], ln_final, l, tm, final=(l == DEPTH - 1))
        new = (new_ret, new_mc, new_mn, new_mm,
               x_last[:, :S5_N].reshape(nseq, S5_GROUPS, S5_STATE), x_last[:, S5_N:].reshape(nseq, S5_GROUPS, S5_STATE))
        for acc, s in zip(outs, new):
            acc.append(s)
    stacked = [None if acc[0] is None else jnp.stack(acc, axis=0) for acc in outs]
    if not prompt:
        stacked[0], stacked[1] = ret_acc, mc_acc
    return x, tuple(stacked)


def kernel(x_prompt, x_sample, state_ret, state_mlstm_c, state_mlstm_n, state_mlstm_m, state_s5_re, state_s5_im, ln_ffn1, w_ffn1_gu, w_ffn1_down, ln_mix, w_in, b_gates, s5_a_re, s5_a_im, s5_b_re, s5_b_im, s5_c_re, s5_c_im, s5_d, s5_log_dt, w_s5_glu, w_branch, w_out, ln_ffn2, w_ffn2_gu, w_ffn2_down, ln_final):
    wts = _prepare_weights(w_ffn1_gu, w_ffn1_down, w_in, w_s5_glu, w_branch, w_out, w_ffn2_gu, w_ffn2_down)
    norms = dict(ffn1=ln_ffn1.reshape(DEPTH, 1, D_MODEL), mix=ln_mix.reshape(DEPTH, 1, D_MODEL),
                 ffn2=ln_ffn2.reshape(DEPTH, 1, D_MODEL))
    lnf = ln_final.reshape(1, D_MODEL)
    a_row, bbr, bbi = _s5_discretise(s5_a_re, s5_a_im, s5_log_dt, s5_b_re, s5_b_im)
    s5w = (a_row, _block_diag_in(bbr), _block_diag_in(bbi), _block_diag_out(s5_c_re), _block_diag_out(-s5_c_im),
           s5_d.reshape(DEPTH, 1, MIX_WIDTH), wts["glu"])
    bias = jnp.pad(b_gates, ((0, 0), (0, N_GIF - 2 * HEADS))).reshape(DEPTH, 1, N_GIF)

    pb, pl_len, _ = x_prompt.shape
    sb, sl, _ = x_sample.shape
    y_p, st_p = _run_trunk(x_prompt.reshape(pb * pl_len, D_MODEL), pb, pl_len, None, wts, norms, s5w, bias, lnf, True)
    sample_states = (state_ret, state_mlstm_c, state_mlstm_n, state_mlstm_m, state_s5_re, state_s5_im)
    y_s, st_s = _run_trunk(x_sample.reshape(sb * sl, D_MODEL), sb, sl, sample_states, wts, norms, s5w, bias, lnf, False)
    return (y_p.reshape(pb, pl_len, D_MODEL), y_s.reshape(sb, sl, D_MODEL)) + st_p + st_s
```

```python
import functools
import math

import jax
import jax.numpy as jnp
from jax import lax
from jax.experimental import pallas as pl
from jax.experimental.pallas import tpu as pltpu

F32 = jnp.float32
BF16 = jnp.bfloat16

D_MODEL = 1024
DEPTH = 4
MIX_WIDTH = D_MODEL // 2
N_BRANCH = 3
HEADS = 4
HEAD_DIM = MIX_WIDTH // HEADS
S5_GROUP = 16
S5_GROUPS = MIX_WIDTH // S5_GROUP
S5_STATE = 64
S5_N = S5_GROUPS * S5_STATE
D_FF = 2816
CHUNK = 128
PAST_LEN = 16384
ROPE_THETA = 10000.0
EPS = 1e-6

N_MAIN = 8 * MIX_WIDTH
N_GATES = N_BRANCH * D_MODEL
N_GIF = 128
N_IN_PADDED = N_MAIN + N_GATES + MIX_WIDTH + N_GIF
OFF_GATES = N_MAIN
OFF_SU = N_MAIN + N_GATES
OFF_GIF = OFF_SU + MIX_WIDTH

V7X_LANES = 128
V7X_MXU_DIM = 256
V7X_VMEM_BYTES = 64 * 1024 * 1024

FF_CHUNK = V7X_MXU_DIM
PROJ_CHUNK = 2 * V7X_MXU_DIM
S5_HALF = MIX_WIDTH // 2
S5_HALF_N = S5_N // 2
S5_TOKENS = 64
LOG_GAMMA = tuple(math.log1p(-(2.0 ** (-5.0 - h))) for h in range(HEADS))
QK_SCALE = HEAD_DIM ** -0.5


def _vmem_limit(block_bytes):
    want = int(block_bytes * 1.25) + (8 << 20)
    return min(want, V7X_VMEM_BYTES - (6 << 20))


def _params(block_bytes, n_grid):
    return pltpu.CompilerParams(
        dimension_semantics=("arbitrary",) * n_grid,
        vmem_limit_bytes=_vmem_limit(block_bytes))


def _resident(shape, index_map):
    return pl.BlockSpec(shape, index_map, pipeline_mode=pl.Buffered(1))


def _dot(a, b):
    return jnp.dot(a, b, preferred_element_type=F32)


def _dot_nt(a, b):
    return lax.dot_general(a, b, (((1,), (1,)), ((), ())), preferred_element_type=F32)


def _dot_tn(a, b):
    return lax.dot_general(a, b, (((0,), (0,)), ((), ())), preferred_element_type=F32)


def _rms(x, g):
    return x * lax.rsqrt(jnp.mean(x * x, axis=-1, keepdims=True) + EPS) * g


def _head_norm(x):
    mu = jnp.mean(x, axis=-1, keepdims=True)
    xc = x - mu
    var = jnp.mean(xc * xc, axis=-1, keepdims=True)
    return xc * lax.rsqrt(var + EPS)


def _rope(x, cos, sin_signed):
    return x * cos + pltpu.roll(x, HEAD_DIM // 2, 1) * sin_signed


def _ffn_kernel(x_ref, ln_ref, wgu_ref, wd_ref, lnf_ref, o_ref, acc_ref, *, final):
    x = x_ref[...]
    xn = _rms(x, ln_ref[...]).astype(BF16)
    for c in range(D_FF // FF_CHUNK):
        lo = c * FF_CHUNK
        g = _dot(xn, wgu_ref[:, lo:lo + FF_CHUNK])
        u = _dot(xn, wgu_ref[:, D_FF + lo:D_FF + lo + FF_CHUNK])
        a = (g * jax.nn.sigmoid(g) * u).astype(BF16)
        d = _dot(a, wd_ref[lo:lo + FF_CHUNK, :])
        if c == 0:
            acc_ref[...] = d
        else:
            acc_ref[...] += d
    y = x + 0.5 * acc_ref[...]
    if final:
        y = _rms(y, lnf_ref[...])
    o_ref[...] = y


def _ffn(x, ln, wgu, wd, lnf, layer, tm, final):
    t = x.shape[0]
    blocks = (4 * tm * D_MODEL * 4 + tm * D_MODEL * 4 + D_MODEL * 2 * D_FF * 2 + D_FF * D_MODEL * 2
              + 4 * tm * FF_CHUNK * 4)
    return pl.pallas_call(
        functools.partial(_ffn_kernel, final=final),
        out_shape=jax.ShapeDtypeStruct((t, D_MODEL), F32),
        grid=(t // tm,),
        in_specs=[
            pl.BlockSpec((tm, D_MODEL), lambda i: (i, 0)),
            _resident((None, 1, D_MODEL), lambda i: (layer, 0, 0)),
            _resident((None, D_MODEL, 2 * D_FF), lambda i: (layer, 0, 0)),
            _resident((None, D_FF, D_MODEL), lambda i: (layer, 0, 0)),
            _resident((1, D_MODEL), lambda i: (0, 0)),
        ],
        out_specs=pl.BlockSpec((tm, D_MODEL), lambda i: (i, 0)),
        scratch_shapes=[pltpu.VMEM((tm, D_MODEL), F32)],
        compiler_params=_params(blocks, 1),
        name="ffn",
    )(x, ln, wgu, wd, lnf)


def _inproj_kernel(x_ref, ln_ref, w_ref, main_ref, gates_ref, su_ref, gif_ref):
    xn = _rms(x_ref[...], ln_ref[...]).astype(BF16)
    for c in range(N_MAIN // PROJ_CHUNK):
        lo = c * PROJ_CHUNK
        main_ref[:, lo:lo + PROJ_CHUNK] = _dot(xn, w_ref[:, lo:lo + PROJ_CHUNK]).astype(main_ref.dtype)
    for c in range(N_GATES // PROJ_CHUNK):
        lo = c * PROJ_CHUNK
        gates_ref[:, lo:lo + PROJ_CHUNK] = _dot(
            xn, w_ref[:, OFF_GATES + lo:OFF_GATES + lo + PROJ_CHUNK]).astype(gates_ref.dtype)
    su_ref[...] = _dot(xn, w_ref[:, OFF_SU:OFF_SU + MIX_WIDTH])
    gif_ref[...] = _dot(xn, w_ref[:, OFF_GIF:OFF_GIF + N_GIF])


def _inproj(x, ln, w, layer, tm, act_dtype):
    t = x.shape[0]
    ab = jnp.dtype(act_dtype).itemsize
    blocks = (2 * tm * D_MODEL * 4 + D_MODEL * N_IN_PADDED * 2
              + 2 * tm * (N_MAIN + N_GATES) * ab + 2 * tm * (MIX_WIDTH + N_GIF) * 4
              + 4 * tm * PROJ_CHUNK * 4)
    return pl.pallas_call(
        _inproj_kernel,
        out_shape=(jax.ShapeDtypeStruct((t, N_MAIN), act_dtype),
                   jax.ShapeDtypeStruct((t, N_GATES), act_dtype),
                   jax.ShapeDtypeStruct((t, MIX_WIDTH), F32),
                   jax.ShapeDtypeStruct((t, N_GIF), F32)),
        grid=(t // tm,),
        in_specs=[
            pl.BlockSpec((tm, D_MODEL), lambda i: (i, 0)),
            _resident((None, 1, D_MODEL), lambda i: (layer, 0, 0)),
            _resident((None, D_MODEL, N_IN_PADDED), lambda i: (layer, 0, 0)),
        ],
        out_specs=(pl.BlockSpec((tm, N_MAIN), lambda i: (i, 0)),
                   pl.BlockSpec((tm, N_GATES), lambda i: (i, 0)),
                   pl.BlockSpec((tm, MIX_WIDTH), lambda i: (i, 0)),
                   pl.BlockSpec((tm, N_GIF), lambda i: (i, 0))),
        compiler_params=_params(blocks, 1),
        name="inproj",
    )(x, ln, w)


def _merge_kernel(x_ref, r_ref, m_ref, s_ref, g_ref, wb_ref, wo_ref, o_ref):
    merged = None
    for n, b_ref in enumerate((r_ref, m_ref, s_ref)):
        up = _dot(b_ref[...].astype(BF16), wb_ref[n])
        gate = jax.nn.sigmoid(g_ref[:, n * D_MODEL:(n + 1) * D_MODEL].astype(F32))
        merged = gate * up if merged is None else merged + gate * up
    o_ref[...] = x_ref[...] + _dot(merged.astype(BF16), wo_ref[...])


def _merge(x, r, m, s, gates, wb, wo, layer, tm):
    t = x.shape[0]
    blocks = (4 * tm * D_MODEL * 4 + 6 * tm * MIX_WIDTH * 4 + 2 * tm * N_GATES * 4
              + N_BRANCH * MIX_WIDTH * D_MODEL * 2 + D_MODEL * D_MODEL * 2 + 4 * tm * D_MODEL * 4)
    row = lambda w: pl.BlockSpec((tm, w), lambda i: (i, 0))
    if s.ndim == 3:
        assert s.shape[1] % tm == 0
        per_seq = s.shape[1] // tm
        s_spec = pl.BlockSpec((None, tm, MIX_WIDTH), lambda i: (i // per_seq, i % per_seq, 0))
    else:
        s_spec = row(MIX_WIDTH)
    return pl.pallas_call(
        _merge_kernel,
        out_shape=jax.ShapeDtypeStruct((t, D_MODEL), F32),
        grid=(t // tm,),
        in_specs=[row(D_MODEL), row(MIX_WIDTH), row(MIX_WIDTH), s_spec, row(N_GATES),
                  _resident((None, N_BRANCH, MIX_WIDTH, D_MODEL), lambda i: (layer, 0, 0, 0)),
                  _resident((None, D_MODEL, D_MODEL), lambda i: (layer, 0, 0))],
        out_specs=row(D_MODEL),
        compiler_params=_params(blocks, 1),
        name="merge",
    )(x, r, m, s, gates, wb, wo)


def _s5_disc_kernel(are_ref, aim_ref, ldt_ref, bre_ref, bim_ref, abr_ref, abi_ref, bbr_ref, bbi_ref):
    ar, ai = are_ref[...], aim_ref[...]
    dt = jnp.exp(ldt_ref[...])
    mag = jnp.exp(ar * dt)
    abr = mag * jnp.cos(ai * dt)
    abi = mag * jnp.sin(ai * dt)
    nr, ni = abr - 1.0, abi
    den = ar * ar + ai * ai
    cr = (nr * ar + ni * ai) / den
    ci = (ni * ar - nr * ai) / den
    br, bi = bre_ref[...], bim_ref[...]
    abr_ref[...] = abr
    abi_ref[...] = abi
    bbr_ref[...] = cr * br - ci * bi
    bbi_ref[...] = cr * bi + ci * br


def _s5_discretise(a_re, a_im, log_dt, b_re, b_im):
    rows = DEPTH * S5_N
    col = lambda a: a.reshape(rows, 1)
    ldt = jnp.broadcast_to(log_dt[:, :, None], (DEPTH, S5_GROUPS, S5_STATE))
    outs = pl.pallas_call(
        _s5_disc_kernel,
        out_shape=(jax.ShapeDtypeStruct((rows, 1), F32), jax.ShapeDtypeStruct((rows, 1), F32),
                   jax.ShapeDtypeStruct((rows, S5_GROUP), F32), jax.ShapeDtypeStruct((rows, S5_GROUP), F32)),
        name="s5_disc",
    )(col(a_re), col(a_im), col(ldt), b_re.reshape(rows, S5_GROUP), b_im.reshape(rows, S5_GROUP))
    abr, abi, bbr, bbi = outs
    a_row = jnp.concatenate([abr.reshape(DEPTH, 1, S5_N), abi.reshape(DEPTH, 1, S5_N)], axis=-1)
    return a_row, bbr.reshape(DEPTH, S5_GROUPS, S5_STATE, S5_GROUP), bbi.reshape(DEPTH, S5_GROUPS, S5_STATE, S5_GROUP)


def _block_diag_in(b):
    gh = S5_GROUPS // 2
    eye = jnp.eye(gh, dtype=b.dtype)
    m = jnp.einsum('lkgpc,gh->lkgchp', b.reshape(DEPTH, 2, gh, S5_STATE, S5_GROUP), eye)
    return m.reshape(DEPTH, 2, S5_HALF, S5_HALF_N).astype(BF16)


def _block_diag_out(c):
    gh = S5_GROUPS // 2
    eye = jnp.eye(gh, dtype=c.dtype)
    m = jnp.einsum('lkgcp,gh->lkhpgc', c.reshape(DEPTH, 2, gh, S5_GROUP, S5_STATE), eye)
    return m.reshape(DEPTH, 2, S5_HALF_N, S5_HALF).astype(BF16)


def _s5_project_in(u, xs_ref, bre_ref, bim_ref):
    ub = u.astype(BF16)
    for k in range(2):
        uk = ub[:, k * S5_HALF:(k + 1) * S5_HALF]
        xs_ref[:, k * S5_HALF_N:(k + 1) * S5_HALF_N] = _dot(uk, bre_ref[k])
        xs_ref[:, S5_N + k * S5_HALF_N:S5_N + (k + 1) * S5_HALF_N] = _dot(uk, bim_ref[k])


def _s5_project_out(xs_ref, u, cre_ref, cimn_ref, d_ref, wglu_ref):
    ys = []
    for k in range(2):
        xr = xs_ref[:, k * S5_HALF_N:(k + 1) * S5_HALF_N].astype(BF16)
        xi = xs_ref[:, S5_N + k * S5_HALF_N:S5_N + (k + 1) * S5_HALF_N].astype(BF16)
        ys.append(_dot(xr, cre_ref[k]) + _dot(xi, cimn_ref[k]))
    y = jnp.concatenate(ys, axis=1) + d_ref[...] * u
    y = jax.nn.gelu(y)
    return y * jax.nn.sigmoid(_dot(y.astype(BF16), wglu_ref[...]))


def _s5_scan_step(ab_ref, xs_ref, rows, xre, xim):
    are = ab_ref[:, :S5_N]
    aim = ab_ref[:, S5_N:]
    nre = are * xre - aim * xim + xs_ref[rows, :S5_N]
    nim = are * xim + aim * xre + xs_ref[rows, S5_N:]
    xs_ref[rows, :S5_N] = nre
    xs_ref[rows, S5_N:] = nim
    return nre, nim


def _s5_step_kernel(u_ref, x0_ref, a_ref, bre_ref, bim_ref, cre_ref, cimn_ref, d_ref, wglu_ref,
                    o_ref, xl_ref, xs_scr):
    u = u_ref[...]
    _s5_project_in(u, xs_scr, bre_ref, bim_ref)
    xre, xim = _s5_scan_step(a_ref, xs_scr, slice(None), x0_ref[:, :S5_N], x0_ref[:, S5_N:])
    xl_ref[:, :S5_N] = xre
    xl_ref[:, S5_N:] = xim
    o_ref[...] = _s5_project_out(xs_scr, u, cre_ref, cimn_ref, d_ref, wglu_ref)


def _s5_prompt_kernel(u_ref, un_ref, x0_ref, a_ref, bre_ref, bim_ref, cre_ref, cimn_ref, d_ref, wglu_ref,
                      o_ref, xl_ref, xs0, xs1, ut0, ut1, st_scr, ab_scr, p_in, p_out, *, nb, tl):
    g = pl.program_id(0)
    m = nb * tl

    tg = V7X_MXU_DIM // nb
    rows_g = nb * tg

    def to_token_major(ref, first):
        pm = p_in[...]
        groups = []
        for lo_t in range(first, first + tl, tg):
            us = ref[:, lo_t:lo_t + tg, :].reshape(rows_g, MIX_WIDTH)
            hi = us.astype(BF16)
            lo = (us - hi.astype(F32)).astype(BF16)
            groups.append(_dot(pm, hi) + _dot(pm, lo))
        return jnp.concatenate(groups, axis=0)

    def store_sequence_major(out, first):
        pm = p_out[...]
        for gq in range(tl // tg):
            piece = _dot(pm, out[gq * rows_g:(gq + 1) * rows_g].astype(BF16)).astype(o_ref.dtype)
            o_ref[:, first + gq * tg:first + (gq + 1) * tg, :] = piece.reshape(nb, tg, MIX_WIDTH)

    def scan(xs, xre, xim):
        for t in range(tl):
            xre, xim = _s5_scan_step(ab_scr, xs, slice(t * nb, (t + 1) * nb), xre, xim)
        return xre, xim

    @pl.when(g == 0)
    def _():
        st_scr[...] = x0_ref[...]
        ab_scr[...] = jnp.broadcast_to(a_ref[...], ab_scr.shape)
        shift = nb.bit_length() - 1
        ro = lax.broadcasted_iota(jnp.int32, (rows_g, rows_g), 0)
        ci = lax.broadcasted_iota(jnp.int32, (rows_g, rows_g), 1)
        seq_major = lambda r: (r & (nb - 1)) * tg + (r >> shift)
        p_in[...] = jnp.where(ci == seq_major(ro), 1.0, 0.0).astype(BF16)
        p_out[...] = jnp.where(ro == seq_major(ci), 1.0, 0.0).astype(BF16)
        u0 = to_token_major(u_ref, 0)
        ut0[...] = u0
        _s5_project_in(u0, xs0, bre_ref, bim_ref)

    u1 = to_token_major(u_ref, tl)
    ut1[...] = u1
    _s5_project_in(u1, xs1, bre_ref, bim_ref)
    xre, xim = scan(xs0, st_scr[:, :S5_N], st_scr[:, S5_N:])
    store_sequence_major(_s5_project_out(xs0, ut0[...], cre_ref, cimn_ref, d_ref, wglu_ref), 0)
    xre, xim = scan(xs1, xre, xim)
    st_scr[:, :S5_N] = xre
    st_scr[:, S5_N:] = xim
    un = to_token_major(un_ref, 0)
    ut0[...] = un
    _s5_project_in(un, xs0, bre_ref, bim_ref)
    store_sequence_major(_s5_project_out(xs1, u1, cre_ref, cimn_ref, d_ref, wglu_ref), tl)
    xl_ref[...] = st_scr[...]


def _s5_weight_specs(layer):
    lay3 = lambda i: (layer, 0, 0)
    lay4 = lambda i: (layer, 0, 0, 0)
    return [_resident((None, 1, 2 * S5_N), lay3),
            _resident((None, 2, S5_HALF, S5_HALF_N), lay4),
            _resident((None, 2, S5_HALF, S5_HALF_N), lay4),
            _resident((None, 2, S5_HALF_N, S5_HALF), lay4),
            _resident((None, 2, S5_HALF_N, S5_HALF), lay4),
            _resident((None, 1, MIX_WIDTH), lay3),
            _resident((None, MIX_WIDTH, MIX_WIDTH), lay3)]


S5_WEIGHT_BYTES = 4 * S5_HALF * S5_HALF_N * 2 * 2 + MIX_WIDTH * MIX_WIDTH * 2 + (2 * S5_N + MIX_WIDTH) * 4


def _s5_step(u, x0, s5w, layer):
    nb = u.shape[0]
    blocks = 4 * nb * MIX_WIDTH * 4 + 5 * nb * 2 * S5_N * 4 + S5_WEIGHT_BYTES + 6 * nb * MIX_WIDTH * 4
    whole = lambda w: pl.BlockSpec((nb, w), lambda i: (0, 0))
    return pl.pallas_call(
        _s5_step_kernel,
        out_shape=(jax.ShapeDtypeStruct(u.shape, F32), jax.ShapeDtypeStruct((nb, 2 * S5_N), F32)),
        grid=(1,),
        in_specs=[whole(MIX_WIDTH), pl.BlockSpec((None, nb, 2 * S5_N), lambda i: (layer, 0, 0))] + _s5_weight_specs(layer),
        out_specs=(whole(MIX_WIDTH), whole(2 * S5_N)),
        scratch_shapes=[pltpu.VMEM((nb, 2 * S5_N), F32)],
        compiler_params=_params(blocks, 1),
        name="s5_step",
    )(u, x0, *s5w)


def _s5_prompt(u, x0, s5w, layer):
    nb, seq, _ = u.shape
    tl = S5_TOKENS
    m = nb * tl
    steps = seq // (2 * tl)
    perm_rows = V7X_MXU_DIM
    assert nb & (nb - 1) == 0 and seq % (2 * tl) == 0 and m % perm_rows == 0
    blocks = (2 * 2 * m * MIX_WIDTH * 4 + 2 * m * MIX_WIDTH * 4 + 2 * 2 * m * MIX_WIDTH * 2 + 2 * m * 2 * S5_N * 4
              + 2 * m * MIX_WIDTH * 4 + 6 * nb * 2 * S5_N * 4 + S5_WEIGHT_BYTES + 2 * perm_rows * perm_rows * 2
              + m * 2 * S5_N * 2 + 6 * m * MIX_WIDTH * 4)
    pair = pl.BlockSpec((nb, 2 * tl, MIX_WIDTH), lambda i: (0, i, 0))
    nxt = pl.BlockSpec((nb, tl, MIX_WIDTH), lambda i: (0, jnp.minimum(2 * i + 2, 2 * steps - 1), 0))
    state = pl.BlockSpec((nb, 2 * S5_N), lambda i: (0, 0))
    return pl.pallas_call(
        functools.partial(_s5_prompt_kernel, nb=nb, tl=tl),
        out_shape=(jax.ShapeDtypeStruct(u.shape, BF16), jax.ShapeDtypeStruct((nb, 2 * S5_N), F32)),
        grid=(steps,),
        in_specs=[pair, nxt, _resident((nb, 2 * S5_N), lambda i: (0, 0))] + _s5_weight_specs(layer),
        out_specs=(pair, state),
        scratch_shapes=[pltpu.VMEM((m, 2 * S5_N), F32), pltpu.VMEM((m, 2 * S5_N), F32),
                        pltpu.VMEM((m, MIX_WIDTH), F32), pltpu.VMEM((m, MIX_WIDTH), F32),
                        pltpu.VMEM((nb, 2 * S5_N), F32), pltpu.VMEM((nb, 2 * S5_N), F32),
                        pltpu.VMEM((perm_rows, perm_rows), BF16), pltpu.VMEM((perm_rows, perm_rows), BF16)],
        compiler_params=_params(blocks, 1),
        name="s5",
    )(u, u, x0, *s5w)


def _ret_kernel(p_ref, cos_ref, sin_ref, o_ref, s_out_ref, s_scr, *, seq):
    c = CHUNK
    heads = range(HEADS)
    row = lax.broadcasted_iota(jnp.int32, (c, c), 0)
    col = lax.broadcasted_iota(jnp.int32, (c, c), 1)
    diff = (row - col).astype(F32)
    ivec = lax.broadcasted_iota(jnp.int32, (c, HEAD_DIM), 0).astype(F32)
    dmats = [jnp.where(diff >= 0, jnp.exp(lg * jnp.maximum(diff, 0.0)), 0.0) * QK_SCALE for lg in LOG_GAMMA]
    q_decs = [jnp.exp(lg * (ivec + 1.0)) for lg in LOG_GAMMA]
    k_decs = [jnp.exp(lg * (c - 1.0 - ivec)) * QK_SCALE for lg in LOG_GAMMA]
    s_scr[...] = jnp.zeros_like(s_scr)

    def chunk(j, carry):
        r = pl.ds(pl.multiple_of(j * c, c), c)
        cs = cos_ref[r, :]
        sn = sin_ref[r, :]
        sl = lambda part, h: p_ref[r, part * MIX_WIDTH + h * HEAD_DIM:part * MIX_WIDTH + (h + 1) * HEAD_DIM]
        q = [_rope(sl(0, h).astype(F32), cs, sn) for h in heads]
        k = [_rope(sl(1, h).astype(F32), cs, sn) for h in heads]
        v = [sl(2, h).astype(BF16) for h in heads]
        s = [s_scr[h] for h in heads]
        qb = [q[h].astype(BF16) for h in heads]
        kb = [k[h].astype(BF16) for h in heads]
        att = [_dot_nt(qb[h], kb[h]) for h in heads]
        inter = [_dot((q[h] * q_decs[h]).astype(BF16), s[h].astype(BF16)) for h in heads]
        upd = [_dot_tn((k[h] * k_decs[h]).astype(BF16), v[h]) for h in heads]
        attb = [(att[h] * dmats[h]).astype(BF16) for h in heads]
        o = [_dot(attb[h], v[h]) + inter[h] for h in heads]
        for h in heads:
            s_scr[h] = s[h] * math.exp(LOG_GAMMA[h] * c) + upd[h]
        on = [_head_norm(o[h]) for h in heads]
        for h in heads:
            g = sl(3, h).astype(F32)
            o_ref[r, h * HEAD_DIM:(h + 1) * HEAD_DIM] = (on[h] * (g * jax.nn.sigmoid(g))).astype(o_ref.dtype)
        return carry

    lax.fori_loop(0, seq // c, chunk, 0, unroll=2)
    s_out_ref[...] = s_scr[...]


def _retention_prompt(p, cos, sin, b, seq):
    half = 4 * MIX_WIDTH
    blocks = 2 * seq * half * 2 + 2 * seq * V7X_LANES * 4 + 2 * seq * MIX_WIDTH * 2 + 3 * HEADS * HEAD_DIM * HEAD_DIM * 4
    return pl.pallas_call(
        functools.partial(_ret_kernel, seq=seq),
        out_shape=(jax.ShapeDtypeStruct((b * seq, MIX_WIDTH), BF16),
                   jax.ShapeDtypeStruct((b, HEADS, HEAD_DIM, HEAD_DIM), F32)),
        grid=(b,),
        in_specs=[pl.BlockSpec((seq, half), lambda i: (i, 0)),
                  _resident((seq, HEAD_DIM), lambda i: (0, 0)),
                  _resident((seq, HEAD_DIM), lambda i: (0, 0))],
        out_specs=(pl.BlockSpec((seq, MIX_WIDTH), lambda i: (i, 0)),
                   pl.BlockSpec((None, HEADS, HEAD_DIM, HEAD_DIM), lambda i: (i, 0, 0, 0))),
        scratch_shapes=[pltpu.VMEM((HEADS, HEAD_DIM, HEAD_DIM), F32)],
        compiler_params=_params(blocks, 1),
        name="retention",
    )(p, cos, sin)


N_REP = 3 * HEADS


def _split_dot(x, w):
    hi = x.astype(BF16)
    lo = (x - hi.astype(F32)).astype(BF16)
    return _dot(hi, w) + _dot(lo, w)


def _split_dot_tn(x, w):
    hi = x.astype(BF16)
    lo = (x - hi.astype(F32)).astype(BF16)
    return _dot_tn(hi, w) + _dot_tn(lo, w)


def _mlstm_kernel(p_ref, gif_ref, bias_ref, o_ref, c_out_ref, n_out_ref, m_out_ref,
                  cn_scr, m_scr, rows_scr, rep_scr, *, seq):
    c = CHUNK
    nch = seq // c
    nrow = 8 * nch
    heads = range(HEADS)
    neg_inf = float("-inf")
    row = lax.broadcasted_iota(jnp.int32, (c, c), 0)
    col = lax.broadcasted_iota(jnp.int32, (c, c), 1)
    causal = row >= col
    triu = jnp.where(row <= col, 1.0, 0.0).astype(BF16)
    lane = lax.broadcasted_iota(jnp.int32, (c, V7X_LANES), 1)
    ones = jnp.ones((c, V7X_LANES), BF16)
    er = lax.broadcasted_iota(jnp.int32, (16, N_REP * V7X_LANES), 0)
    ec = lax.broadcasted_iota(jnp.int32, (16, N_REP * V7X_LANES), 1)
    expand = jnp.where(ec // V7X_LANES == er, 1.0, 0.0).astype(BF16)
    cn_scr[...] = jnp.zeros_like(cn_scr)
    m_scr[...] = jnp.zeros_like(m_scr)

    pieces = []
    for j in range(nch):
        fb = gif_ref[j * c:(j + 1) * c, :] + bias_ref[...]
        gl = jnp.where(lane < HEADS, fb, jax.nn.log_sigmoid(fb))
        pieces.append(gl.T[0:8, :])
    x = jnp.concatenate(pieces, axis=0)
    cum = _split_dot(x, triu)
    sub = lax.broadcasted_iota(jnp.int32, (nrow, c), 0) % 8
    tok = lax.broadcasted_iota(jnp.int32, (nrow, c), 1)
    a = x - pltpu.roll(cum, nrow - HEADS, 0)
    d = 1
    while d < c:
        a = jnp.maximum(a, jnp.where(tok >= d, pltpu.roll(a, d, 1), neg_inf))
        d *= 2
    rows_scr[0] = jnp.where(sub < HEADS, x, cum)
    rows_scr[1] = a

    def replicate(j):
        r8 = pl.ds(pl.multiple_of(j * 8, 8), 8)
        r16 = jnp.concatenate([rows_scr[0, r8, :], rows_scr[1, r8, :]], axis=0)
        return _split_dot_tn(r16, expand)

    rep_scr[0] = replicate(0)

    def chunk(j, carry):
        r = pl.ds(pl.multiple_of(j * c, c), c)
        r8 = pl.ds(pl.multiple_of(j * 8, 8), 8)
        slot = j % 2
        rep_next = replicate(jnp.minimum(j + 1, nch - 1))
        sl = lambda part, h: p_ref[r, part * MIX_WIDTH + h * HEAD_DIM:part * MIX_WIDTH + (h + 1) * HEAD_DIM]
        blk = lambda n, h: rep_scr[slot, :, (n * HEADS + h) * V7X_LANES:(n * HEADS + h + 1) * V7X_LANES]
        y8 = rows_scr[0, r8, :]
        q = [sl(0, h).astype(BF16) for h in heads]
        k = [sl(1, h).astype(BF16) for h in heads]
        v1 = [jnp.concatenate([sl(2, h).astype(BF16), ones], axis=1) for h in heads]
        cn = [cn_scr[h] for h in heads]
        m_prev = [m_scr[h] for h in heads]
        ic = [blk(0, h) for h in heads]
        bc = [blk(1, h) for h in heads]
        cmc = [blk(2, h) for h in heads]
        qk = [_dot_nt(q[h], k[h]) for h in heads]
        qcn = [_dot(q[h], cn[h].astype(BF16)) for h in heads]
        b_last = [bc[h][c - 1:c, :] for h in heads]
        wk_log = [b_last[h] - bc[h] + ic[h] for h in heads]
        m_new = [jnp.maximum(b_last[h] + m_prev[h], jnp.max(wk_log[h], axis=0, keepdims=True)) for h in heads]
        wk = [jnp.exp(wk_log[h] - m_new[h]) * QK_SCALE for h in heads]
        decay = [jnp.exp(b_last[h] + m_prev[h] - m_new[h]) for h in heads]
        kw = [(k[h].astype(F32) * wk[h]).astype(BF16) for h in heads]
        upd = [_dot_tn(kw[h], v1[h]) for h in heads]
        m_t = [bc[h] + jnp.maximum(m_prev[h], cmc[h]) for h in heads]
        dlog = [jnp.where(causal, bc[h] - y8[HEADS + h:HEADS + h + 1, :] + y8[h:h + 1, :], neg_inf) for h in heads]
        w_intra = [jnp.exp(dlog[h] - m_t[h]) * QK_SCALE for h in heads]
        w_inter = [jnp.exp(bc[h] + m_prev[h] - m_t[h]) for h in heads]
        s = [(qk[h] * w_intra[h]).astype(BF16) for h in heads]
        sv = [_dot(s[h], v1[h]) for h in heads]
        hh = []
        for h in heads:
            num = sv[h][:, :HEAD_DIM] + w_inter[h] * qcn[h][:, :HEAD_DIM]
            den = sv[h][:, HEAD_DIM:] + w_inter[h] * qcn[h][:, HEAD_DIM:]
            hh.append(num / jnp.maximum(jnp.abs(den), jnp.exp(-m_t[h])))
        for h in heads:
            cn_scr[h] = jnp.concatenate([decay[h], decay[h]], axis=1) * cn[h] + upd[h]
            m_scr[h] = m_new[h]
        on = [_head_norm(hh[h]) for h in heads]
        for h in heads:
            o_ref[r, h * HEAD_DIM:(h + 1) * HEAD_DIM] = (on[h] * jax.nn.sigmoid(sl(3, h).astype(F32))).astype(o_ref.dtype)
        rep_scr[1 - slot] = rep_next
        return carry

    lax.fori_loop(0, nch, chunk, 0, unroll=2)
    for h in heads:
        cnh = cn_scr[h]
        c_out_ref[h] = cnh[:, :HEAD_DIM]
        n_out_ref[h:h + 1, :] = cnh[:, HEAD_DIM:].T[0:1, :]
        m_out_ref[h:h + 1, :] = m_scr[h]


def _mlstm_prompt(p, gif, bias, layer, b, seq):
    half = 4 * MIX_WIDTH
    nch = seq // CHUNK
    blocks = (2 * seq * half * 2 + 2 * seq * N_GIF * 4 + 2 * seq * MIX_WIDTH * 2
              + 5 * HEADS * HEAD_DIM * 2 * HEAD_DIM * 4 + 4 * CHUNK * N_REP * V7X_LANES * 4)
    return pl.pallas_call(
        functools.partial(_mlstm_kernel, seq=seq),
        out_shape=(jax.ShapeDtypeStruct((b * seq, MIX_WIDTH), BF16),
                   jax.ShapeDtypeStruct((b, HEADS, HEAD_DIM, HEAD_DIM), F32),
                   jax.ShapeDtypeStruct((b, HEADS, HEAD_DIM), F32),
                   jax.ShapeDtypeStruct((b, HEADS, V7X_LANES), F32)),
        grid=(b,),
        in_specs=[pl.BlockSpec((seq, half), lambda i: (i, 1)),
                  pl.BlockSpec((seq, N_GIF), lambda i: (i, 0)),
                  _resident((None, 1, N_GIF), lambda i: (layer, 0, 0))],
        out_specs=(pl.BlockSpec((seq, MIX_WIDTH), lambda i: (i, 0)),
                   pl.BlockSpec((None, HEADS, HEAD_DIM, HEAD_DIM), lambda i: (i, 0, 0, 0)),
                   pl.BlockSpec((None, HEADS, HEAD_DIM), lambda i: (i, 0, 0)),
                   pl.BlockSpec((None, HEADS, V7X_LANES), lambda i: (i, 0, 0))),
        scratch_shapes=[pltpu.VMEM((HEADS, HEAD_DIM, 2 * HEAD_DIM), F32),
                        pltpu.VMEM((HEADS, 1, V7X_LANES), F32),
                        pltpu.VMEM((2, 8 * nch, CHUNK), F32),
                        pltpu.VMEM((2, CHUNK, N_REP * V7X_LANES), F32)],
        compiler_params=_params(blocks, 1),
        name="mlstm",
    )(p, gif, bias)


SEQ_BLOCK = 8


def _column_expander():
    er = lax.broadcasted_iota(jnp.int32, (SEQ_BLOCK, SEQ_BLOCK * V7X_LANES), 0)
    ec = lax.broadcasted_iota(jnp.int32, (SEQ_BLOCK, SEQ_BLOCK * V7X_LANES), 1)
    return jnp.where(ec // V7X_LANES == er, 1.0, 0.0).astype(BF16)


def _columns(x, expander):
    return _dot_tn(x.astype(BF16), expander)


def _ret_step_kernel(p_ref, cos_ref, sin_ref, s_ref, *rest, layer, first):
    o_ref, s_out_ref, row_scr = rest[-3:]
    s_out_ref = _own_layer(s_out_ref, layer, first)
    cs = cos_ref[...]
    sn = sin_ref[...]
    expander = _column_expander()
    for h in range(HEADS):
        lo = h * HEAD_DIM
        gamma = math.exp(LOG_GAMMA[h])
        q = _rope(p_ref[:, lo:lo + HEAD_DIM], cs, sn)
        k = _rope(p_ref[:, MIX_WIDTH + lo:MIX_WIDTH + lo + HEAD_DIM], cs, sn) * QK_SCALE
        v = p_ref[:, 2 * MIX_WIDTH + lo:2 * MIX_WIDTH + lo + HEAD_DIM]
        g = p_ref[:, 3 * MIX_WIDTH + lo:3 * MIX_WIDTH + lo + HEAD_DIM]
        qt = _columns(q * gamma, expander)
        kt = _columns(k, expander)
        for b in range(SEQ_BLOCK):
            s = s_ref[b, h]
            blk = slice(b * V7X_LANES, (b + 1) * V7X_LANES)
            row_scr[b:b + 1, :] = jnp.sum(qt[:, blk] * s, axis=0, keepdims=True)
            s_out_ref[b, h] = gamma * s + kt[:, blk] * v[b:b + 1, :]
        o = jnp.sum(q * k, axis=-1, keepdims=True) * v + row_scr[...]
        o_ref[:, lo:lo + HEAD_DIM] = _head_norm(o) * (g * jax.nn.sigmoid(g))


def _stacked_state(acc, layer):
    tail = (HEADS, HEAD_DIM, HEAD_DIM)
    if acc is None:
        return (), [], pl.BlockSpec((DEPTH, SEQ_BLOCK) + tail, lambda i: (0, i, 0, 0, 0))
    return (acc,), [pl.BlockSpec(memory_space=pl.ANY)], pl.BlockSpec((None, SEQ_BLOCK) + tail, lambda i: (layer, i, 0, 0, 0))


def _own_layer(state_out_ref, layer, first):
    if not first:
        return state_out_ref
    for other in range(DEPTH):
        if other != layer:
            state_out_ref[other] = jnp.zeros(state_out_ref.shape[1:], F32)
    return state_out_ref.at[layer]


def _retention_step(p, cos, sin, state, layer, acc):
    nseq = p.shape[0]
    half = 4 * MIX_WIDTH
    sblk = SEQ_BLOCK * HEADS * HEAD_DIM * HEAD_DIM * 4
    blocks = (2 + 2 * (DEPTH if acc is None else 1)) * sblk + 2 * SEQ_BLOCK * half * 4 + 2 * SEQ_BLOCK * MIX_WIDTH * 4
    state_blk = pl.BlockSpec((None, SEQ_BLOCK, HEADS, HEAD_DIM, HEAD_DIM), lambda i: (layer, i, 0, 0, 0))
    extra, extra_specs, out_state_blk = _stacked_state(acc, layer)
    return pl.pallas_call(
        functools.partial(_ret_step_kernel, layer=layer, first=acc is None),
        out_shape=(jax.ShapeDtypeStruct((nseq, MIX_WIDTH), F32),
                   jax.ShapeDtypeStruct(state.shape, F32)),
        grid=(nseq // SEQ_BLOCK,),
        in_specs=[pl.BlockSpec((SEQ_BLOCK, half), lambda i: (i, 0)),
                  _resident((1, HEAD_DIM), lambda i: (0, 0)),
                  _resident((1, HEAD_DIM), lambda i: (0, 0)),
                  state_blk] + extra_specs,
        out_specs=(pl.BlockSpec((SEQ_BLOCK, MIX_WIDTH), lambda i: (i, 0)), out_state_blk),
        input_output_aliases={4: 1} if extra else {},
        scratch_shapes=[pltpu.VMEM((SEQ_BLOCK, HEAD_DIM), F32)],
        compiler_params=_params(blocks, 1),
        name="retention_step",
    )(p, cos, sin, state, *extra)


def _mlstm_step_kernel(p_ref, gif_ref, bias_ref, c_ref, n_ref, m_ref, *rest, layer, first):
    o_ref, c_out_ref, n_out_ref, m_out_ref, row_scr = rest[-5:]
    c_out_ref = _own_layer(c_out_ref, layer, first)
    fb = gif_ref[...] + bias_ref[...]
    lf_all = jax.nn.log_sigmoid(fb)
    lane = lax.broadcasted_iota(jnp.int32, (SEQ_BLOCK, V7X_LANES), 1)
    m_all = m_ref[...]
    m_acc = jnp.zeros((SEQ_BLOCK, V7X_LANES), F32)
    expander = _column_expander()
    for h in range(HEADS):
        lo = h * HEAD_DIM
        q = p_ref[:, lo:lo + HEAD_DIM]
        k = p_ref[:, MIX_WIDTH + lo:MIX_WIDTH + lo + HEAD_DIM] * QK_SCALE
        v = p_ref[:, 2 * MIX_WIDTH + lo:2 * MIX_WIDTH + lo + HEAD_DIM]
        og = p_ref[:, 3 * MIX_WIDTH + lo:3 * MIX_WIDTH + lo + HEAD_DIM]
        ig = fb[:, h:h + 1]
        inter = lf_all[:, HEADS + h:HEADS + h + 1] + m_all[:, h:h + 1]
        m_t = jnp.maximum(inter, ig)
        w_intra = jnp.exp(ig - m_t)
        w_inter = jnp.exp(inter - m_t)
        nv = n_ref[:, lo:lo + HEAD_DIM]
        kw = k * w_intra
        qt = _columns(q, expander)
        kt = _columns(kw, expander)
        for b in range(SEQ_BLOCK):
            cm = c_ref[b, h]
            blk = slice(b * V7X_LANES, (b + 1) * V7X_LANES)
            row_scr[b:b + 1, :] = jnp.sum(qt[:, blk] * cm, axis=0, keepdims=True)
            c_out_ref[b, h] = w_inter[b:b + 1, :] * cm + kt[:, blk] * v[b:b + 1, :]
        s = jnp.sum(q * k, axis=-1, keepdims=True) * w_intra
        num = s * v + w_inter * row_scr[...]
        den = s + w_inter * jnp.sum(q * nv, axis=-1, keepdims=True)
        hh = num / jnp.maximum(jnp.abs(den), jnp.exp(-m_t))
        n_out_ref[:, lo:lo + HEAD_DIM] = w_inter * nv + kw
        m_acc = jnp.where(lane == h, m_t, m_acc)
        o_ref[:, lo:lo + HEAD_DIM] = _head_norm(hh) * jax.nn.sigmoid(og)
    m_out_ref[...] = m_acc


def _mlstm_step(p, gif, bias, c_state, n_state, m_state, layer, acc):
    nseq = p.shape[0]
    half = 4 * MIX_WIDTH
    sblk = SEQ_BLOCK * HEADS * HEAD_DIM * HEAD_DIM * 4
    blocks = (2 + 2 * (DEPTH if acc is None else 1)) * sblk + 2 * SEQ_BLOCK * half * 4 + 8 * SEQ_BLOCK * MIX_WIDTH * 4
    rows = lambda w: pl.BlockSpec((SEQ_BLOCK, w), lambda i: (i, 0))
    state_blk = pl.BlockSpec((None, SEQ_BLOCK, HEADS, HEAD_DIM, HEAD_DIM), lambda i: (layer, i, 0, 0, 0))
    extra, extra_specs, out_state_blk = _stacked_state(acc, layer)
    return pl.pallas_call(
        functools.partial(_mlstm_step_kernel, layer=layer, first=acc is None),
        out_shape=(jax.ShapeDtypeStruct((nseq, MIX_WIDTH), F32),
                   jax.ShapeDtypeStruct(c_state.shape, F32),
                   jax.ShapeDtypeStruct((nseq, MIX_WIDTH), F32),
                   jax.ShapeDtypeStruct((nseq, V7X_LANES), F32)),
        grid=(nseq // SEQ_BLOCK,),
        in_specs=[pl.BlockSpec((SEQ_BLOCK, half), lambda i: (i, 1)),
                  rows(N_GIF),
                  _resident((None, 1, N_GIF), lambda i: (layer, 0, 0)),
                  state_blk,
                  pl.BlockSpec((None, SEQ_BLOCK, MIX_WIDTH), lambda i: (layer, i, 0)),
                  pl.BlockSpec((None, SEQ_BLOCK, HEADS), lambda i: (layer, i, 0))] + extra_specs,
        out_specs=(rows(MIX_WIDTH), out_state_blk, rows(MIX_WIDTH), rows(V7X_LANES)),
        input_output_aliases={6: 1} if extra else {},
        scratch_shapes=[pltpu.VMEM((SEQ_BLOCK, HEAD_DIM), F32)],
        compiler_params=_params(blocks, 1),
        name="mlstm_step",
    )(p, gif, bias, c_state, n_state, m_state, *extra)


def _rope_tables(pos):
    inv = ROPE_THETA ** (-jnp.arange(0, HEAD_DIM, 2, dtype=F32) / HEAD_DIM)
    ang = pos.astype(F32)[:, None] * inv[None, :]
    cos, sin = jnp.cos(ang), jnp.sin(ang)
    return jnp.concatenate([cos, cos], axis=-1), jnp.concatenate([-sin, sin], axis=-1)


def _w_in_layout_kernel(w_ref, o_ref):
    off_su = N_MAIN + 2 * HEADS
    o_ref[:, :N_MAIN] = w_ref[:, :N_MAIN].astype(BF16)
    o_ref[:, OFF_GATES:OFF_GATES + N_GATES] = w_ref[:, off_su + MIX_WIDTH:off_su + MIX_WIDTH + N_GATES].astype(BF16)
    o_ref[:, OFF_SU:OFF_SU + MIX_WIDTH] = w_ref[:, off_su:off_su + MIX_WIDTH].astype(BF16)
    tail = w_ref[:, N_MAIN:N_MAIN + N_GIF]
    lane = lax.broadcasted_iota(jnp.int32, tail.shape, 1)
    o_ref[:, OFF_GIF:] = jnp.where(lane < 2 * HEADS, tail, 0.0).astype(BF16)


def _w_in_layout(w_in):
    depth, rows, n_in = w_in.shape
    tr = 256
    blocks = 2 * tr * n_in * 4 + 2 * tr * N_IN_PADDED * 2 + 2 * tr * N_IN_PADDED * 4
    return pl.pallas_call(
        _w_in_layout_kernel,
        out_shape=jax.ShapeDtypeStruct((depth, rows, N_IN_PADDED), BF16),
        grid=(depth, rows // tr),
        in_specs=[pl.BlockSpec((None, tr, n_in), lambda l, i: (l, i, 0))],
        out_specs=pl.BlockSpec((None, tr, N_IN_PADDED), lambda l, i: (l, i, 0)),
        compiler_params=_params(blocks, 2),
        name="w_in_layout",
    )(w_in)


def _prepare_weights(w_ffn1_gu, w_ffn1_down, w_in, w_s5_glu, w_branch, w_out, w_ffn2_gu, w_ffn2_down):
    cast = lambda w: w.astype(BF16)
    return dict(gu1=cast(w_ffn1_gu), d1=cast(w_ffn1_down), w_in=_w_in_layout(w_in), glu=cast(w_s5_glu),
                branch=cast(w_branch), out=cast(w_out), gu2=cast(w_ffn2_gu), d2=cast(w_ffn2_down))


def _run_trunk(x, nseq, seq, states, wts, norms, s5w, bias, ln_final, prompt):
    t = nseq * seq
    tm = 512 if t % 512 == 0 else t
    act_dtype = BF16 if prompt else F32
    if prompt:
        cos, sin = _rope_tables(jnp.arange(seq, dtype=jnp.int32))
    else:
        cos, sin = _rope_tables(PAST_LEN + jnp.arange(seq, dtype=jnp.int32))
    if prompt:
        x0 = jnp.zeros((nseq, 2 * S5_N), F32)
    else:
        st_ret, st_mc, st_mn, st_mm, st_sre, st_sim = states
        st_mn = st_mn.reshape(DEPTH, nseq, MIX_WIDTH)
        x0 = jnp.concatenate([st_sre.reshape(DEPTH, nseq, S5_N), st_sim.reshape(DEPTH, nseq, S5_N)], axis=-1)
    rets, mcs, mns, mms, xls = [], [], [], [], []
    ret_acc = mc_acc = None
    for l in range(DEPTH):
        x = _ffn(x, norms["ffn1"], wts["gu1"], wts["d1"], ln_final, l, tm, final=False)
        main, gates, su, gif = _inproj(x, norms["mix"], wts["w_in"], l, tm, act_dtype)
        if prompt:
            r_out, new_ret = _retention_prompt(main, cos, sin, nseq, seq)
            m_out, new_mc, new_mn, new_mm = _mlstm_prompt(main, gif, bias, l, nseq, seq)
            s_out, x_last = _s5_prompt(su.reshape(nseq, seq, MIX_WIDTH), x0, s5w, l)
            rets.append(new_ret)
            mcs.append(new_mc)
        else:
            r_out, ret_acc = _retention_step(main, cos, sin, st_ret, l, ret_acc)
            m_out, mc_acc, new_mn, new_mm = _mlstm_step(main, gif, bias, st_mc, st_mn, st_mm, l, mc_acc)
            s_out, x_last = _s5_step(su, x0, s5w, l)
        mns.append(new_mn)
        mms.append(new_mm)
        xls.append(x_last)
        x = _merge(x, r_out, m_out, s_out, gates, wts["branch"], wts["out"], l, tm)
        x = _ffn(x, norms["ffn2"], wts["gu2"], wts["d2"], ln_final, l, tm, final=(l == DEPTH - 1))
    new_ret, new_mc = (jnp.stack(rets), jnp.stack(mcs)) if prompt else (ret_acc, mc_acc)
    new_mn = jnp.stack(mns).reshape(DEPTH, nseq, HEADS, HEAD_DIM)
    new_mm = jnp.stack(mms)[:, :, :, 0] if prompt else jnp.stack(mms)[:, :, :HEADS]
    xl = jnp.stack(xls)
    new_sre = xl[:, :, :S5_N].reshape(DEPTH, nseq, S5_GROUPS, S5_STATE)
    new_sim = xl[:, :, S5_N:].reshape(DEPTH, nseq, S5_GROUPS, S5_STATE)
    return x, (new_ret, new_mc, new_mn, new_mm, new_sre, new_sim)


def kernel(x_prompt, x_sample, state_ret, state_mlstm_c, state_mlstm_n, state_mlstm_m, state_s5_re, state_s5_im, ln_ffn1, w_ffn1_gu, w_ffn1_down, ln_mix, w_in, b_gates, s5_a_re, s5_a_im, s5_b_re, s5_b_im, s5_c_re, s5_c_im, s5_d, s5_log_dt, w_s5_glu, w_branch, w_out, ln_ffn2, w_ffn2_gu, w_ffn2_down, ln_final):
    wts = _prepare_weights(w_ffn1_gu, w_ffn1_down, w_in, w_s5_glu, w_branch, w_out, w_ffn2_gu, w_ffn2_down)
    norms = dict(ffn1=ln_ffn1.reshape(DEPTH, 1, D_MODEL), mix=ln_mix.reshape(DEPTH, 1, D_MODEL),
                 ffn2=ln_ffn2.reshape(DEPTH, 1, D_MODEL))
    lnf = ln_final.reshape(1, D_MODEL)
    a_row, bbr, bbi = _s5_discretise(s5_a_re, s5_a_im, s5_log_dt, s5_b_re, s5_b_im)
    s5w = (a_row, _block_diag_in(bbr), _block_diag_in(bbi), _block_diag_out(s5_c_re), _block_diag_out(-s5_c_im),
           s5_d.reshape(DEPTH, 1, MIX_WIDTH), wts["glu"])
    bias = jnp.pad(b_gates, ((0, 0), (0, N_GIF - 2 * HEADS))).reshape(DEPTH, 1, N_GIF)

    pb, pl_len, _ = x_prompt.shape
    sb, sl, _ = x_sample.shape
    y_p, st_p = _run_trunk(x_prompt.reshape(pb * pl_len, D_MODEL), pb, pl_len, None, wts, norms, s5w, bias, lnf, True)
    sample_states = (state_ret, state_mlstm_c, state_mlstm_n, state_mlstm_m, state_s5_re, state_s5_im)
    y_s, st_s = _run_trunk(x_sample.reshape(sb * sl, D_MODEL), sb, sl, sample_states, wts, norms, s5w, bias, lnf, False)
    return (y_p.reshape(pb, pl_len, D_MODEL), y_s.reshape(sb, sl, D_MODEL)) + st_p + st_s
```

```python
import functools
import math

import jax
import jax.numpy as jnp
from jax import lax
from jax.experimental import pallas as pl
from jax.experimental.pallas import tpu as pltpu

F32 = jnp.float32
BF16 = jnp.bfloat16

D_MODEL = 1024
DEPTH = 4
MIX_WIDTH = D_MODEL // 2
N_BRANCH = 3
HEADS = 4
HEAD_DIM = MIX_WIDTH // HEADS
S5_GROUP = 16
S5_GROUPS = MIX_WIDTH // S5_GROUP
S5_STATE = 64
S5_N = S5_GROUPS * S5_STATE
D_FF = 2816
CHUNK = 128
PAST_LEN = 16384
ROPE_THETA = 10000.0
EPS = 1e-6

N_MAIN = 8 * MIX_WIDTH
N_GATES = N_BRANCH * D_MODEL
N_GIF = 128
W_IN_ROWS = 512
N_IN_PADDED = N_MAIN + N_GATES + MIX_WIDTH + W_IN_ROWS
OFF_GATES = N_MAIN
OFF_SU = N_MAIN + N_GATES
OFF_GIF = OFF_SU + MIX_WIDTH

V7X_LANES = 128
V7X_MXU_DIM = 256
V7X_VMEM_BYTES = 64 * 1024 * 1024

FF_CHUNK = V7X_MXU_DIM
PROJ_CHUNK = 2 * V7X_MXU_DIM
S5_HALF = MIX_WIDTH // 2
S5_HALF_N = S5_N // 2
S5_TOKENS = 64
LOG_GAMMA = tuple(math.log1p(-(2.0 ** (-5.0 - h))) for h in range(HEADS))
QK_SCALE = HEAD_DIM ** -0.5


def _vmem_limit(block_bytes):
    want = int(block_bytes * 1.25) + (8 << 20)
    return min(want, V7X_VMEM_BYTES - (6 << 20))


def _params(block_bytes, n_grid):
    return pltpu.CompilerParams(
        dimension_semantics=("arbitrary",) * n_grid,
        vmem_limit_bytes=_vmem_limit(block_bytes))


def _resident(shape, index_map):
    return pl.BlockSpec(shape, index_map, pipeline_mode=pl.Buffered(1))


def _dot(a, b):
    return jnp.dot(a, b, preferred_element_type=F32)


def _dot_nt(a, b):
    return lax.dot_general(a, b, (((1,), (1,)), ((), ())), preferred_element_type=F32)


def _dot_tn(a, b):
    return lax.dot_general(a, b, (((0,), (0,)), ((), ())), preferred_element_type=F32)


def _rms(x, g):
    return x * lax.rsqrt(jnp.mean(x * x, axis=-1, keepdims=True) + EPS) * g


def _head_norm(x):
    mu = jnp.mean(x, axis=-1, keepdims=True)
    xc = x - mu
    var = jnp.mean(xc * xc, axis=-1, keepdims=True)
    return xc * lax.rsqrt(var + EPS)


def _rope(x, cos, sin_signed):
    return x * cos + pltpu.roll(x, HEAD_DIM // 2, 1) * sin_signed


def _ffn_kernel(x_ref, ln_ref, wgu_ref, wd_ref, lnf_ref, o_ref, acc_ref, *, final):
    x = x_ref[...]
    xn = _rms(x, ln_ref[...]).astype(BF16)
    for c in range(D_FF // FF_CHUNK):
        lo = c * FF_CHUNK
        g = _dot(xn, wgu_ref[:, lo:lo + FF_CHUNK])
        u = _dot(xn, wgu_ref[:, D_FF + lo:D_FF + lo + FF_CHUNK])
        a = (g * jax.nn.sigmoid(g) * u).astype(BF16)
        d = _dot(a, wd_ref[lo:lo + FF_CHUNK, :])
        if c == 0:
            acc_ref[...] = d
        else:
            acc_ref[...] += d
    y = x + 0.5 * acc_ref[...]
    if final:
        y = _rms(y, lnf_ref[...])
    o_ref[...] = y


def _ffn(x, ln, wgu, wd, lnf, layer, tm, final):
    t = x.shape[0]
    blocks = (4 * tm * D_MODEL * 4 + tm * D_MODEL * 4 + D_MODEL * 2 * D_FF * 2 + D_FF * D_MODEL * 2
              + 4 * tm * FF_CHUNK * 4)
    return pl.pallas_call(
        functools.partial(_ffn_kernel, final=final),
        out_shape=jax.ShapeDtypeStruct((t, D_MODEL), F32),
        grid=(t // tm,),
        in_specs=[
            pl.BlockSpec((tm, D_MODEL), lambda i: (i, 0)),
            _resident((None, 1, D_MODEL), lambda i: (layer, 0, 0)),
            _resident((None, D_MODEL, 2 * D_FF), lambda i: (layer, 0, 0)),
            _resident((None, D_FF, D_MODEL), lambda i: (layer, 0, 0)),
            _resident((1, D_MODEL), lambda i: (0, 0)),
        ],
        out_specs=pl.BlockSpec((tm, D_MODEL), lambda i: (i, 0)),
        scratch_shapes=[pltpu.VMEM((tm, D_MODEL), F32)],
        compiler_params=_params(blocks, 1),
        name="ffn",
    )(x, ln, wgu, wd, lnf)


def _inproj_kernel(x_ref, ln_ref, w_ref, main_ref, gates_ref, su_ref, gif_ref):
    xn = _rms(x_ref[...], ln_ref[...]).astype(BF16)
    for c in range(N_MAIN // PROJ_CHUNK):
        lo = c * PROJ_CHUNK
        main_ref[:, lo:lo + PROJ_CHUNK] = _dot_nt(xn, w_ref[lo:lo + PROJ_CHUNK, :]).astype(main_ref.dtype)
    for c in range(N_GATES // PROJ_CHUNK):
        lo = c * PROJ_CHUNK
        gates_ref[:, lo:lo + PROJ_CHUNK] = _dot_nt(
            xn, w_ref[OFF_GATES + lo:OFF_GATES + lo + PROJ_CHUNK, :]).astype(gates_ref.dtype)
    su_ref[...] = _dot_nt(xn, w_ref[OFF_SU:OFF_SU + MIX_WIDTH, :])
    gif_ref[...] = _dot_nt(xn, w_ref[OFF_GIF:OFF_GIF + N_GIF, :])


def _inproj(x, ln, w, layer, tm, act_dtype):
    t = x.shape[0]
    ab = jnp.dtype(act_dtype).itemsize
    blocks = (2 * tm * D_MODEL * 4 + D_MODEL * N_IN_PADDED * 2
              + 2 * tm * (N_MAIN + N_GATES) * ab + 2 * tm * (MIX_WIDTH + N_GIF) * 4
              + 4 * tm * PROJ_CHUNK * 4)
    return pl.pallas_call(
        _inproj_kernel,
        out_shape=(jax.ShapeDtypeStruct((t, N_MAIN), act_dtype),
                   jax.ShapeDtypeStruct((t, N_GATES), act_dtype),
                   jax.ShapeDtypeStruct((t, MIX_WIDTH), F32),
                   jax.ShapeDtypeStruct((t, N_GIF), F32)),
        grid=(t // tm,),
        in_specs=[
            pl.BlockSpec((tm, D_MODEL), lambda i: (i, 0)),
            _resident((None, 1, D_MODEL), lambda i: (layer, 0, 0)),
            _resident((None, N_IN_PADDED, D_MODEL), lambda i: (layer, 0, 0)),
        ],
        out_specs=(pl.BlockSpec((tm, N_MAIN), lambda i: (i, 0)),
                   pl.BlockSpec((tm, N_GATES), lambda i: (i, 0)),
                   pl.BlockSpec((tm, MIX_WIDTH), lambda i: (i, 0)),
                   pl.BlockSpec((tm, N_GIF), lambda i: (i, 0))),
        compiler_params=_params(blocks, 1),
        name="inproj",
    )(x, ln, w)


def _merge_kernel(x_ref, r_ref, m_ref, s_ref, g_ref, wb_ref, wo_ref, o_ref):
    merged = None
    for n, b_ref in enumerate((r_ref, m_ref, s_ref)):
        up = _dot(b_ref[...].astype(BF16), wb_ref[n])
        gate = jax.nn.sigmoid(g_ref[:, n * D_MODEL:(n + 1) * D_MODEL].astype(F32))
        merged = gate * up if merged is None else merged + gate * up
    o_ref[...] = x_ref[...] + _dot(merged.astype(BF16), wo_ref[...])


def _merge(x, r, m, s, gates, wb, wo, layer, tm):
    t = x.shape[0]
    blocks = (4 * tm * D_MODEL * 4 + 6 * tm * MIX_WIDTH * 4 + 2 * tm * N_GATES * 4
              + N_BRANCH * MIX_WIDTH * D_MODEL * 2 + D_MODEL * D_MODEL * 2 + 4 * tm * D_MODEL * 4)
    row = lambda w: pl.BlockSpec((tm, w), lambda i: (i, 0))
    if s.ndim == 3:
        assert s.shape[1] % tm == 0
        per_seq = s.shape[1] // tm
        s_spec = pl.BlockSpec((None, tm, MIX_WIDTH), lambda i: (i // per_seq, i % per_seq, 0))
    else:
        s_spec = row(MIX_WIDTH)
    return pl.pallas_call(
        _merge_kernel,
        out_shape=jax.ShapeDtypeStruct((t, D_MODEL), F32),
        grid=(t // tm,),
        in_specs=[row(D_MODEL), row(MIX_WIDTH), row(MIX_WIDTH), s_spec, row(N_GATES),
                  _resident((None, N_BRANCH, MIX_WIDTH, D_MODEL), lambda i: (layer, 0, 0, 0)),
                  _resident((None, D_MODEL, D_MODEL), lambda i: (layer, 0, 0))],
        out_specs=row(D_MODEL),
        compiler_params=_params(blocks, 1),
        name="merge",
    )(x, r, m, s, gates, wb, wo)


def _s5_disc_kernel(are_ref, aim_ref, ldt_ref, bre_ref, bim_ref, abr_ref, abi_ref, bbr_ref, bbi_ref):
    ar, ai = are_ref[...], aim_ref[...]
    dt = jnp.exp(ldt_ref[...])
    mag = jnp.exp(ar * dt)
    abr = mag * jnp.cos(ai * dt)
    abi = mag * jnp.sin(ai * dt)
    nr, ni = abr - 1.0, abi
    den = ar * ar + ai * ai
    cr = (nr * ar + ni * ai) / den
    ci = (ni * ar - nr * ai) / den
    br, bi = bre_ref[...], bim_ref[...]
    abr_ref[...] = abr
    abi_ref[...] = abi
    bbr_ref[...] = cr * br - ci * bi
    bbi_ref[...] = cr * bi + ci * br


def _s5_discretise(a_re, a_im, log_dt, b_re, b_im):
    rows = DEPTH * S5_N
    col = lambda a: a.reshape(rows, 1)
    ldt = jnp.broadcast_to(log_dt[:, :, None], (DEPTH, S5_GROUPS, S5_STATE))
    outs = pl.pallas_call(
        _s5_disc_kernel,
        out_shape=(jax.ShapeDtypeStruct((rows, 1), F32), jax.ShapeDtypeStruct((rows, 1), F32),
                   jax.ShapeDtypeStruct((rows, S5_GROUP), F32), jax.ShapeDtypeStruct((rows, S5_GROUP), F32)),
        name="s5_disc",
    )(col(a_re), col(a_im), col(ldt), b_re.reshape(rows, S5_GROUP), b_im.reshape(rows, S5_GROUP))
    abr, abi, bbr, bbi = outs
    a_row = jnp.concatenate([abr.reshape(DEPTH, 1, S5_N), abi.reshape(DEPTH, 1, S5_N)], axis=-1)
    return a_row, bbr.reshape(DEPTH, S5_GROUPS, S5_STATE, S5_GROUP), bbi.reshape(DEPTH, S5_GROUPS, S5_STATE, S5_GROUP)


def _block_diag_in(b):
    gh = S5_GROUPS // 2
    eye = jnp.eye(gh, dtype=b.dtype)
    m = jnp.einsum('lkgpc,gh->lkgchp', b.reshape(DEPTH, 2, gh, S5_STATE, S5_GROUP), eye)
    return m.reshape(DEPTH, 2, S5_HALF, S5_HALF_N).astype(BF16)


def _block_diag_out(c):
    gh = S5_GROUPS // 2
    eye = jnp.eye(gh, dtype=c.dtype)
    m = jnp.einsum('lkgcp,gh->lkhpgc', c.reshape(DEPTH, 2, gh, S5_GROUP, S5_STATE), eye)
    return m.reshape(DEPTH, 2, S5_HALF_N, S5_HALF).astype(BF16)


def _s5_project_in(u, xs_ref, bre_ref, bim_ref):
    ub = u.astype(BF16)
    for k in range(2):
        uk = ub[:, k * S5_HALF:(k + 1) * S5_HALF]
        xs_ref[:, k * S5_HALF_N:(k + 1) * S5_HALF_N] = _dot(uk, bre_ref[k])
        xs_ref[:, S5_N + k * S5_HALF_N:S5_N + (k + 1) * S5_HALF_N] = _dot(uk, bim_ref[k])


def _s5_project_out(xs_ref, u, cre_ref, cimn_ref, d_ref, wglu_ref):
    ys = []
    for k in range(2):
        xr = xs_ref[:, k * S5_HALF_N:(k + 1) * S5_HALF_N].astype(BF16)
        xi = xs_ref[:, S5_N + k * S5_HALF_N:S5_N + (k + 1) * S5_HALF_N].astype(BF16)
        ys.append(_dot(xr, cre_ref[k]) + _dot(xi, cimn_ref[k]))
    y = jnp.concatenate(ys, axis=1) + d_ref[...] * u
    y = jax.nn.gelu(y)
    return y * jax.nn.sigmoid(_dot(y.astype(BF16), wglu_ref[...]))


def _s5_scan_step(ab_ref, xs_ref, rows, xre, xim):
    are = ab_ref[:, :S5_N]
    aim = ab_ref[:, S5_N:]
    nre = are * xre - aim * xim + xs_ref[rows, :S5_N]
    nim = are * xim + aim * xre + xs_ref[rows, S5_N:]
    xs_ref[rows, :S5_N] = nre
    xs_ref[rows, S5_N:] = nim
    return nre, nim


def _s5_step_kernel(u_ref, x0_ref, a_ref, bre_ref, bim_ref, cre_ref, cimn_ref, d_ref, wglu_ref,
                    o_ref, xl_ref, xs_scr):
    u = u_ref[...]
    _s5_project_in(u, xs_scr, bre_ref, bim_ref)
    xre, xim = _s5_scan_step(a_ref, xs_scr, slice(None), x0_ref[:, :S5_N], x0_ref[:, S5_N:])
    xl_ref[:, :S5_N] = xre
    xl_ref[:, S5_N:] = xim
    o_ref[...] = _s5_project_out(xs_scr, u, cre_ref, cimn_ref, d_ref, wglu_ref)


def _s5_prompt_kernel(u_ref, un_ref, x0_ref, a_ref, bre_ref, bim_ref, cre_ref, cimn_ref, d_ref, wglu_ref,
                      o_ref, xl_ref, xs0, xs1, ut0, ut1, st_scr, ab_scr, p_in, p_out, *, nb, tl):
    g = pl.program_id(0)
    m = nb * tl

    tg = V7X_MXU_DIM // nb
    rows_g = nb * tg

    def to_token_major(ref, first):
        pm = p_in[...]
        groups = []
        for lo_t in range(first, first + tl, tg):
            us = ref[:, lo_t:lo_t + tg, :].reshape(rows_g, MIX_WIDTH)
            hi = us.astype(BF16)
            lo = (us - hi.astype(F32)).astype(BF16)
            groups.append(_dot(pm, hi) + _dot(pm, lo))
        return jnp.concatenate(groups, axis=0)

    def store_sequence_major(out, first):
        pm = p_out[...]
        for gq in range(tl // tg):
            piece = _dot(pm, out[gq * rows_g:(gq + 1) * rows_g].astype(BF16)).astype(o_ref.dtype)
            o_ref[:, first + gq * tg:first + (gq + 1) * tg, :] = piece.reshape(nb, tg, MIX_WIDTH)

    def scan(xs, xre, xim):
        for t in range(tl):
            xre, xim = _s5_scan_step(ab_scr, xs, slice(t * nb, (t + 1) * nb), xre, xim)
        return xre, xim

    @pl.when(g == 0)
    def _():
        st_scr[...] = x0_ref[...]
        ab_scr[...] = jnp.broadcast_to(a_ref[...], ab_scr.shape)
        shift = nb.bit_length() - 1
        ro = lax.broadcasted_iota(jnp.int32, (rows_g, rows_g), 0)
        ci = lax.broadcasted_iota(jnp.int32, (rows_g, rows_g), 1)
        seq_major = lambda r: (r & (nb - 1)) * tg + (r >> shift)
        p_in[...] = jnp.where(ci == seq_major(ro), 1.0, 0.0).astype(BF16)
        p_out[...] = jnp.where(ro == seq_major(ci), 1.0, 0.0).astype(BF16)
        u0 = to_token_major(u_ref, 0)
        ut0[...] = u0
        _s5_project_in(u0, xs0, bre_ref, bim_ref)

    u1 = to_token_major(u_ref, tl)
    ut1[...] = u1
    _s5_project_in(u1, xs1, bre_ref, bim_ref)
    xre, xim = scan(xs0, st_scr[:, :S5_N], st_scr[:, S5_N:])
    store_sequence_major(_s5_project_out(xs0, ut0[...], cre_ref, cimn_ref, d_ref, wglu_ref), 0)
    xre, xim = scan(xs1, xre, xim)
    st_scr[:, :S5_N] = xre
    st_scr[:, S5_N:] = xim
    un = to_token_major(un_ref, 0)
    ut0[...] = un
    _s5_project_in(un, xs0, bre_ref, bim_ref)
    store_sequence_major(_s5_project_out(xs1, u1, cre_ref, cimn_ref, d_ref, wglu_ref), tl)
    xl_ref[...] = st_scr[...]


def _s5_weight_specs(layer):
    lay3 = lambda i: (layer, 0, 0)
    lay4 = lambda i: (layer, 0, 0, 0)
    return [_resident((None, 1, 2 * S5_N), lay3),
            _resident((None, 2, S5_HALF, S5_HALF_N), lay4),
            _resident((None, 2, S5_HALF, S5_HALF_N), lay4),
            _resident((None, 2, S5_HALF_N, S5_HALF), lay4),
            _resident((None, 2, S5_HALF_N, S5_HALF), lay4),
            _resident((None, 1, MIX_WIDTH), lay3),
            _resident((None, MIX_WIDTH, MIX_WIDTH), lay3)]


S5_WEIGHT_BYTES = 4 * S5_HALF * S5_HALF_N * 2 * 2 + MIX_WIDTH * MIX_WIDTH * 2 + (2 * S5_N + MIX_WIDTH) * 4


def _s5_step(u, x0, s5w, layer):
    nb = u.shape[0]
    blocks = 4 * nb * MIX_WIDTH * 4 + 5 * nb * 2 * S5_N * 4 + S5_WEIGHT_BYTES + 6 * nb * MIX_WIDTH * 4
    whole = lambda w: pl.BlockSpec((nb, w), lambda i: (0, 0))
    return pl.pallas_call(
        _s5_step_kernel,
        out_shape=(jax.ShapeDtypeStruct(u.shape, F32), jax.ShapeDtypeStruct((nb, 2 * S5_N), F32)),
        grid=(1,),
        in_specs=[whole(MIX_WIDTH), pl.BlockSpec((None, nb, 2 * S5_N), lambda i: (layer, 0, 0))] + _s5_weight_specs(layer),
        out_specs=(whole(MIX_WIDTH), whole(2 * S5_N)),
        scratch_shapes=[pltpu.VMEM((nb, 2 * S5_N), F32)],
        compiler_params=_params(blocks, 1),
        name="s5_step",
    )(u, x0, *s5w)


def _s5_prompt(u, x0, s5w, layer):
    nb, seq, _ = u.shape
    tl = S5_TOKENS
    m = nb * tl
    steps = seq // (2 * tl)
    perm_rows = V7X_MXU_DIM
    assert nb & (nb - 1) == 0 and seq % (2 * tl) == 0 and m % perm_rows == 0
    blocks = (2 * 2 * m * MIX_WIDTH * 4 + 2 * m * MIX_WIDTH * 4 + 2 * 2 * m * MIX_WIDTH * 2 + 2 * m * 2 * S5_N * 4
              + 2 * m * MIX_WIDTH * 4 + 6 * nb * 2 * S5_N * 4 + S5_WEIGHT_BYTES + 2 * perm_rows * perm_rows * 2
              + m * 2 * S5_N * 2 + 6 * m * MIX_WIDTH * 4)
    pair = pl.BlockSpec((nb, 2 * tl, MIX_WIDTH), lambda i: (0, i, 0))
    nxt = pl.BlockSpec((nb, tl, MIX_WIDTH), lambda i: (0, jnp.minimum(2 * i + 2, 2 * steps - 1), 0))
    state = pl.BlockSpec((nb, 2 * S5_N), lambda i: (0, 0))
    return pl.pallas_call(
        functools.partial(_s5_prompt_kernel, nb=nb, tl=tl),
        out_shape=(jax.ShapeDtypeStruct(u.shape, BF16), jax.ShapeDtypeStruct((nb, 2 * S5_N), F32)),
        grid=(steps,),
        in_specs=[pair, nxt, _resident((nb, 2 * S5_N), lambda i: (0, 0))] + _s5_weight_specs(layer),
        out_specs=(pair, state),
        scratch_shapes=[pltpu.VMEM((m, 2 * S5_N), F32), pltpu.VMEM((m, 2 * S5_N), F32),
                        pltpu.VMEM((m, MIX_WIDTH), F32), pltpu.VMEM((m, MIX_WIDTH), F32),
                        pltpu.VMEM((nb, 2 * S5_N), F32), pltpu.VMEM((nb, 2 * S5_N), F32),
                        pltpu.VMEM((perm_rows, perm_rows), BF16), pltpu.VMEM((perm_rows, perm_rows), BF16)],
        compiler_params=_params(blocks, 1),
        name="s5",
    )(u, u, x0, *s5w)


def _ret_kernel(p_ref, cos_ref, sin_ref, o_ref, s_out_ref, s_scr, *, seq):
    c = CHUNK
    heads = range(HEADS)
    row = lax.broadcasted_iota(jnp.int32, (c, c), 0)
    col = lax.broadcasted_iota(jnp.int32, (c, c), 1)
    diff = (row - col).astype(F32)
    ivec = lax.broadcasted_iota(jnp.int32, (c, HEAD_DIM), 0).astype(F32)
    dmats = [jnp.where(diff >= 0, jnp.exp(lg * jnp.maximum(diff, 0.0)), 0.0) * QK_SCALE for lg in LOG_GAMMA]
    q_decs = [jnp.exp(lg * (ivec + 1.0)) for lg in LOG_GAMMA]
    k_decs = [jnp.exp(lg * (c - 1.0 - ivec)) * QK_SCALE for lg in LOG_GAMMA]
    s_scr[...] = jnp.zeros_like(s_scr)

    def chunk(j, carry):
        r = pl.ds(pl.multiple_of(j * c, c), c)
        cs = cos_ref[r, :]
        sn = sin_ref[r, :]
        sl = lambda part, h: p_ref[r, part * MIX_WIDTH + h * HEAD_DIM:part * MIX_WIDTH + (h + 1) * HEAD_DIM]
        q = [_rope(sl(0, h).astype(F32), cs, sn) for h in heads]
        k = [_rope(sl(1, h).astype(F32), cs, sn) for h in heads]
        v = [sl(2, h).astype(BF16) for h in heads]
        s = [s_scr[h] for h in heads]
        qb = [q[h].astype(BF16) for h in heads]
        kb = [k[h].astype(BF16) for h in heads]
        att = [_dot_nt(qb[h], kb[h]) for h in heads]
        inter = [_dot((q[h] * q_decs[h]).astype(BF16), s[h].astype(BF16)) for h in heads]
        upd = [_dot_tn((k[h] * k_decs[h]).astype(BF16), v[h]) for h in heads]
        attb = [(att[h] * dmats[h]).astype(BF16) for h in heads]
        o = [_dot(attb[h], v[h]) + inter[h] for h in heads]
        for h in heads:
            s_scr[h] = s[h] * math.exp(LOG_GAMMA[h] * c) + upd[h]
        on = [_head_norm(o[h]) for h in heads]
        for h in heads:
            g = sl(3, h).astype(F32)
            o_ref[r, h * HEAD_DIM:(h + 1) * HEAD_DIM] = (on[h] * (g * jax.nn.sigmoid(g))).astype(o_ref.dtype)
        return carry

    lax.fori_loop(0, seq // c, chunk, 0, unroll=2)
    s_out_ref[...] = s_scr[...]


def _retention_prompt(p, cos, sin, b, seq):
    half = 4 * MIX_WIDTH
    blocks = 2 * seq * half * 2 + 2 * seq * V7X_LANES * 4 + 2 * seq * MIX_WIDTH * 2 + 3 * HEADS * HEAD_DIM * HEAD_DIM * 4
    return pl.pallas_call(
        functools.partial(_ret_kernel, seq=seq),
        out_shape=(jax.ShapeDtypeStruct((b * seq, MIX_WIDTH), BF16),
                   jax.ShapeDtypeStruct((b, HEADS, HEAD_DIM, HEAD_DIM), F32)),
        grid=(b,),
        in_specs=[pl.BlockSpec((seq, half), lambda i: (i, 0)),
                  _resident((seq, HEAD_DIM), lambda i: (0, 0)),
                  _resident((seq, HEAD_DIM), lambda i: (0, 0))],
        out_specs=(pl.BlockSpec((seq, MIX_WIDTH), lambda i: (i, 0)),
                   pl.BlockSpec((None, HEADS, HEAD_DIM, HEAD_DIM), lambda i: (i, 0, 0, 0))),
        scratch_shapes=[pltpu.VMEM((HEADS, HEAD_DIM, HEAD_DIM), F32)],
        compiler_params=_params(blocks, 1),
        name="retention",
    )(p, cos, sin)


N_REP = 3 * HEADS


def _split_dot(x, w):
    hi = x.astype(BF16)
    lo = (x - hi.astype(F32)).astype(BF16)
    return _dot(hi, w) + _dot(lo, w)


def _split_dot_tn(x, w):
    hi = x.astype(BF16)
    lo = (x - hi.astype(F32)).astype(BF16)
    return _dot_tn(hi, w) + _dot_tn(lo, w)


def _mlstm_kernel(p_ref, gif_ref, bias_ref, o_ref, c_out_ref, n_out_ref, m_out_ref,
                  cn_scr, m_scr, rows_scr, rep_scr, *, seq):
    c = CHUNK
    nch = seq // c
    nrow = 8 * nch
    heads = range(HEADS)
    neg_inf = float("-inf")
    row = lax.broadcasted_iota(jnp.int32, (c, c), 0)
    col = lax.broadcasted_iota(jnp.int32, (c, c), 1)
    causal = row >= col
    triu = jnp.where(row <= col, 1.0, 0.0).astype(BF16)
    lane = lax.broadcasted_iota(jnp.int32, (c, V7X_LANES), 1)
    ones = jnp.ones((c, V7X_LANES), BF16)
    er = lax.broadcasted_iota(jnp.int32, (16, N_REP * V7X_LANES), 0)
    ec = lax.broadcasted_iota(jnp.int32, (16, N_REP * V7X_LANES), 1)
    expand = jnp.where(ec // V7X_LANES == er, 1.0, 0.0).astype(BF16)
    cn_scr[...] = jnp.zeros_like(cn_scr)
    m_scr[...] = jnp.zeros_like(m_scr)

    pieces = []
    for j in range(nch):
        fb = gif_ref[j * c:(j + 1) * c, :] + bias_ref[...]
        gl = jnp.where(lane < HEADS, fb, jax.nn.log_sigmoid(fb))
        pieces.append(gl.T[0:8, :])
    x = jnp.concatenate(pieces, axis=0)
    cum = _split_dot(x, triu)
    sub = lax.broadcasted_iota(jnp.int32, (nrow, c), 0) % 8
    tok = lax.broadcasted_iota(jnp.int32, (nrow, c), 1)
    a = x - pltpu.roll(cum, nrow - HEADS, 0)
    d = 1
    while d < c:
        a = jnp.maximum(a, jnp.where(tok >= d, pltpu.roll(a, d, 1), neg_inf))
        d *= 2
    rows_scr[0] = jnp.where(sub < HEADS, x, cum)
    rows_scr[1] = a

    def replicate(j):
        r8 = pl.ds(pl.multiple_of(j * 8, 8), 8)
        r16 = jnp.concatenate([rows_scr[0, r8, :], rows_scr[1, r8, :]], axis=0)
        return _split_dot_tn(r16, expand)

    rep_scr[0] = replicate(0)

    def chunk(j, carry):
        r = pl.ds(pl.multiple_of(j * c, c), c)
        r8 = pl.ds(pl.multiple_of(j * 8, 8), 8)
        slot = j % 2
        rep_next = replicate(jnp.minimum(j + 1, nch - 1))
        sl = lambda part, h: p_ref[r, part * MIX_WIDTH + h * HEAD_DIM:part * MIX_WIDTH + (h + 1) * HEAD_DIM]
        blk = lambda n, h: rep_scr[slot, :, (n * HEADS + h) * V7X_LANES:(n * HEADS + h + 1) * V7X_LANES]
        y8 = rows_scr[0, r8, :]
        q = [sl(0, h).astype(BF16) for h in heads]
        k = [sl(1, h).astype(BF16) for h in heads]
        v1 = [jnp.concatenate([sl(2, h).astype(BF16), ones], axis=1) for h in heads]
        cn = [cn_scr[h] for h in heads]
        m_prev = [m_scr[h] for h in heads]
        ic = [blk(0, h) for h in heads]
        bc = [blk(1, h) for h in heads]
        cmc = [blk(2, h) for h in heads]
        qk = [_dot_nt(q[h], k[h]) for h in heads]
        qcn = [_dot(q[h], cn[h].astype(BF16)) for h in heads]
        b_last = [bc[h][c - 1:c, :] for h in heads]
        wk_log = [b_last[h] - bc[h] + ic[h] for h in heads]
        m_new = [jnp.maximum(b_last[h] + m_prev[h], jnp.max(wk_log[h], axis=0, keepdims=True)) for h in heads]
        wk = [jnp.exp(wk_log[h] - m_new[h]) * QK_SCALE for h in heads]
        decay = [jnp.exp(b_last[h] + m_prev[h] - m_new[h]) for h in heads]
        kw = [(k[h].astype(F32) * wk[h]).astype(BF16) for h in heads]
        upd = [_dot_tn(kw[h], v1[h]) for h in heads]
        m_t = [bc[h] + jnp.maximum(m_prev[h], cmc[h]) for h in heads]
        dlog = [jnp.where(causal, bc[h] - y8[HEADS + h:HEADS + h + 1, :] + y8[h:h + 1, :], neg_inf) for h in heads]
        w_intra = [jnp.exp(dlog[h] - m_t[h]) * QK_SCALE for h in heads]
        w_inter = [jnp.exp(bc[h] + m_prev[h] - m_t[h]) for h in heads]
        s = [(qk[h] * w_intra[h]).astype(BF16) for h in heads]
        sv = [_dot(s[h], v1[h]) for h in heads]
        hh = []
        for h in heads:
            num = sv[h][:, :HEAD_DIM] + w_inter[h] * qcn[h][:, :HEAD_DIM]
            den = sv[h][:, HEAD_DIM:] + w_inter[h] * qcn[h][:, HEAD_DIM:]
            hh.append(num / jnp.maximum(jnp.abs(den), jnp.exp(-m_t[h])))
        for h in heads:
            cn_scr[h] = jnp.concatenate([decay[h], decay[h]], axis=1) * cn[h] + upd[h]
            m_scr[h] = m_new[h]
        on = [_head_norm(hh[h]) for h in heads]
        for h in heads:
            o_ref[r, h * HEAD_DIM:(h + 1) * HEAD_DIM] = (on[h] * jax.nn.sigmoid(sl(3, h).astype(F32))).astype(o_ref.dtype)
        rep_scr[1 - slot] = rep_next
        return carry

    lax.fori_loop(0, nch, chunk, 0, unroll=2)
    for h in heads:
        cnh = cn_scr[h]
        c_out_ref[h] = cnh[:, :HEAD_DIM]
        n_out_ref[h:h + 1, :] = cnh[:, HEAD_DIM:].T[0:1, :]
        m_out_ref[h:h + 1, :] = m_scr[h]


def _mlstm_prompt(p, gif, bias, layer, b, seq):
    half = 4 * MIX_WIDTH
    nch = seq // CHUNK
    blocks = (2 * seq * half * 2 + 2 * seq * N_GIF * 4 + 2 * seq * MIX_WIDTH * 2
              + 5 * HEADS * HEAD_DIM * 2 * HEAD_DIM * 4 + 4 * CHUNK * N_REP * V7X_LANES * 4)
    return pl.pallas_call(
        functools.partial(_mlstm_kernel, seq=seq),
        out_shape=(jax.ShapeDtypeStruct((b * seq, MIX_WIDTH), BF16),
                   jax.ShapeDtypeStruct((b, HEADS, HEAD_DIM, HEAD_DIM), F32),
                   jax.ShapeDtypeStruct((b, HEADS, HEAD_DIM), F32),
                   jax.ShapeDtypeStruct((b, HEADS, V7X_LANES), F32)),
        grid=(b,),
        in_specs=[pl.BlockSpec((seq, half), lambda i: (i, 1)),
                  pl.BlockSpec((seq, N_GIF), lambda i: (i, 0)),
                  _resident((None, 1, N_GIF), lambda i: (layer, 0, 0))],
        out_specs=(pl.BlockSpec((seq, MIX_WIDTH), lambda i: (i, 0)),
                   pl.BlockSpec((None, HEADS, HEAD_DIM, HEAD_DIM), lambda i: (i, 0, 0, 0)),
                   pl.BlockSpec((None, HEADS, HEAD_DIM), lambda i: (i, 0, 0)),
                   pl.BlockSpec((None, HEADS, V7X_LANES), lambda i: (i, 0, 0))),
        scratch_shapes=[pltpu.VMEM((HEADS, HEAD_DIM, 2 * HEAD_DIM), F32),
                        pltpu.VMEM((HEADS, 1, V7X_LANES), F32),
                        pltpu.VMEM((2, 8 * nch, CHUNK), F32),
                        pltpu.VMEM((2, CHUNK, N_REP * V7X_LANES), F32)],
        compiler_params=_params(blocks, 1),
        name="mlstm",
    )(p, gif, bias)


SEQ_BLOCK = 8


def _column_expander():
    er = lax.broadcasted_iota(jnp.int32, (SEQ_BLOCK, SEQ_BLOCK * V7X_LANES), 0)
    ec = lax.broadcasted_iota(jnp.int32, (SEQ_BLOCK, SEQ_BLOCK * V7X_LANES), 1)
    return jnp.where(ec // V7X_LANES == er, 1.0, 0.0).astype(BF16)


def _columns(x, expander):
    return _dot_tn(x.astype(BF16), expander)


def _ret_step_kernel(p_ref, cos_ref, sin_ref, s_ref, *rest, layer, first):
    o_ref, s_out_ref, row_scr = rest[-3:]
    s_out_ref = _own_layer(s_out_ref, layer, first)
    cs = cos_ref[...]
    sn = sin_ref[...]
    expander = _column_expander()
    for h in range(HEADS):
        lo = h * HEAD_DIM
        gamma = math.exp(LOG_GAMMA[h])
        q = _rope(p_ref[:, lo:lo + HEAD_DIM], cs, sn)
        k = _rope(p_ref[:, MIX_WIDTH + lo:MIX_WIDTH + lo + HEAD_DIM], cs, sn) * QK_SCALE
        v = p_ref[:, 2 * MIX_WIDTH + lo:2 * MIX_WIDTH + lo + HEAD_DIM]
        g = p_ref[:, 3 * MIX_WIDTH + lo:3 * MIX_WIDTH + lo + HEAD_DIM]
        qt = _columns(q * gamma, expander)
        kt = _columns(k, expander)
        for b in range(SEQ_BLOCK):
            s = s_ref[b, h]
            blk = slice(b * V7X_LANES, (b + 1) * V7X_LANES)
            row_scr[b:b + 1, :] = jnp.sum(qt[:, blk] * s, axis=0, keepdims=True)
            s_out_ref[b, h] = gamma * s + kt[:, blk] * v[b:b + 1, :]
        o = jnp.sum(q * k, axis=-1, keepdims=True) * v + row_scr[...]
        o_ref[:, lo:lo + HEAD_DIM] = _head_norm(o) * (g * jax.nn.sigmoid(g))


def _stacked_state(acc, layer):
    tail = (HEADS, HEAD_DIM, HEAD_DIM)
    if acc is None:
        return (), [], pl.BlockSpec((DEPTH, SEQ_BLOCK) + tail, lambda i: (0, i, 0, 0, 0))
    return (acc,), [pl.BlockSpec(memory_space=pl.ANY)], pl.BlockSpec((None, SEQ_BLOCK) + tail, lambda i: (layer, i, 0, 0, 0))


def _own_layer(state_out_ref, layer, first):
    if not first:
        return state_out_ref
    for other in range(DEPTH):
        if other != layer:
            state_out_ref[other] = jnp.zeros(state_out_ref.shape[1:], F32)
    return state_out_ref.at[layer]


def _retention_step(p, cos, sin, state, layer, acc):
    nseq = p.shape[0]
    half = 4 * MIX_WIDTH
    sblk = SEQ_BLOCK * HEADS * HEAD_DIM * HEAD_DIM * 4
    blocks = (2 + 2 * (DEPTH if acc is None else 1)) * sblk + 2 * SEQ_BLOCK * half * 4 + 2 * SEQ_BLOCK * MIX_WIDTH * 4
    state_blk = pl.BlockSpec((None, SEQ_BLOCK, HEADS, HEAD_DIM, HEAD_DIM), lambda i: (layer, i, 0, 0, 0))
    extra, extra_specs, out_state_blk = _stacked_state(acc, layer)
    return pl.pallas_call(
        functools.partial(_ret_step_kernel, layer=layer, first=acc is None),
        out_shape=(jax.ShapeDtypeStruct((nseq, MIX_WIDTH), F32),
                   jax.ShapeDtypeStruct(state.shape, F32)),
        grid=(nseq // SEQ_BLOCK,),
        in_specs=[pl.BlockSpec((SEQ_BLOCK, half), lambda i: (i, 0)),
                  _resident((1, HEAD_DIM), lambda i: (0, 0)),
                  _resident((1, HEAD_DIM), lambda i: (0, 0)),
                  state_blk] + extra_specs,
        out_specs=(pl.BlockSpec((SEQ_BLOCK, MIX_WIDTH), lambda i: (i, 0)), out_state_blk),
        input_output_aliases={4: 1} if extra else {},
        scratch_shapes=[pltpu.VMEM((SEQ_BLOCK, HEAD_DIM), F32)],
        compiler_params=_params(blocks, 1),
        name="retention_step",
    )(p, cos, sin, state, *extra)


def _mlstm_step_kernel(p_ref, gif_ref, bias_ref, c_ref, n_ref, m_ref, *rest, layer, first):
    o_ref, c_out_ref, n_out_ref, m_out_ref, row_scr = rest[-5:]
    c_out_ref = _own_layer(c_out_ref, layer, first)
    fb = gif_ref[...] + bias_ref[...]
    lf_all = jax.nn.log_sigmoid(fb)
    lane = lax.broadcasted_iota(jnp.int32, (SEQ_BLOCK, V7X_LANES), 1)
    m_all = m_ref[...]
    m_acc = jnp.zeros((SEQ_BLOCK, V7X_LANES), F32)
    expander = _column_expander()
    for h in range(HEADS):
        lo = h * HEAD_DIM
        q = p_ref[:, lo:lo + HEAD_DIM]
        k = p_ref[:, MIX_WIDTH + lo:MIX_WIDTH + lo + HEAD_DIM] * QK_SCALE
        v = p_ref[:, 2 * MIX_WIDTH + lo:2 * MIX_WIDTH + lo + HEAD_DIM]
        og = p_ref[:, 3 * MIX_WIDTH + lo:3 * MIX_WIDTH + lo + HEAD_DIM]
        ig = fb[:, h:h + 1]
        inter = lf_all[:, HEADS + h:HEADS + h + 1] + m_all[:, h:h + 1]
        m_t = jnp.maximum(inter, ig)
        w_intra = jnp.exp(ig - m_t)
        w_inter = jnp.exp(inter - m_t)
        nv = n_ref[:, lo:lo + HEAD_DIM]
        kw = k * w_intra
        qt = _columns(q, expander)
        kt = _columns(kw, expander)
        for b in range(SEQ_BLOCK):
            cm = c_ref[b, h]
            blk = slice(b * V7X_LANES, (b + 1) * V7X_LANES)
            row_scr[b:b + 1, :] = jnp.sum(qt[:, blk] * cm, axis=0, keepdims=True)
            c_out_ref[b, h] = w_inter[b:b + 1, :] * cm + kt[:, blk] * v[b:b + 1, :]
        s = jnp.sum(q * k, axis=-1, keepdims=True) * w_intra
        num = s * v + w_inter * row_scr[...]
        den = s + w_inter * jnp.sum(q * nv, axis=-1, keepdims=True)
        hh = num / jnp.maximum(jnp.abs(den), jnp.exp(-m_t))
        n_out_ref[:, lo:lo + HEAD_DIM] = w_inter * nv + kw
        m_acc = jnp.where(lane == h, m_t, m_acc)
        o_ref[:, lo:lo + HEAD_DIM] = _head_norm(hh) * jax.nn.sigmoid(og)
    m_out_ref[...] = m_acc


def _mlstm_step(p, gif, bias, c_state, n_state, m_state, layer, acc):
    nseq = p.shape[0]
    half = 4 * MIX_WIDTH
    sblk = SEQ_BLOCK * HEADS * HEAD_DIM * HEAD_DIM * 4
    blocks = (2 + 2 * (DEPTH if acc is None else 1)) * sblk + 2 * SEQ_BLOCK * half * 4 + 8 * SEQ_BLOCK * MIX_WIDTH * 4
    rows = lambda w: pl.BlockSpec((SEQ_BLOCK, w), lambda i: (i, 0))
    state_blk = pl.BlockSpec((None, SEQ_BLOCK, HEADS, HEAD_DIM, HEAD_DIM), lambda i: (layer, i, 0, 0, 0))
    extra, extra_specs, out_state_blk = _stacked_state(acc, layer)
    return pl.pallas_call(
        functools.partial(_mlstm_step_kernel, layer=layer, first=acc is None),
        out_shape=(jax.ShapeDtypeStruct((nseq, MIX_WIDTH), F32),
                   jax.ShapeDtypeStruct(c_state.shape, F32),
                   jax.ShapeDtypeStruct((nseq, MIX_WIDTH), F32),
                   jax.ShapeDtypeStruct((nseq, V7X_LANES), F32)),
        grid=(nseq // SEQ_BLOCK,),
        in_specs=[pl.BlockSpec((SEQ_BLOCK, half), lambda i: (i, 1)),
                  rows(N_GIF),
                  _resident((None, 1, N_GIF), lambda i: (layer, 0, 0)),
                  state_blk,
                  pl.BlockSpec((None, SEQ_BLOCK, MIX_WIDTH), lambda i: (layer, i, 0)),
                  pl.BlockSpec((None, SEQ_BLOCK, HEADS), lambda i: (layer, i, 0))] + extra_specs,
        out_specs=(rows(MIX_WIDTH), out_state_blk, rows(MIX_WIDTH), rows(V7X_LANES)),
        input_output_aliases={6: 1} if extra else {},
        scratch_shapes=[pltpu.VMEM((SEQ_BLOCK, HEAD_DIM), F32)],
        compiler_params=_params(blocks, 1),
        name="mlstm_step",
    )(p, gif, bias, c_state, n_state, m_state, *extra)


def _rope_tables(pos):
    inv = ROPE_THETA ** (-jnp.arange(0, HEAD_DIM, 2, dtype=F32) / HEAD_DIM)
    ang = pos.astype(F32)[:, None] * inv[None, :]
    cos, sin = jnp.cos(ang), jnp.sin(ang)
    return jnp.concatenate([cos, cos], axis=-1), jnp.concatenate([-sin, sin], axis=-1)


W_IN_BLOCKS = N_IN_PADDED // W_IN_ROWS


def _w_in_source_row(j):
    o = j * W_IN_ROWS
    off_gif = N_MAIN
    off_su = off_gif + 2 * HEADS
    off_gates = off_su + MIX_WIDTH
    return jnp.where(o < OFF_GATES, o,
                     jnp.where(o < OFF_SU, o - OFF_GATES + off_gates,
                               jnp.where(o < OFF_GIF, o - OFF_SU + off_su, off_gif)))


def _w_in_layout_kernel(w_hbm, o_ref, buf, sem):
    l, j = pl.program_id(0), pl.program_id(1)
    step = l * W_IN_BLOCKS + j
    total = pl.num_programs(0) * W_IN_BLOCKS

    def read(s):
        src = pl.multiple_of(_w_in_source_row(s % W_IN_BLOCKS), 8)
        return pltpu.make_async_copy(w_hbm.at[s // W_IN_BLOCKS, pl.ds(src, W_IN_ROWS), :], buf.at[s % 2], sem.at[s % 2])

    @pl.when(step == 0)
    def _():
        read(step).start()

    @pl.when(step + 1 < total)
    def _():
        read(step + 1).start()

    read(step).wait()
    x = buf[step % 2]
    row = lax.broadcasted_iota(jnp.int32, x.shape, 0)
    keep = jnp.logical_or(j < W_IN_BLOCKS - 1, row < 2 * HEADS)
    o_ref[...] = jnp.where(keep, x, 0.0).astype(BF16)


def _w_in_layout(w_in):
    depth = w_in.shape[0]
    w_t = jnp.swapaxes(w_in, 1, 2)
    blocks = 2 * W_IN_ROWS * D_MODEL * 4 + 2 * W_IN_ROWS * D_MODEL * 2 + 2 * W_IN_ROWS * D_MODEL * 4
    return pl.pallas_call(
        _w_in_layout_kernel,
        out_shape=jax.ShapeDtypeStruct((depth, N_IN_PADDED, D_MODEL), BF16),
        grid=(depth, W_IN_BLOCKS),
        in_specs=[pl.BlockSpec(memory_space=pl.ANY)],
        out_specs=pl.BlockSpec((None, W_IN_ROWS, D_MODEL), lambda l, j: (l, j, 0)),
        scratch_shapes=[pltpu.VMEM((2, W_IN_ROWS, D_MODEL), F32), pltpu.SemaphoreType.DMA((2,))],
        compiler_params=_params(blocks, 2),
        name="w_in_layout",
    )(w_t)


def _prepare_weights(w_ffn1_gu, w_ffn1_down, w_in, w_s5_glu, w_branch, w_out, w_ffn2_gu, w_ffn2_down):
    cast = lambda w: w.astype(BF16)
    return dict(gu1=cast(w_ffn1_gu), d1=cast(w_ffn1_down), w_in=_w_in_layout(w_in), glu=cast(w_s5_glu),
                branch=cast(w_branch), out=cast(w_out), gu2=cast(w_ffn2_gu), d2=cast(w_ffn2_down))


def _run_trunk(x, nseq, seq, states, wts, norms, s5w, bias, ln_final, prompt):
    t = nseq * seq
    tm = 512 if t % 512 == 0 else t
    act_dtype = BF16 if prompt else F32
    if prompt:
        cos, sin = _rope_tables(jnp.arange(seq, dtype=jnp.int32))
    else:
        cos, sin = _rope_tables(PAST_LEN + jnp.arange(seq, dtype=jnp.int32))
    if prompt:
        x0 = jnp.zeros((nseq, 2 * S5_N), F32)
    else:
        st_ret, st_mc, st_mn, st_mm, st_sre, st_sim = states
        st_mn = st_mn.reshape(DEPTH, nseq, MIX_WIDTH)
        x0 = jnp.concatenate([st_sre.reshape(DEPTH, nseq, S5_N), st_sim.reshape(DEPTH, nseq, S5_N)], axis=-1)
    rets, mcs, mns, mms, xls = [], [], [], [], []
    ret_acc = mc_acc = None
    for l in range(DEPTH):
        x = _ffn(x, norms["ffn1"], wts["gu1"], wts["d1"], ln_final, l, tm, final=False)
        main, gates, su, gif = _inproj(x, norms["mix"], wts["w_in"], l, tm, act_dtype)
        if prompt:
            r_out, new_ret = _retention_prompt(main, cos, sin, nseq, seq)
            m_out, new_mc, new_mn, new_mm = _mlstm_prompt(main, gif, bias, l, nseq, seq)
            s_out, x_last = _s5_prompt(su.reshape(nseq, seq, MIX_WIDTH), x0, s5w, l)
            rets.append(new_ret)
            mcs.append(new_mc)
        else:
            r_out, ret_acc = _retention_step(main, cos, sin, st_ret, l, ret_acc)
            m_out, mc_acc, new_mn, new_mm = _mlstm_step(main, gif, bias, st_mc, st_mn, st_mm, l, mc_acc)
            s_out, x_last = _s5_step(su, x0, s5w, l)
        mns.append(new_mn)
        mms.append(new_mm)
        xls.append(x_last)
        x = _merge(x, r_out, m_out, s_out, gates, wts["branch"], wts["out"], l, tm)
        x = _ffn(x, norms["ffn2"], wts["gu2"], wts["d2"], ln_final, l, tm, final=(l == DEPTH - 1))
    new_ret, new_mc = (jnp.stack(rets), jnp.stack(mcs)) if prompt else (ret_acc, mc_acc)
    new_mn = jnp.stack(mns).reshape(DEPTH, nseq, HEADS, HEAD_DIM)
    new_mm = jnp.stack(mms)[:, :, :, 0] if prompt else jnp.stack(mms)[:, :, :HEADS]
    xl = jnp.stack(xls)
    new_sre = xl[:, :, :S5_N].reshape(DEPTH, nseq, S5_GROUPS, S5_STATE)
    new_sim = xl[:, :, S5_N:].reshape(DEPTH, nseq, S5_GROUPS, S5_STATE)
    return x, (new_ret, new_mc, new_mn, new_mm, new_sre, new_sim)


def kernel(x_prompt, x_sample, state_ret, state_mlstm_c, state_mlstm_n, state_mlstm_m, state_s5_re, state_s5_im, ln_ffn1, w_ffn1_gu, w_ffn1_down, ln_mix, w_in, b_gates, s5_a_re, s5_a_im, s5_b_re, s5_b_im, s5_c_re, s5_c_im, s5_d, s5_log_dt, w_s5_glu, w_branch, w_out, ln_ffn2, w_ffn2_gu, w_ffn2_down, ln_final):
    wts = _prepare_weights(w_ffn1_gu, w_ffn1_down, w_in, w_s5_glu, w_branch, w_out, w_ffn2_gu, w_ffn2_down)
    norms = dict(ffn1=ln_ffn1.reshape(DEPTH, 1, D_MODEL), mix=ln_mix.reshape(DEPTH, 1, D_MODEL),
                 ffn2=ln_ffn2.reshape(DEPTH, 1, D_MODEL))
    lnf = ln_final.reshape(1, D_MODEL)
    a_row, bbr, bbi = _s5_discretise(s5_a_re, s5_a_im, s5_log_dt, s5_b_re, s5_b_im)
    s5w = (a_row, _block_diag_in(bbr), _block_diag_in(bbi), _block_diag_out(s5_c_re), _block_diag_out(-s5_c_im),
           s5_d.reshape(DEPTH, 1, MIX_WIDTH), wts["glu"])
    bias = jnp.pad(b_gates, ((0, 0), (0, N_GIF - 2 * HEADS))).reshape(DEPTH, 1, N_GIF)

    pb, pl_len, _ = x_prompt.shape
    sb, sl, _ = x_sample.shape
    y_p, st_p = _run_trunk(x_prompt.reshape(pb * pl_len, D_MODEL), pb, pl_len, None, wts, norms, s5w, bias, lnf, True)
    sample_states = (state_ret, state_mlstm_c, state_mlstm_n, state_mlstm_m, state_s5_re, state_s5_im)
    y_s, st_s = _run_trunk(x_sample.reshape(sb * sl, D_MODEL), sb, sl, sample_states, wts, norms, s5w, bias, lnf, False)
    return (y_p.reshape(pb, pl_len, D_MODEL), y_s.reshape(sb, sl, D_MODEL)) + st_p + st_s
```

```python
import functools
import math

import jax
import jax.numpy as jnp
from jax import lax
from jax.experimental import pallas as pl
from jax.experimental.pallas import tpu as pltpu

F32 = jnp.float32
BF16 = jnp.bfloat16

D_MODEL = 1024
DEPTH = 4
MIX_WIDTH = D_MODEL // 2
N_BRANCH = 3
HEADS = 4
HEAD_DIM = MIX_WIDTH // HEADS
S5_GROUP = 16
S5_GROUPS = MIX_WIDTH // S5_GROUP
S5_STATE = 64
S5_N = S5_GROUPS * S5_STATE
D_FF = 2816
CHUNK = 128
PAST_LEN = 16384
ROPE_THETA = 10000.0
EPS = 1e-6

N_MAIN = 8 * MIX_WIDTH
N_GATES = N_BRANCH * D_MODEL
N_GIF = 128
W_IN_ROWS = 512
N_IN_PADDED = N_MAIN + N_GATES + MIX_WIDTH + W_IN_ROWS
OFF_GATES = N_MAIN
OFF_SU = N_MAIN + N_GATES
OFF_GIF = OFF_SU + MIX_WIDTH

V7X_LANES = 128
V7X_MXU_DIM = 256
V7X_VMEM_BYTES = 64 * 1024 * 1024

FF_CHUNK = V7X_MXU_DIM
PROJ_CHUNK = 2 * V7X_MXU_DIM
S5_HALF = MIX_WIDTH // 2
S5_HALF_N = S5_N // 2
S5_TOKENS = 64
LOG_GAMMA = tuple(math.log1p(-(2.0 ** (-5.0 - h))) for h in range(HEADS))
QK_SCALE = HEAD_DIM ** -0.5


def _vmem_limit(block_bytes):
    want = int(block_bytes * 1.25) + (8 << 20)
    return min(want, V7X_VMEM_BYTES - (6 << 20))


def _params(block_bytes, n_grid):
    return pltpu.CompilerParams(
        dimension_semantics=("arbitrary",) * n_grid,
        vmem_limit_bytes=_vmem_limit(block_bytes))


def _resident(shape, index_map):
    return pl.BlockSpec(shape, index_map, pipeline_mode=pl.Buffered(1))


def _dot(a, b):
    return jnp.dot(a, b, preferred_element_type=F32)


def _dot_nt(a, b):
    return lax.dot_general(a, b, (((1,), (1,)), ((), ())), preferred_element_type=F32)


def _dot_tn(a, b):
    return lax.dot_general(a, b, (((0,), (0,)), ((), ())), preferred_element_type=F32)


def _rms(x, g):
    return x * lax.rsqrt(jnp.mean(x * x, axis=-1, keepdims=True) + EPS) * g


def _head_norm(x):
    mu = jnp.mean(x, axis=-1, keepdims=True)
    xc = x - mu
    var = jnp.mean(xc * xc, axis=-1, keepdims=True)
    return xc * lax.rsqrt(var + EPS)


def _rope(x, cos, sin_signed):
    return x * cos + pltpu.roll(x, HEAD_DIM // 2, 1) * sin_signed


def _ffn_tile(x, ln_ref, wgu_ref, wd_ref, lnf_ref, o_ref, acc_ref, final):
    xn = _rms(x, ln_ref[...]).astype(BF16)
    for c in range(D_FF // FF_CHUNK):
        lo = c * FF_CHUNK
        g = _dot(xn, wgu_ref[:, lo:lo + FF_CHUNK])
        u = _dot(xn, wgu_ref[:, D_FF + lo:D_FF + lo + FF_CHUNK])
        a = (g * jax.nn.sigmoid(g) * u).astype(BF16)
        d = _dot(a, wd_ref[lo:lo + FF_CHUNK, :])
        if c == 0:
            acc_ref[...] = d
        else:
            acc_ref[...] += d
    y = x + 0.5 * acc_ref[...]
    if final:
        y = _rms(y, lnf_ref[...])
    o_ref[...] = y


def _ffn_kernel(x_ref, ln_ref, wgu_ref, wd_ref, lnf_ref, o_ref, acc_ref, *, final):
    _ffn_tile(x_ref[...], ln_ref, wgu_ref, wd_ref, lnf_ref, o_ref, acc_ref, final)


def _ffn(x, ln, wgu, wd, lnf, layer, tm, final):
    t = x.shape[0]
    blocks = (4 * tm * D_MODEL * 4 + tm * D_MODEL * 4 + D_MODEL * 2 * D_FF * 2 + D_FF * D_MODEL * 2
              + 4 * tm * FF_CHUNK * 4)
    return pl.pallas_call(
        functools.partial(_ffn_kernel, final=final),
        out_shape=jax.ShapeDtypeStruct((t, D_MODEL), F32),
        grid=(t // tm,),
        in_specs=[
            pl.BlockSpec((tm, D_MODEL), lambda i: (i, 0)),
            _resident((None, 1, D_MODEL), lambda i: (layer, 0, 0)),
            _resident((None, D_MODEL, 2 * D_FF), lambda i: (layer, 0, 0)),
            _resident((None, D_FF, D_MODEL), lambda i: (layer, 0, 0)),
            _resident((1, D_MODEL), lambda i: (0, 0)),
        ],
        out_specs=pl.BlockSpec((tm, D_MODEL), lambda i: (i, 0)),
        scratch_shapes=[pltpu.VMEM((tm, D_MODEL), F32)],
        compiler_params=_params(blocks, 1),
        name="ffn",
    )(x, ln, wgu, wd, lnf)


def _inproj_kernel(x_ref, ln_ref, w_ref, main_ref, gates_ref, su_ref, gif_ref):
    xn = _rms(x_ref[...], ln_ref[...]).astype(BF16)
    for c in range(N_MAIN // PROJ_CHUNK):
        lo = c * PROJ_CHUNK
        main_ref[:, lo:lo + PROJ_CHUNK] = _dot_nt(xn, w_ref[lo:lo + PROJ_CHUNK, :]).astype(main_ref.dtype)
    for c in range(N_GATES // PROJ_CHUNK):
        lo = c * PROJ_CHUNK
        gates_ref[:, lo:lo + PROJ_CHUNK] = _dot_nt(
            xn, w_ref[OFF_GATES + lo:OFF_GATES + lo + PROJ_CHUNK, :]).astype(gates_ref.dtype)
    su_ref[...] = _dot_nt(xn, w_ref[OFF_SU:OFF_SU + MIX_WIDTH, :])
    gif_ref[...] = _dot_nt(xn, w_ref[OFF_GIF:OFF_GIF + N_GIF, :])


def _inproj(x, ln, w, layer, tm, act_dtype):
    t = x.shape[0]
    ab = jnp.dtype(act_dtype).itemsize
    blocks = (2 * tm * D_MODEL * 4 + D_MODEL * N_IN_PADDED * 2
              + 2 * tm * (N_MAIN + N_GATES) * ab + 2 * tm * (MIX_WIDTH + N_GIF) * 4
              + 4 * tm * PROJ_CHUNK * 4)
    return pl.pallas_call(
        _inproj_kernel,
        out_shape=(jax.ShapeDtypeStruct((t, N_MAIN), act_dtype),
                   jax.ShapeDtypeStruct((t, N_GATES), act_dtype),
                   jax.ShapeDtypeStruct((t, MIX_WIDTH), F32),
                   jax.ShapeDtypeStruct((t, N_GIF), F32)),
        grid=(t // tm,),
        in_specs=[
            pl.BlockSpec((tm, D_MODEL), lambda i: (i, 0)),
            _resident((None, 1, D_MODEL), lambda i: (layer, 0, 0)),
            _resident((None, N_IN_PADDED, D_MODEL), lambda i: (layer, 0, 0)),
        ],
        out_specs=(pl.BlockSpec((tm, N_MAIN), lambda i: (i, 0)),
                   pl.BlockSpec((tm, N_GATES), lambda i: (i, 0)),
                   pl.BlockSpec((tm, MIX_WIDTH), lambda i: (i, 0)),
                   pl.BlockSpec((tm, N_GIF), lambda i: (i, 0))),
        compiler_params=_params(blocks, 1),
        name="inproj",
    )(x, ln, w)


def _merge_ffn_kernel(x_ref, r_ref, m_ref, s_ref, g_ref, wb_ref, wo_ref, ln_ref, wgu_ref, wd_ref, lnf_ref,
                      o_ref, acc_ref, *, final):
    merged = None
    for n, b_ref in enumerate((r_ref, m_ref, s_ref)):
        up = _dot(b_ref[...].astype(BF16), wb_ref[n])
        gate = jax.nn.sigmoid(g_ref[:, n * D_MODEL:(n + 1) * D_MODEL].astype(F32))
        merged = gate * up if merged is None else merged + gate * up
    x1 = x_ref[...] + _dot(merged.astype(BF16), wo_ref[...])
    _ffn_tile(x1, ln_ref, wgu_ref, wd_ref, lnf_ref, o_ref, acc_ref, final)


def _merge_ffn(x, r, m, s, gates, wb, wo, ln, wgu, wd, lnf, layer, tm, final):
    t = x.shape[0]
    blocks = (4 * tm * D_MODEL * 4 + 6 * tm * MIX_WIDTH * 4 + 2 * tm * N_GATES * 4
              + N_BRANCH * MIX_WIDTH * D_MODEL * 2 + D_MODEL * D_MODEL * 2 + 4 * tm * D_MODEL * 4
              + tm * D_MODEL * 4 + D_MODEL * 2 * D_FF * 2 + D_FF * D_MODEL * 2 + 4 * tm * FF_CHUNK * 4)
    row = lambda w: pl.BlockSpec((tm, w), lambda i: (i, 0))
    if s.ndim == 3:
        assert s.shape[1] % tm == 0
        per_seq = s.shape[1] // tm
        s_spec = pl.BlockSpec((None, tm, MIX_WIDTH), lambda i: (i // per_seq, i % per_seq, 0))
    else:
        s_spec = row(MIX_WIDTH)
    return pl.pallas_call(
        functools.partial(_merge_ffn_kernel, final=final),
        out_shape=jax.ShapeDtypeStruct((t, D_MODEL), F32),
        grid=(t // tm,),
        in_specs=[row(D_MODEL), row(MIX_WIDTH), row(MIX_WIDTH), s_spec, row(N_GATES),
                  _resident((None, N_BRANCH, MIX_WIDTH, D_MODEL), lambda i: (layer, 0, 0, 0)),
                  _resident((None, D_MODEL, D_MODEL), lambda i: (layer, 0, 0)),
                  _resident((None, 1, D_MODEL), lambda i: (layer, 0, 0)),
                  _resident((None, D_MODEL, 2 * D_FF), lambda i: (layer, 0, 0)),
                  _resident((None, D_FF, D_MODEL), lambda i: (layer, 0, 0)),
                  _resident((1, D_MODEL), lambda i: (0, 0))],
        out_specs=row(D_MODEL),
        scratch_shapes=[pltpu.VMEM((tm, D_MODEL), F32)],
        compiler_params=_params(blocks, 1),
        name="merge_ffn",
    )(x, r, m, s, gates, wb, wo, ln, wgu, wd, lnf)


def _s5_disc_kernel(are_ref, aim_ref, ldt_ref, bre_ref, bim_ref, abr_ref, abi_ref, bbr_ref, bbi_ref):
    ar, ai = are_ref[...], aim_ref[...]
    dt = jnp.exp(ldt_ref[...])
    mag = jnp.exp(ar * dt)
    abr = mag * jnp.cos(ai * dt)
    abi = mag * jnp.sin(ai * dt)
    nr, ni = abr - 1.0, abi
    den = ar * ar + ai * ai
    cr = (nr * ar + ni * ai) / den
    ci = (ni * ar - nr * ai) / den
    br, bi = bre_ref[...], bim_ref[...]
    abr_ref[...] = abr
    abi_ref[...] = abi
    bbr_ref[...] = cr * br - ci * bi
    bbi_ref[...] = cr * bi + ci * br


def _s5_discretise(a_re, a_im, log_dt, b_re, b_im):
    rows = DEPTH * S5_N
    col = lambda a: a.reshape(rows, 1)
    ldt = jnp.broadcast_to(log_dt[:, :, None], (DEPTH, S5_GROUPS, S5_STATE))
    outs = pl.pallas_call(
        _s5_disc_kernel,
        out_shape=(jax.ShapeDtypeStruct((rows, 1), F32), jax.ShapeDtypeStruct((rows, 1), F32),
                   jax.ShapeDtypeStruct((rows, S5_GROUP), F32), jax.ShapeDtypeStruct((rows, S5_GROUP), F32)),
        name="s5_disc",
    )(col(a_re), col(a_im), col(ldt), b_re.reshape(rows, S5_GROUP), b_im.reshape(rows, S5_GROUP))
    abr, abi, bbr, bbi = outs
    a_row = jnp.concatenate([abr.reshape(DEPTH, 1, S5_N), abi.reshape(DEPTH, 1, S5_N)], axis=-1)
    return a_row, bbr.reshape(DEPTH, S5_GROUPS, S5_STATE, S5_GROUP), bbi.reshape(DEPTH, S5_GROUPS, S5_STATE, S5_GROUP)


def _block_diag_in(b):
    gh = S5_GROUPS // 2
    eye = jnp.eye(gh, dtype=b.dtype)
    m = jnp.einsum('lkgpc,gh->lkgchp', b.reshape(DEPTH, 2, gh, S5_STATE, S5_GROUP), eye)
    return m.reshape(DEPTH, 2, S5_HALF, S5_HALF_N).astype(BF16)


def _block_diag_out(c):
    gh = S5_GROUPS // 2
    eye = jnp.eye(gh, dtype=c.dtype)
    m = jnp.einsum('lkgcp,gh->lkhpgc', c.reshape(DEPTH, 2, gh, S5_GROUP, S5_STATE), eye)
    return m.reshape(DEPTH, 2, S5_HALF_N, S5_HALF).astype(BF16)


def _s5_project_in(u, xs_ref, bre_ref, bim_ref):
    ub = u.astype(BF16)
    for k in range(2):
        uk = ub[:, k * S5_HALF:(k + 1) * S5_HALF]
        xs_ref[:, k * S5_HALF_N:(k + 1) * S5_HALF_N] = _dot(uk, bre_ref[k])
        xs_ref[:, S5_N + k * S5_HALF_N:S5_N + (k + 1) * S5_HALF_N] = _dot(uk, bim_ref[k])


def _s5_project_out(xs_ref, u, cre_ref, cimn_ref, d_ref, wglu_ref):
    ys = []
    for k in range(2):
        xr = xs_ref[:, k * S5_HALF_N:(k + 1) * S5_HALF_N].astype(BF16)
        xi = xs_ref[:, S5_N + k * S5_HALF_N:S5_N + (k + 1) * S5_HALF_N].astype(BF16)
        ys.append(_dot(xr, cre_ref[k]) + _dot(xi, cimn_ref[k]))
    y = jnp.concatenate(ys, axis=1) + d_ref[...] * u
    y = jax.nn.gelu(y)
    return y * jax.nn.sigmoid(_dot(y.astype(BF16), wglu_ref[...]))


def _s5_scan_step(ab_ref, xs_ref, rows, xre, xim):
    are = ab_ref[:, :S5_N]
    aim = ab_ref[:, S5_N:]
    nre = are * xre - aim * xim + xs_ref[rows, :S5_N]
    nim = are * xim + aim * xre + xs_ref[rows, S5_N:]
    xs_ref[rows, :S5_N] = nre
    xs_ref[rows, S5_N:] = nim
    return nre, nim


def _s5_step_kernel(u_ref, x0_ref, a_ref, bre_ref, bim_ref, cre_ref, cimn_ref, d_ref, wglu_ref,
                    o_ref, xl_ref, xs_scr):
    u = u_ref[...]
    _s5_project_in(u, xs_scr, bre_ref, bim_ref)
    xre, xim = _s5_scan_step(a_ref, xs_scr, slice(None), x0_ref[:, :S5_N], x0_ref[:, S5_N:])
    xl_ref[:, :S5_N] = xre
    xl_ref[:, S5_N:] = xim
    o_ref[...] = _s5_project_out(xs_scr, u, cre_ref, cimn_ref, d_ref, wglu_ref)


def _s5_prompt_kernel(u_ref, un_ref, x0_ref, a_ref, bre_ref, bim_ref, cre_ref, cimn_ref, d_ref, wglu_ref,
                      o_ref, xl_ref, xs0, xs1, ut0, ut1, st_scr, ab_scr, p_in, p_out, *, nb, tl):
    g = pl.program_id(0)
    m = nb * tl

    tg = V7X_MXU_DIM // nb
    rows_g = nb * tg

    def to_token_major(ref, first):
        pm = p_in[...]
        groups = []
        for lo_t in range(first, first + tl, tg):
            us = ref[:, lo_t:lo_t + tg, :].reshape(rows_g, MIX_WIDTH)
            hi = us.astype(BF16)
            lo = (us - hi.astype(F32)).astype(BF16)
            groups.append(_dot(pm, hi) + _dot(pm, lo))
        return jnp.concatenate(groups, axis=0)

    def store_sequence_major(out, first):
        pm = p_out[...]
        for gq in range(tl // tg):
            piece = _dot(pm, out[gq * rows_g:(gq + 1) * rows_g].astype(BF16)).astype(o_ref.dtype)
            o_ref[:, first + gq * tg:first + (gq + 1) * tg, :] = piece.reshape(nb, tg, MIX_WIDTH)

    def scan(xs, xre, xim):
        for t in range(tl):
            xre, xim = _s5_scan_step(ab_scr, xs, slice(t * nb, (t + 1) * nb), xre, xim)
        return xre, xim

    @pl.when(g == 0)
    def _():
        st_scr[...] = x0_ref[...]
        ab_scr[...] = jnp.broadcast_to(a_ref[...], ab_scr.shape)
        shift = nb.bit_length() - 1
        ro = lax.broadcasted_iota(jnp.int32, (rows_g, rows_g), 0)
        ci = lax.broadcasted_iota(jnp.int32, (rows_g, rows_g), 1)
        seq_major = lambda r: (r & (nb - 1)) * tg + (r >> shift)
        p_in[...] = jnp.where(ci == seq_major(ro), 1.0, 0.0).astype(BF16)
        p_out[...] = jnp.where(ro == seq_major(ci), 1.0, 0.0).astype(BF16)
        u0 = to_token_major(u_ref, 0)
        ut0[...] = u0
        _s5_project_in(u0, xs0, bre_ref, bim_ref)

    u1 = to_token_major(u_ref, tl)
    ut1[...] = u1
    _s5_project_in(u1, xs1, bre_ref, bim_ref)
    xre, xim = scan(xs0, st_scr[:, :S5_N], st_scr[:, S5_N:])
    store_sequence_major(_s5_project_out(xs0, ut0[...], cre_ref, cimn_ref, d_ref, wglu_ref), 0)
    xre, xim = scan(xs1, xre, xim)
    st_scr[:, :S5_N] = xre
    st_scr[:, S5_N:] = xim
    un = to_token_major(un_ref, 0)
    ut0[...] = un
    _s5_project_in(un, xs0, bre_ref, bim_ref)
    store_sequence_major(_s5_project_out(xs1, u1, cre_ref, cimn_ref, d_ref, wglu_ref), tl)
    xl_ref[...] = st_scr[...]


def _s5_weight_specs(layer):
    lay3 = lambda i: (layer, 0, 0)
    lay4 = lambda i: (layer, 0, 0, 0)
    return [_resident((None, 1, 2 * S5_N), lay3),
            _resident((None, 2, S5_HALF, S5_HALF_N), lay4),
            _resident((None, 2, S5_HALF, S5_HALF_N), lay4),
            _resident((None, 2, S5_HALF_N, S5_HALF), lay4),
            _resident((None, 2, S5_HALF_N, S5_HALF), lay4),
            _resident((None, 1, MIX_WIDTH), lay3),
            _resident((None, MIX_WIDTH, MIX_WIDTH), lay3)]


S5_WEIGHT_BYTES = 4 * S5_HALF * S5_HALF_N * 2 * 2 + MIX_WIDTH * MIX_WIDTH * 2 + (2 * S5_N + MIX_WIDTH) * 4


def _s5_step(u, x0, s5w, layer):
    nb = u.shape[0]
    blocks = 4 * nb * MIX_WIDTH * 4 + 5 * nb * 2 * S5_N * 4 + S5_WEIGHT_BYTES + 6 * nb * MIX_WIDTH * 4
    whole = lambda w: pl.BlockSpec((nb, w), lambda i: (0, 0))
    return pl.pallas_call(
        _s5_step_kernel,
        out_shape=(jax.ShapeDtypeStruct(u.shape, F32), jax.ShapeDtypeStruct((nb, 2 * S5_N), F32)),
        grid=(1,),
        in_specs=[whole(MIX_WIDTH), pl.BlockSpec((None, nb, 2 * S5_N), lambda i: (layer, 0, 0))] + _s5_weight_specs(layer),
        out_specs=(whole(MIX_WIDTH), whole(2 * S5_N)),
        scratch_shapes=[pltpu.VMEM((nb, 2 * S5_N), F32)],
        compiler_params=_params(blocks, 1),
        name="s5_step",
    )(u, x0, *s5w)


def _s5_prompt(u, x0, s5w, layer):
    nb, seq, _ = u.shape
    tl = S5_TOKENS
    m = nb * tl
    steps = seq // (2 * tl)
    perm_rows = V7X_MXU_DIM
    assert nb & (nb - 1) == 0 and seq % (2 * tl) == 0 and m % perm_rows == 0
    blocks = (2 * 2 * m * MIX_WIDTH * 4 + 2 * m * MIX_WIDTH * 4 + 2 * 2 * m * MIX_WIDTH * 2 + 2 * m * 2 * S5_N * 4
              + 2 * m * MIX_WIDTH * 4 + 6 * nb * 2 * S5_N * 4 + S5_WEIGHT_BYTES + 2 * perm_rows * perm_rows * 2
              + m * 2 * S5_N * 2 + 6 * m * MIX_WIDTH * 4)
    pair = pl.BlockSpec((nb, 2 * tl, MIX_WIDTH), lambda i: (0, i, 0))
    nxt = pl.BlockSpec((nb, tl, MIX_WIDTH), lambda i: (0, jnp.minimum(2 * i + 2, 2 * steps - 1), 0))
    state = pl.BlockSpec((nb, 2 * S5_N), lambda i: (0, 0))
    return pl.pallas_call(
        functools.partial(_s5_prompt_kernel, nb=nb, tl=tl),
        out_shape=(jax.ShapeDtypeStruct(u.shape, BF16), jax.ShapeDtypeStruct((nb, 2 * S5_N), F32)),
        grid=(steps,),
        in_specs=[pair, nxt, _resident((nb, 2 * S5_N), lambda i: (0, 0))] + _s5_weight_specs(layer),
        out_specs=(pair, state),
        scratch_shapes=[pltpu.VMEM((m, 2 * S5_N), F32), pltpu.VMEM((m, 2 * S5_N), F32),
                        pltpu.VMEM((m, MIX_WIDTH), F32), pltpu.VMEM((m, MIX_WIDTH), F32),
                        pltpu.VMEM((nb, 2 * S5_N), F32), pltpu.VMEM((nb, 2 * S5_N), F32),
                        pltpu.VMEM((perm_rows, perm_rows), BF16), pltpu.VMEM((perm_rows, perm_rows), BF16)],
        compiler_params=_params(blocks, 1),
        name="s5",
    )(u, u, x0, *s5w)


def _ret_kernel(p_ref, cos_ref, sin_ref, o_ref, s_out_ref, s_scr, *, seq):
    c = CHUNK
    heads = range(HEADS)
    row = lax.broadcasted_iota(jnp.int32, (c, c), 0)
    col = lax.broadcasted_iota(jnp.int32, (c, c), 1)
    diff = (row - col).astype(F32)
    ivec = lax.broadcasted_iota(jnp.int32, (c, HEAD_DIM), 0).astype(F32)
    dmats = [jnp.where(diff >= 0, jnp.exp(lg * jnp.maximum(diff, 0.0)), 0.0) * QK_SCALE for lg in LOG_GAMMA]
    q_decs = [jnp.exp(lg * (ivec + 1.0)) for lg in LOG_GAMMA]
    k_decs = [jnp.exp(lg * (c - 1.0 - ivec)) * QK_SCALE for lg in LOG_GAMMA]
    s_scr[...] = jnp.zeros_like(s_scr)

    def chunk(j, carry):
        r = pl.ds(pl.multiple_of(j * c, c), c)
        cs = cos_ref[r, :]
        sn = sin_ref[r, :]
        sl = lambda part, h: p_ref[r, part * MIX_WIDTH + h * HEAD_DIM:part * MIX_WIDTH + (h + 1) * HEAD_DIM]
        q = [_rope(sl(0, h).astype(F32), cs, sn) for h in heads]
        k = [_rope(sl(1, h).astype(F32), cs, sn) for h in heads]
        v = [sl(2, h).astype(BF16) for h in heads]
        s = [s_scr[h] for h in heads]
        qb = [q[h].astype(BF16) for h in heads]
        kb = [k[h].astype(BF16) for h in heads]
        att = [_dot_nt(qb[h], kb[h]) for h in heads]
        inter = [_dot((q[h] * q_decs[h]).astype(BF16), s[h].astype(BF16)) for h in heads]
        upd = [_dot_tn((k[h] * k_decs[h]).astype(BF16), v[h]) for h in heads]
        attb = [(att[h] * dmats[h]).astype(BF16) for h in heads]
        o = [_dot(attb[h], v[h]) + inter[h] for h in heads]
        for h in heads:
            s_scr[h] = s[h] * math.exp(LOG_GAMMA[h] * c) + upd[h]
        on = [_head_norm(o[h]) for h in heads]
        for h in heads:
            g = sl(3, h).astype(F32)
            o_ref[r, h * HEAD_DIM:(h + 1) * HEAD_DIM] = (on[h] * (g * jax.nn.sigmoid(g))).astype(o_ref.dtype)
        return carry

    lax.fori_loop(0, seq // c, chunk, 0, unroll=2)
    s_out_ref[...] = s_scr[...]


def _retention_prompt(p, cos, sin, b, seq):
    half = 4 * MIX_WIDTH
    blocks = 2 * seq * half * 2 + 2 * seq * V7X_LANES * 4 + 2 * seq * MIX_WIDTH * 2 + 3 * HEADS * HEAD_DIM * HEAD_DIM * 4
    return pl.pallas_call(
        functools.partial(_ret_kernel, seq=seq),
        out_shape=(jax.ShapeDtypeStruct((b * seq, MIX_WIDTH), BF16),
                   jax.ShapeDtypeStruct((b, HEADS, HEAD_DIM, HEAD_DIM), F32)),
        grid=(b,),
        in_specs=[pl.BlockSpec((seq, half), lambda i: (i, 0)),
                  _resident((seq, HEAD_DIM), lambda i: (0, 0)),
                  _resident((seq, HEAD_DIM), lambda i: (0, 0))],
        out_specs=(pl.BlockSpec((seq, MIX_WIDTH), lambda i: (i, 0)),
                   pl.BlockSpec((None, HEADS, HEAD_DIM, HEAD_DIM), lambda i: (i, 0, 0, 0))),
        scratch_shapes=[pltpu.VMEM((HEADS, HEAD_DIM, HEAD_DIM), F32)],
        compiler_params=_params(blocks, 1),
        name="retention",
    )(p, cos, sin)


N_REP = 3 * HEADS


def _split_dot(x, w):
    hi = x.astype(BF16)
    lo = (x - hi.astype(F32)).astype(BF16)
    return _dot(hi, w) + _dot(lo, w)


def _split_dot_tn(x, w):
    hi = x.astype(BF16)
    lo = (x - hi.astype(F32)).astype(BF16)
    return _dot_tn(hi, w) + _dot_tn(lo, w)


def _mlstm_kernel(p_ref, gif_ref, bias_ref, o_ref, c_out_ref, n_out_ref, m_out_ref,
                  cn_scr, m_scr, rows_scr, rep_scr, *, seq):
    c = CHUNK
    nch = seq // c
    nrow = 8 * nch
    heads = range(HEADS)
    neg_inf = float("-inf")
    row = lax.broadcasted_iota(jnp.int32, (c, c), 0)
    col = lax.broadcasted_iota(jnp.int32, (c, c), 1)
    causal = row >= col
    triu = jnp.where(row <= col, 1.0, 0.0).astype(BF16)
    lane = lax.broadcasted_iota(jnp.int32, (c, V7X_LANES), 1)
    ones = jnp.ones((c, V7X_LANES), BF16)
    er = lax.broadcasted_iota(jnp.int32, (16, N_REP * V7X_LANES), 0)
    ec = lax.broadcasted_iota(jnp.int32, (16, N_REP * V7X_LANES), 1)
    expand = jnp.where(ec // V7X_LANES == er, 1.0, 0.0).astype(BF16)
    cn_scr[...] = jnp.zeros_like(cn_scr)
    m_scr[...] = jnp.zeros_like(m_scr)

    pieces = []
    for j in range(nch):
        fb = gif_ref[j * c:(j + 1) * c, :] + bias_ref[...]
        gl = jnp.where(lane < HEADS, fb, jax.nn.log_sigmoid(fb))
        pieces.append(gl.T[0:8, :])
    x = jnp.concatenate(pieces, axis=0)
    cum = _split_dot(x, triu)
    sub = lax.broadcasted_iota(jnp.int32, (nrow, c), 0) % 8
    tok = lax.broadcasted_iota(jnp.int32, (nrow, c), 1)
    a = x - pltpu.roll(cum, nrow - HEADS, 0)
    d = 1
    while d < c:
        a = jnp.maximum(a, jnp.where(tok >= d, pltpu.roll(a, d, 1), neg_inf))
        d *= 2
    rows_scr[0] = jnp.where(sub < HEADS, x, cum)
    rows_scr[1] = a

    def replicate(j):
        r8 = pl.ds(pl.multiple_of(j * 8, 8), 8)
        r16 = jnp.concatenate([rows_scr[0, r8, :], rows_scr[1, r8, :]], axis=0)
        return _split_dot_tn(r16, expand)

    rep_scr[0] = replicate(0)

    def chunk(j, carry):
        r = pl.ds(pl.multiple_of(j * c, c), c)
        r8 = pl.ds(pl.multiple_of(j * 8, 8), 8)
        slot = j % 2
        rep_next = replicate(jnp.minimum(j + 1, nch - 1))
        sl = lambda part, h: p_ref[r, part * MIX_WIDTH + h * HEAD_DIM:part * MIX_WIDTH + (h + 1) * HEAD_DIM]
        blk = lambda n, h: rep_scr[slot, :, (n * HEADS + h) * V7X_LANES:(n * HEADS + h + 1) * V7X_LANES]
        y8 = rows_scr[0, r8, :]
        q = [sl(0, h).astype(BF16) for h in heads]
        k = [sl(1, h).astype(BF16) for h in heads]
        v1 = [jnp.concatenate([sl(2, h).astype(BF16), ones], axis=1) for h in heads]
        cn = [cn_scr[h] for h in heads]
        m_prev = [m_scr[h] for h in heads]
        ic = [blk(0, h) for h in heads]
        bc = [blk(1, h) for h in heads]
        cmc = [blk(2, h) for h in heads]
        qk = [_dot_nt(q[h], k[h]) for h in heads]
        qcn = [_dot(q[h], cn[h].astype(BF16)) for h in heads]
        b_last = [bc[h][c - 1:c, :] for h in heads]
        wk_log = [b_last[h] - bc[h] + ic[h] for h in heads]
        m_new = [jnp.maximum(b_last[h] + m_prev[h], jnp.max(wk_log[h], axis=0, keepdims=True)) for h in heads]
        wk = [jnp.exp(wk_log[h] - m_new[h]) * QK_SCALE for h in heads]
        decay = [jnp.exp(b_last[h] + m_prev[h] - m_new[h]) for h in heads]
        kw = [(k[h].astype(F32) * wk[h]).astype(BF16) for h in heads]
        upd = [_dot_tn(kw[h], v1[h]) for h in heads]
        m_t = [bc[h] + jnp.maximum(m_prev[h], cmc[h]) for h in heads]
        dlog = [jnp.where(causal, bc[h] - y8[HEADS + h:HEADS + h + 1, :] + y8[h:h + 1, :], neg_inf) for h in heads]
        w_intra = [jnp.exp(dlog[h] - m_t[h]) * QK_SCALE for h in heads]
        w_inter = [jnp.exp(bc[h] + m_prev[h] - m_t[h]) for h in heads]
        s = [(qk[h] * w_intra[h]).astype(BF16) for h in heads]
        sv = [_dot(s[h], v1[h]) for h in heads]
        hh = []
        for h in heads:
            num = sv[h][:, :HEAD_DIM] + w_inter[h] * qcn[h][:, :HEAD_DIM]
            den = sv[h][:, HEAD_DIM:] + w_inter[h] * qcn[h][:, HEAD_DIM:]
            hh.append(num / jnp.maximum(jnp.abs(den), jnp.exp(-m_t[h])))
        for h in heads:
            cn_scr[h] = jnp.concatenate([decay[h], decay[h]], axis=1) * cn[h] + upd[h]
            m_scr[h] = m_new[h]
        on = [_head_norm(hh[h]) for h in heads]
        for h in heads:
            o_ref[r, h * HEAD_DIM:(h + 1) * HEAD_DIM] = (on[h] * jax.nn.sigmoid(sl(3, h).astype(F32))).astype(o_ref.dtype)
        rep_scr[1 - slot] = rep_next
        return carry

    lax.fori_loop(0, nch, chunk, 0, unroll=2)
    for h in heads:
        cnh = cn_scr[h]
        c_out_ref[h] = cnh[:, :HEAD_DIM]
        n_out_ref[h:h + 1, :] = cnh[:, HEAD_DIM:].T[0:1, :]
        m_out_ref[h:h + 1, :] = m_scr[h]


def _mlstm_prompt(p, gif, bias, layer, b, seq):
    half = 4 * MIX_WIDTH
    nch = seq // CHUNK
    blocks = (2 * seq * half * 2 + 2 * seq * N_GIF * 4 + 2 * seq * MIX_WIDTH * 2
              + 5 * HEADS * HEAD_DIM * 2 * HEAD_DIM * 4 + 4 * CHUNK * N_REP * V7X_LANES * 4)
    return pl.pallas_call(
        functools.partial(_mlstm_kernel, seq=seq),
        out_shape=(jax.ShapeDtypeStruct((b * seq, MIX_WIDTH), BF16),
                   jax.ShapeDtypeStruct((b, HEADS, HEAD_DIM, HEAD_DIM), F32),
                   jax.ShapeDtypeStruct((b, HEADS, HEAD_DIM), F32),
                   jax.ShapeDtypeStruct((b, HEADS, V7X_LANES), F32)),
        grid=(b,),
        in_specs=[pl.BlockSpec((seq, half), lambda i: (i, 1)),
                  pl.BlockSpec((seq, N_GIF), lambda i: (i, 0)),
                  _resident((None, 1, N_GIF), lambda i: (layer, 0, 0))],
        out_specs=(pl.BlockSpec((seq, MIX_WIDTH), lambda i: (i, 0)),
                   pl.BlockSpec((None, HEADS, HEAD_DIM, HEAD_DIM), lambda i: (i, 0, 0, 0)),
                   pl.BlockSpec((None, HEADS, HEAD_DIM), lambda i: (i, 0, 0)),
                   pl.BlockSpec((None, HEADS, V7X_LANES), lambda i: (i, 0, 0))),
        scratch_shapes=[pltpu.VMEM((HEADS, HEAD_DIM, 2 * HEAD_DIM), F32),
                        pltpu.VMEM((HEADS, 1, V7X_LANES), F32),
                        pltpu.VMEM((2, 8 * nch, CHUNK), F32),
                        pltpu.VMEM((2, CHUNK, N_REP * V7X_LANES), F32)],
        compiler_params=_params(blocks, 1),
        name="mlstm",
    )(p, gif, bias)


SEQ_BLOCK = 8


def _column_expander():
    er = lax.broadcasted_iota(jnp.int32, (SEQ_BLOCK, SEQ_BLOCK * V7X_LANES), 0)
    ec = lax.broadcasted_iota(jnp.int32, (SEQ_BLOCK, SEQ_BLOCK * V7X_LANES), 1)
    return jnp.where(ec // V7X_LANES == er, 1.0, 0.0).astype(BF16)


def _columns(x, expander):
    return _dot_tn(x.astype(BF16), expander)


def _ret_step_kernel(p_ref, cos_ref, sin_ref, s_ref, *rest, layer, first):
    o_ref, s_out_ref, row_scr = rest[-3:]
    s_out_ref = _own_layer(s_out_ref, layer, first)
    cs = cos_ref[...]
    sn = sin_ref[...]
    expander = _column_expander()
    for h in range(HEADS):
        lo = h * HEAD_DIM
        gamma = math.exp(LOG_GAMMA[h])
        q = _rope(p_ref[:, lo:lo + HEAD_DIM], cs, sn)
        k = _rope(p_ref[:, MIX_WIDTH + lo:MIX_WIDTH + lo + HEAD_DIM], cs, sn) * QK_SCALE
        v = p_ref[:, 2 * MIX_WIDTH + lo:2 * MIX_WIDTH + lo + HEAD_DIM]
        g = p_ref[:, 3 * MIX_WIDTH + lo:3 * MIX_WIDTH + lo + HEAD_DIM]
        qt = _columns(q * gamma, expander)
        kt = _columns(k, expander)
        for b in range(SEQ_BLOCK):
            s = s_ref[b, h]
            blk = slice(b * V7X_LANES, (b + 1) * V7X_LANES)
            row_scr[b:b + 1, :] = jnp.sum(qt[:, blk] * s, axis=0, keepdims=True)
            s_out_ref[b, h] = gamma * s + kt[:, blk] * v[b:b + 1, :]
        o = jnp.sum(q * k, axis=-1, keepdims=True) * v + row_scr[...]
        o_ref[:, lo:lo + HEAD_DIM] = _head_norm(o) * (g * jax.nn.sigmoid(g))


def _stacked_state(acc, layer):
    tail = (HEADS, HEAD_DIM, HEAD_DIM)
    if acc is None:
        return (), [], pl.BlockSpec((DEPTH, SEQ_BLOCK) + tail, lambda i: (0, i, 0, 0, 0))
    return (acc,), [pl.BlockSpec(memory_space=pl.ANY)], pl.BlockSpec((None, SEQ_BLOCK) + tail, lambda i: (layer, i, 0, 0, 0))


def _own_layer(state_out_ref, layer, first):
    if not first:
        return state_out_ref
    for other in range(DEPTH):
        if other != layer:
            state_out_ref[other] = jnp.zeros(state_out_ref.shape[1:], F32)
    return state_out_ref.at[layer]


def _retention_step(p, cos, sin, state, layer, acc):
    nseq = p.shape[0]
    half = 4 * MIX_WIDTH
    sblk = SEQ_BLOCK * HEADS * HEAD_DIM * HEAD_DIM * 4
    blocks = (2 + 2 * (DEPTH if acc is None else 1)) * sblk + 2 * SEQ_BLOCK * half * 4 + 2 * SEQ_BLOCK * MIX_WIDTH * 4
    state_blk = pl.BlockSpec((None, SEQ_BLOCK, HEADS, HEAD_DIM, HEAD_DIM), lambda i: (layer, i, 0, 0, 0))
    extra, extra_specs, out_state_blk = _stacked_state(acc, layer)
    return pl.pallas_call(
        functools.partial(_ret_step_kernel, layer=layer, first=acc is None),
        out_shape=(jax.ShapeDtypeStruct((nseq, MIX_WIDTH), F32),
                   jax.ShapeDtypeStruct(state.shape, F32)),
        grid=(nseq // SEQ_BLOCK,),
        in_specs=[pl.BlockSpec((SEQ_BLOCK, half), lambda i: (i, 0)),
                  _resident((1, HEAD_DIM), lambda i: (0, 0)),
                  _resident((1, HEAD_DIM), lambda i: (0, 0)),
                  state_blk] + extra_specs,
        out_specs=(pl.BlockSpec((SEQ_BLOCK, MIX_WIDTH), lambda i: (i, 0)), out_state_blk),
        input_output_aliases={4: 1} if extra else {},
        scratch_shapes=[pltpu.VMEM((SEQ_BLOCK, HEAD_DIM), F32)],
        compiler_params=_params(blocks, 1),
        name="retention_step",
    )(p, cos, sin, state, *extra)


def _mlstm_step_kernel(p_ref, gif_ref, bias_ref, c_ref, n_ref, m_ref, *rest, layer, first):
    o_ref, c_out_ref, n_out_ref, m_out_ref, row_scr = rest[-5:]
    c_out_ref = _own_layer(c_out_ref, layer, first)
    fb = gif_ref[...] + bias_ref[...]
    lf_all = jax.nn.log_sigmoid(fb)
    lane = lax.broadcasted_iota(jnp.int32, (SEQ_BLOCK, V7X_LANES), 1)
    m_all = m_ref[...]
    m_acc = jnp.zeros((SEQ_BLOCK, V7X_LANES), F32)
    expander = _column_expander()
    for h in range(HEADS):
        lo = h * HEAD_DIM
        q = p_ref[:, lo:lo + HEAD_DIM]
        k = p_ref[:, MIX_WIDTH + lo:MIX_WIDTH + lo + HEAD_DIM] * QK_SCALE
        v = p_ref[:, 2 * MIX_WIDTH + lo:2 * MIX_WIDTH + lo + HEAD_DIM]
        og = p_ref[:, 3 * MIX_WIDTH + lo:3 * MIX_WIDTH + lo + HEAD_DIM]
        ig = fb[:, h:h + 1]
        inter = lf_all[:, HEADS + h:HEADS + h + 1] + m_all[:, h:h + 1]
        m_t = jnp.maximum(inter, ig)
        w_intra = jnp.exp(ig - m_t)
        w_inter = jnp.exp(inter - m_t)
        nv = n_ref[:, lo:lo + HEAD_DIM]
        kw = k * w_intra
        qt = _columns(q, expander)
        kt = _columns(kw, expander)
        for b in range(SEQ_BLOCK):
            cm = c_ref[b, h]
            blk = slice(b * V7X_LANES, (b + 1) * V7X_LANES)
            row_scr[b:b + 1, :] = jnp.sum(qt[:, blk] * cm, axis=0, keepdims=True)
            c_out_ref[b, h] = w_inter[b:b + 1, :] * cm + kt[:, blk] * v[b:b + 1, :]
        s = jnp.sum(q * k, axis=-1, keepdims=True) * w_intra
        num = s * v + w_inter * row_scr[...]
        den = s + w_inter * jnp.sum(q * nv, axis=-1, keepdims=True)
        hh = num / jnp.maximum(jnp.abs(den), jnp.exp(-m_t))
        n_out_ref[:, lo:lo + HEAD_DIM] = w_inter * nv + kw
        m_acc = jnp.where(lane == h, m_t, m_acc)
        o_ref[:, lo:lo + HEAD_DIM] = _head_norm(hh) * jax.nn.sigmoid(og)
    m_out_ref[...] = m_acc


def _mlstm_step(p, gif, bias, c_state, n_state, m_state, layer, acc):
    nseq = p.shape[0]
    half = 4 * MIX_WIDTH
    sblk = SEQ_BLOCK * HEADS * HEAD_DIM * HEAD_DIM * 4
    blocks = (2 + 2 * (DEPTH if acc is None else 1)) * sblk + 2 * SEQ_BLOCK * half * 4 + 8 * SEQ_BLOCK * MIX_WIDTH * 4
    rows = lambda w: pl.BlockSpec((SEQ_BLOCK, w), lambda i: (i, 0))
    state_blk = pl.BlockSpec((None, SEQ_BLOCK, HEADS, HEAD_DIM, HEAD_DIM), lambda i: (layer, i, 0, 0, 0))
    extra, extra_specs, out_state_blk = _stacked_state(acc, layer)
    return pl.pallas_call(
        functools.partial(_mlstm_step_kernel, layer=layer, first=acc is None),
        out_shape=(jax.ShapeDtypeStruct((nseq, MIX_WIDTH), F32),
                   jax.ShapeDtypeStruct(c_state.shape, F32),
                   jax.ShapeDtypeStruct((nseq, MIX_WIDTH), F32),
                   jax.ShapeDtypeStruct((nseq, V7X_LANES), F32)),
        grid=(nseq // SEQ_BLOCK,),
        in_specs=[pl.BlockSpec((SEQ_BLOCK, half), lambda i: (i, 1)),
                  rows(N_GIF),
                  _resident((None, 1, N_GIF), lambda i: (layer, 0, 0)),
                  state_blk,
                  pl.BlockSpec((None, SEQ_BLOCK, MIX_WIDTH), lambda i: (layer, i, 0)),
                  pl.BlockSpec((None, SEQ_BLOCK, HEADS), lambda i: (layer, i, 0))] + extra_specs,
        out_specs=(rows(MIX_WIDTH), out_state_blk, rows(MIX_WIDTH), rows(V7X_LANES)),
        input_output_aliases={6: 1} if extra else {},
        scratch_shapes=[pltpu.VMEM((SEQ_BLOCK, HEAD_DIM), F32)],
        compiler_params=_params(blocks, 1),
        name="mlstm_step",
    )(p, gif, bias, c_state, n_state, m_state, *extra)


def _rope_tables(pos):
    inv = ROPE_THETA ** (-jnp.arange(0, HEAD_DIM, 2, dtype=F32) / HEAD_DIM)
    ang = pos.astype(F32)[:, None] * inv[None, :]
    cos, sin = jnp.cos(ang), jnp.sin(ang)
    return jnp.concatenate([cos, cos], axis=-1), jnp.concatenate([-sin, sin], axis=-1)


W_IN_BLOCKS = N_IN_PADDED // W_IN_ROWS


def _w_in_source_row(j):
    o = j * W_IN_ROWS
    off_gif = N_MAIN
    off_su = off_gif + 2 * HEADS
    off_gates = off_su + MIX_WIDTH
    return jnp.where(o < OFF_GATES, o,
                     jnp.where(o < OFF_SU, o - OFF_GATES + off_gates,
                               jnp.where(o < OFF_GIF, o - OFF_SU + off_su, off_gif)))


def _w_in_layout_kernel(w_hbm, o_ref, buf, sem):
    l, j = pl.program_id(0), pl.program_id(1)
    step = l * W_IN_BLOCKS + j
    total = pl.num_programs(0) * W_IN_BLOCKS

    def read(s):
        src = pl.multiple_of(_w_in_source_row(s % W_IN_BLOCKS), 8)
        return pltpu.make_async_copy(w_hbm.at[s // W_IN_BLOCKS, pl.ds(src, W_IN_ROWS), :], buf.at[s % 2], sem.at[s % 2])

    @pl.when(step == 0)
    def _():
        read(step).start()

    @pl.when(step + 1 < total)
    def _():
        read(step + 1).start()

    read(step).wait()
    x = buf[step % 2]
    row = lax.broadcasted_iota(jnp.int32, x.shape, 0)
    keep = jnp.logical_or(j < W_IN_BLOCKS - 1, row < 2 * HEADS)
    o_ref[...] = jnp.where(keep, x, 0.0).astype(BF16)


def _w_in_layout(w_in):
    depth = w_in.shape[0]
    w_t = jnp.swapaxes(w_in, 1, 2)
    blocks = 2 * W_IN_ROWS * D_MODEL * 4 + 2 * W_IN_ROWS * D_MODEL * 2 + 2 * W_IN_ROWS * D_MODEL * 4
    return pl.pallas_call(
        _w_in_layout_kernel,
        out_shape=jax.ShapeDtypeStruct((depth, N_IN_PADDED, D_MODEL), BF16),
        grid=(depth, W_IN_BLOCKS),
        in_specs=[pl.BlockSpec(memory_space=pl.ANY)],
        out_specs=pl.BlockSpec((None, W_IN_ROWS, D_MODEL), lambda l, j: (l, j, 0)),
        scratch_shapes=[pltpu.VMEM((2, W_IN_ROWS, D_MODEL), F32), pltpu.SemaphoreType.DMA((2,))],
        compiler_params=_params(blocks, 2),
        name="w_in_layout",
    )(w_t)


def _prepare_weights(w_ffn1_gu, w_ffn1_down, w_in, w_s5_glu, w_branch, w_out, w_ffn2_gu, w_ffn2_down):
    cast = lambda w: w.astype(BF16)
    return dict(gu1=cast(w_ffn1_gu), d1=cast(w_ffn1_down), w_in=_w_in_layout(w_in), glu=cast(w_s5_glu),
                branch=cast(w_branch), out=cast(w_out), gu2=cast(w_ffn2_gu), d2=cast(w_ffn2_down))


def _run_trunk(x, nseq, seq, states, wts, norms, s5w, bias, ln_final, prompt):
    t = nseq * seq
    tm = 512 if t % 512 == 0 else t
    act_dtype = BF16 if prompt else F32
    if prompt:
        cos, sin = _rope_tables(jnp.arange(seq, dtype=jnp.int32))
    else:
        cos, sin = _rope_tables(PAST_LEN + jnp.arange(seq, dtype=jnp.int32))
    if prompt:
        x0 = jnp.zeros((nseq, 2 * S5_N), F32)
    else:
        st_ret, st_mc, st_mn, st_mm, st_sre, st_sim = states
        st_mn = st_mn.reshape(DEPTH, nseq, MIX_WIDTH)
        x0 = jnp.concatenate([st_sre.reshape(DEPTH, nseq, S5_N), st_sim.reshape(DEPTH, nseq, S5_N)], axis=-1)
    rets, mcs, mns, mms, xls = [], [], [], [], []
    ret_acc = mc_acc = None
    for l in range(DEPTH):
        x = _ffn(x, norms["ffn1"], wts["gu1"], wts["d1"], ln_final, l, tm, final=False)
        main, gates, su, gif = _inproj(x, norms["mix"], wts["w_in"], l, tm, act_dtype)
        if prompt:
            r_out, new_ret = _retention_prompt(main, cos, sin, nseq, seq)
            m_out, new_mc, new_mn, new_mm = _mlstm_prompt(main, gif, bias, l, nseq, seq)
            s_out, x_last = _s5_prompt(su.reshape(nseq, seq, MIX_WIDTH), x0, s5w, l)
            rets.append(new_ret)
            mcs.append(new_mc)
        else:
            r_out, ret_acc = _retention_step(main, cos, sin, st_ret, l, ret_acc)
            m_out, mc_acc, new_mn, new_mm = _mlstm_step(main, gif, bias, st_mc, st_mn, st_mm, l, mc_acc)
            s_out, x_last = _s5_step(su, x0, s5w, l)
        mns.append(new_mn)
        mms.append(new_mm)
        xls.append(x_last)
        x = _merge_ffn(x, r_out, m_out, s_out, gates, wts["branch"], wts["out"], norms["ffn2"], wts["gu2"], wts["d2"],
                       ln_final, l, tm, final=(l == DEPTH - 1))
    new_ret, new_mc = (jnp.stack(rets), jnp.stack(mcs)) if prompt else (ret_acc, mc_acc)
    new_mn = jnp.stack(mns).reshape(DEPTH, nseq, HEADS, HEAD_DIM)
    new_mm = jnp.stack(mms)[:, :, :, 0] if prompt else jnp.stack(mms)[:, :, :HEADS]
    xl = jnp.stack(xls)
    new_sre = xl[:, :, :S5_N].reshape(DEPTH, nseq, S5_GROUPS, S5_STATE)
    new_sim = xl[:, :, S5_N:].reshape(DEPTH, nseq, S5_GROUPS, S5_STATE)
    return x, (new_ret, new_mc, new_mn, new_mm, new_sre, new_sim)


def kernel(x_prompt, x_sample, state_ret, state_mlstm_c, state_mlstm_n, state_mlstm_m, state_s5_re, state_s5_im, ln_ffn1, w_ffn1_gu, w_ffn1_down, ln_mix, w_in, b_gates, s5_a_re, s5_a_im, s5_b_re, s5_b_im, s5_c_re, s5_c_im, s5_d, s5_log_dt, w_s5_glu, w_branch, w_out, ln_ffn2, w_ffn2_gu, w_ffn2_down, ln_final):
    wts = _prepare_weights(w_ffn1_gu, w_ffn1_down, w_in, w_s5_glu, w_branch, w_out, w_ffn2_gu, w_ffn2_down)
    norms = dict(ffn1=ln_ffn1.reshape(DEPTH, 1, D_MODEL), mix=ln_mix.reshape(DEPTH, 1, D_MODEL),
                 ffn2=ln_ffn2.reshape(DEPTH, 1, D_MODEL))
    lnf = ln_final.reshape(1, D_MODEL)
    a_row, bbr, bbi = _s5_discretise(s5_a_re, s5_a_im, s5_log_dt, s5_b_re, s5_b_im)
    s5w = (a_row, _block_diag_in(bbr), _block_diag_in(bbi), _block_diag_out(s5_c_re), _block_diag_out(-s5_c_im),
           s5_d.reshape(DEPTH, 1, MIX_WIDTH), wts["glu"])
    bias = jnp.pad(b_gates, ((0, 0), (0, N_GIF - 2 * HEADS))).reshape(DEPTH, 1, N_GIF)

    pb, pl_len, _ = x_prompt.shape
    sb, sl, _ = x_sample.shape
    y_p, st_p = _run_trunk(x_prompt.reshape(pb * pl_len, D_MODEL), pb, pl_len, None, wts, norms, s5w, bias, lnf, True)
    sample_states = (state_ret, state_mlstm_c, state_mlstm_n, state_mlstm_m, state_s5_re, state_s5_im)
    y_s, st_s = _run_trunk(x_sample.reshape(sb * sl, D_MODEL), sb, sl, sample_states, wts, norms, s5w, bias, lnf, False)
    return (y_p.reshape(pb, pl_len, D_MODEL), y_s.reshape(sb, sl, D_MODEL)) + st_p + st_s
```

```python
import functools
import math

import jax
import jax.numpy as jnp
from jax import lax
from jax.experimental import pallas as pl
from jax.experimental.pallas import tpu as pltpu

F32 = jnp.float32
BF16 = jnp.bfloat16

D_MODEL = 1024
DEPTH = 4
MIX_WIDTH = D_MODEL // 2
N_BRANCH = 3
HEADS = 4
HEAD_DIM = MIX_WIDTH // HEADS
S5_GROUP = 16
S5_GROUPS = MIX_WIDTH // S5_GROUP
S5_STATE = 64
S5_N = S5_GROUPS * S5_STATE
D_FF = 2816
CHUNK = 128
PAST_LEN = 16384
ROPE_THETA = 10000.0
EPS = 1e-6

N_MAIN = 8 * MIX_WIDTH
N_GATES = N_BRANCH * D_MODEL
N_GIF = 128
W_IN_ROWS = 512
N_IN_PADDED = N_MAIN + N_GATES + MIX_WIDTH + W_IN_ROWS
OFF_GATES = N_MAIN
OFF_SU = N_MAIN + N_GATES
OFF_GIF = OFF_SU + MIX_WIDTH

V7X_LANES = 128
V7X_MXU_DIM = 256
V7X_VMEM_BYTES = 64 * 1024 * 1024

FF_CHUNK = V7X_MXU_DIM
PROJ_CHUNK = 2 * V7X_MXU_DIM
S5_HALF = MIX_WIDTH // 2
S5_HALF_N = S5_N // 2
S5_TOKENS = 64
LOG_GAMMA = tuple(math.log1p(-(2.0 ** (-5.0 - h))) for h in range(HEADS))
QK_SCALE = HEAD_DIM ** -0.5


def _vmem_limit(block_bytes):
    want = int(block_bytes * 1.25) + (8 << 20)
    return min(want, V7X_VMEM_BYTES - (6 << 20))


def _params(block_bytes, n_grid):
    return pltpu.CompilerParams(
        dimension_semantics=("arbitrary",) * n_grid,
        vmem_limit_bytes=_vmem_limit(block_bytes))


def _resident(shape, index_map):
    return pl.BlockSpec(shape, index_map, pipeline_mode=pl.Buffered(1))


def _dot(a, b):
    return jnp.dot(a, b, preferred_element_type=F32)


def _dot_nt(a, b):
    return lax.dot_general(a, b, (((1,), (1,)), ((), ())), preferred_element_type=F32)


def _dot_tn(a, b):
    return lax.dot_general(a, b, (((0,), (0,)), ((), ())), preferred_element_type=F32)


def _rms(x, g):
    return x * lax.rsqrt(jnp.mean(x * x, axis=-1, keepdims=True) + EPS) * g


def _head_norm(x):
    mu = jnp.mean(x, axis=-1, keepdims=True)
    xc = x - mu
    var = jnp.mean(xc * xc, axis=-1, keepdims=True)
    return xc * lax.rsqrt(var + EPS)


def _rope(x, cos, sin_signed):
    return x * cos + pltpu.roll(x, HEAD_DIM // 2, 1) * sin_signed


def _ffn_tile(x, ln_ref, wgu_ref, wd_ref, lnf_ref, o_ref, acc_ref, final):
    xn = _rms(x, ln_ref[...]).astype(BF16)
    for c in range(D_FF // FF_CHUNK):
        lo = c * FF_CHUNK
        g = _dot(xn, wgu_ref[:, lo:lo + FF_CHUNK])
        u = _dot(xn, wgu_ref[:, D_FF + lo:D_FF + lo + FF_CHUNK])
        a = (g * jax.nn.sigmoid(g) * u).astype(BF16)
        d = _dot(a, wd_ref[lo:lo + FF_CHUNK, :])
        if c == 0:
            acc_ref[...] = d
        else:
            acc_ref[...] += d
    y = x + 0.5 * acc_ref[...]
    if final:
        y = _rms(y, lnf_ref[...])
    o_ref[...] = y


def _ffn_kernel(x_ref, ln_ref, wgu_ref, wd_ref, lnf_ref, o_ref, acc_ref, *, final):
    _ffn_tile(x_ref[...], ln_ref, wgu_ref, wd_ref, lnf_ref, o_ref, acc_ref, final)


def _ffn(x, ln, wgu, wd, lnf, layer, tm, final):
    t = x.shape[0]
    blocks = (4 * tm * D_MODEL * 4 + tm * D_MODEL * 4 + D_MODEL * 2 * D_FF * 2 + D_FF * D_MODEL * 2
              + 4 * tm * FF_CHUNK * 4)
    return pl.pallas_call(
        functools.partial(_ffn_kernel, final=final),
        out_shape=jax.ShapeDtypeStruct((t, D_MODEL), F32),
        grid=(t // tm,),
        in_specs=[
            pl.BlockSpec((tm, D_MODEL), lambda i: (i, 0)),
            _resident((None, 1, D_MODEL), lambda i: (layer, 0, 0)),
            _resident((None, D_MODEL, 2 * D_FF), lambda i: (layer, 0, 0)),
            _resident((None, D_FF, D_MODEL), lambda i: (layer, 0, 0)),
            _resident((1, D_MODEL), lambda i: (0, 0)),
        ],
        out_specs=pl.BlockSpec((tm, D_MODEL), lambda i: (i, 0)),
        scratch_shapes=[pltpu.VMEM((tm, D_MODEL), F32)],
        compiler_params=_params(blocks, 1),
        name="ffn",
    )(x, ln, wgu, wd, lnf)


def _inproj_kernel(x_ref, ln_ref, w_ref, main_ref, gates_ref, su_ref, gif_ref):
    xn = _rms(x_ref[...], ln_ref[...]).astype(BF16)
    for c in range(N_MAIN // PROJ_CHUNK):
        lo = c * PROJ_CHUNK
        main_ref[:, lo:lo + PROJ_CHUNK] = _dot_nt(xn, w_ref[lo:lo + PROJ_CHUNK, :]).astype(main_ref.dtype)
    for c in range(N_GATES // PROJ_CHUNK):
        lo = c * PROJ_CHUNK
        gates_ref[:, lo:lo + PROJ_CHUNK] = _dot_nt(
            xn, w_ref[OFF_GATES + lo:OFF_GATES + lo + PROJ_CHUNK, :]).astype(gates_ref.dtype)
    su_ref[...] = _dot_nt(xn, w_ref[OFF_SU:OFF_SU + MIX_WIDTH, :])
    gif_ref[...] = _dot_nt(xn, w_ref[OFF_GIF:OFF_GIF + N_GIF, :])


def _inproj(x, ln, w, layer, tm, act_dtype):
    t = x.shape[0]
    ab = jnp.dtype(act_dtype).itemsize
    blocks = (2 * tm * D_MODEL * 4 + D_MODEL * N_IN_PADDED * 2
              + 2 * tm * (N_MAIN + N_GATES) * ab + 2 * tm * (MIX_WIDTH + N_GIF) * 4
              + 4 * tm * PROJ_CHUNK * 4)
    return pl.pallas_call(
        _inproj_kernel,
        out_shape=(jax.ShapeDtypeStruct((t, N_MAIN), act_dtype),
                   jax.ShapeDtypeStruct((t, N_GATES), act_dtype),
                   jax.ShapeDtypeStruct((t, MIX_WIDTH), F32),
                   jax.ShapeDtypeStruct((t, N_GIF), F32)),
        grid=(t // tm,),
        in_specs=[
            pl.BlockSpec((tm, D_MODEL), lambda i: (i, 0)),
            _resident((None, 1, D_MODEL), lambda i: (layer, 0, 0)),
            _resident((None, N_IN_PADDED, D_MODEL), lambda i: (layer, 0, 0)),
        ],
        out_specs=(pl.BlockSpec((tm, N_MAIN), lambda i: (i, 0)),
                   pl.BlockSpec((tm, N_GATES), lambda i: (i, 0)),
                   pl.BlockSpec((tm, MIX_WIDTH), lambda i: (i, 0)),
                   pl.BlockSpec((tm, N_GIF), lambda i: (i, 0))),
        compiler_params=_params(blocks, 1),
        name="inproj",
    )(x, ln, w)


def _merge_ffn_kernel(x_ref, r_ref, m_ref, s_ref, g_ref, wb_ref, wo_ref, ln_ref, wgu_ref, wd_ref, lnf_ref,
                      o_ref, acc_ref, *, final):
    merged = None
    for n, b_ref in enumerate((r_ref, m_ref, s_ref)):
        up = _dot(b_ref[...].astype(BF16), wb_ref[n])
        gate = jax.nn.sigmoid(g_ref[:, n * D_MODEL:(n + 1) * D_MODEL].astype(F32))
        merged = gate * up if merged is None else merged + gate * up
    x1 = x_ref[...] + _dot(merged.astype(BF16), wo_ref[...])
    _ffn_tile(x1, ln_ref, wgu_ref, wd_ref, lnf_ref, o_ref, acc_ref, final)


def _merge_ffn(x, r, m, s, gates, wb, wo, ln, wgu, wd, lnf, layer, tm, final):
    t = x.shape[0]
    blocks = (4 * tm * D_MODEL * 4 + 6 * tm * MIX_WIDTH * 4 + 2 * tm * N_GATES * 4
              + N_BRANCH * MIX_WIDTH * D_MODEL * 2 + D_MODEL * D_MODEL * 2 + 4 * tm * D_MODEL * 4
              + tm * D_MODEL * 4 + D_MODEL * 2 * D_FF * 2 + D_FF * D_MODEL * 2 + 4 * tm * FF_CHUNK * 4)
    row = lambda w: pl.BlockSpec((tm, w), lambda i: (i, 0))
    if s.ndim == 3:
        assert s.shape[1] % tm == 0
        per_seq = s.shape[1] // tm
        s_spec = pl.BlockSpec((None, tm, MIX_WIDTH), lambda i: (i // per_seq, i % per_seq, 0))
    else:
        s_spec = row(MIX_WIDTH)
    return pl.pallas_call(
        functools.partial(_merge_ffn_kernel, final=final),
        out_shape=jax.ShapeDtypeStruct((t, D_MODEL), F32),
        grid=(t // tm,),
        in_specs=[row(D_MODEL), row(MIX_WIDTH), row(MIX_WIDTH), s_spec, row(N_GATES),
                  _resident((None, N_BRANCH, MIX_WIDTH, D_MODEL), lambda i: (layer, 0, 0, 0)),
                  _resident((None, D_MODEL, D_MODEL), lambda i: (layer, 0, 0)),
                  _resident((None, 1, D_MODEL), lambda i: (layer, 0, 0)),
                  _resident((None, D_MODEL, 2 * D_FF), lambda i: (layer, 0, 0)),
                  _resident((None, D_FF, D_MODEL), lambda i: (layer, 0, 0)),
                  _resident((1, D_MODEL), lambda i: (0, 0))],
        out_specs=row(D_MODEL),
        scratch_shapes=[pltpu.VMEM((tm, D_MODEL), F32)],
        compiler_params=_params(blocks, 1),
        name="merge_ffn",
    )(x, r, m, s, gates, wb, wo, ln, wgu, wd, lnf)


def _s5_disc_kernel(are_ref, aim_ref, ldt_ref, bre_ref, bim_ref, abr_ref, abi_ref, bbr_ref, bbi_ref):
    ar, ai = are_ref[...], aim_ref[...]
    dt = jnp.exp(ldt_ref[...])
    mag = jnp.exp(ar * dt)
    abr = mag * jnp.cos(ai * dt)
    abi = mag * jnp.sin(ai * dt)
    nr, ni = abr - 1.0, abi
    den = ar * ar + ai * ai
    cr = (nr * ar + ni * ai) / den
    ci = (ni * ar - nr * ai) / den
    br, bi = bre_ref[...], bim_ref[...]
    abr_ref[...] = abr
    abi_ref[...] = abi
    bbr_ref[...] = cr * br - ci * bi
    bbi_ref[...] = cr * bi + ci * br


def _s5_discretise(a_re, a_im, log_dt, b_re, b_im):
    rows = DEPTH * S5_N
    col = lambda a: a.reshape(rows, 1)
    ldt = jnp.broadcast_to(log_dt[:, :, None], (DEPTH, S5_GROUPS, S5_STATE))
    outs = pl.pallas_call(
        _s5_disc_kernel,
        out_shape=(jax.ShapeDtypeStruct((rows, 1), F32), jax.ShapeDtypeStruct((rows, 1), F32),
                   jax.ShapeDtypeStruct((rows, S5_GROUP), F32), jax.ShapeDtypeStruct((rows, S5_GROUP), F32)),
        name="s5_disc",
    )(col(a_re), col(a_im), col(ldt), b_re.reshape(rows, S5_GROUP), b_im.reshape(rows, S5_GROUP))
    abr, abi, bbr, bbi = outs
    a_row = jnp.concatenate([abr.reshape(DEPTH, 1, S5_N), abi.reshape(DEPTH, 1, S5_N)], axis=-1)
    return a_row, bbr.reshape(DEPTH, S5_GROUPS, S5_STATE, S5_GROUP), bbi.reshape(DEPTH, S5_GROUPS, S5_STATE, S5_GROUP)


def _block_diag_in(b):
    gh = S5_GROUPS // 2
    on_diag = jnp.eye(gh, dtype=bool)[None, None, :, None, :, None]
    src = b.reshape(DEPTH, 2, gh, S5_STATE, S5_GROUP).transpose(0, 1, 2, 4, 3)
    m = jnp.where(on_diag, src[:, :, :, :, None, :], 0.0)
    return m.reshape(DEPTH, 2, S5_HALF, S5_HALF_N).astype(BF16)


def _block_diag_out(c):
    gh = S5_GROUPS // 2
    on_diag = jnp.eye(gh, dtype=bool)[None, None, :, None, :, None]
    src = c.reshape(DEPTH, 2, gh, S5_GROUP, S5_STATE).transpose(0, 1, 4, 2, 3)
    m = jnp.where(on_diag, src[:, :, None, :, :, :], 0.0)
    return m.reshape(DEPTH, 2, S5_HALF_N, S5_HALF).astype(BF16)


def _s5_project_in(u, xs_ref, bre_ref, bim_ref):
    ub = u.astype(BF16)
    for k in range(2):
        uk = ub[:, k * S5_HALF:(k + 1) * S5_HALF]
        xs_ref[:, k * S5_HALF_N:(k + 1) * S5_HALF_N] = _dot(uk, bre_ref[k])
        xs_ref[:, S5_N + k * S5_HALF_N:S5_N + (k + 1) * S5_HALF_N] = _dot(uk, bim_ref[k])


def _s5_project_out(xs_ref, u, cre_ref, cimn_ref, d_ref, wglu_ref):
    ys = []
    for k in range(2):
        xr = xs_ref[:, k * S5_HALF_N:(k + 1) * S5_HALF_N].astype(BF16)
        xi = xs_ref[:, S5_N + k * S5_HALF_N:S5_N + (k + 1) * S5_HALF_N].astype(BF16)
        ys.append(_dot(xr, cre_ref[k]) + _dot(xi, cimn_ref[k]))
    y = jnp.concatenate(ys, axis=1) + d_ref[...] * u
    y = jax.nn.gelu(y)
    return y * jax.nn.sigmoid(_dot(y.astype(BF16), wglu_ref[...]))


def _s5_scan_step(ab_ref, xs_ref, rows, xre, xim):
    are = ab_ref[:, :S5_N]
    aim = ab_ref[:, S5_N:]
    nre = are * xre - aim * xim + xs_ref[rows, :S5_N]
    nim = are * xim + aim * xre + xs_ref[rows, S5_N:]
    xs_ref[rows, :S5_N] = nre
    xs_ref[rows, S5_N:] = nim
    return nre, nim


def _s5_step_kernel(u_ref, x0_ref, a_ref, bre_ref, bim_ref, cre_ref, cimn_ref, d_ref, wglu_ref,
                    o_ref, xl_ref, xs_scr):
    u = u_ref[...]
    _s5_project_in(u, xs_scr, bre_ref, bim_ref)
    xre, xim = _s5_scan_step(a_ref, xs_scr, slice(None), x0_ref[:, :S5_N], x0_ref[:, S5_N:])
    xl_ref[:, :S5_N] = xre
    xl_ref[:, S5_N:] = xim
    o_ref[...] = _s5_project_out(xs_scr, u, cre_ref, cimn_ref, d_ref, wglu_ref)


def _s5_prompt_kernel(u_ref, un_ref, x0_ref, a_ref, bre_ref, bim_ref, cre_ref, cimn_ref, d_ref, wglu_ref,
                      o_ref, xl_ref, xs0, xs1, ut0, ut1, st_scr, ab_scr, p_in, p_out, *, nb, tl):
    g = pl.program_id(0)
    m = nb * tl

    tg = V7X_MXU_DIM // nb
    rows_g = nb * tg

    def to_token_major(ref, first):
        pm = p_in[...]
        groups = []
        for lo_t in range(first, first + tl, tg):
            us = ref[:, lo_t:lo_t + tg, :].reshape(rows_g, MIX_WIDTH)
            hi = us.astype(BF16)
            lo = (us - hi.astype(F32)).astype(BF16)
            groups.append(_dot(pm, hi) + _dot(pm, lo))
        return jnp.concatenate(groups, axis=0)

    def store_sequence_major(out, first):
        pm = p_out[...]
        for gq in range(tl // tg):
            piece = _dot(pm, out[gq * rows_g:(gq + 1) * rows_g].astype(BF16)).astype(o_ref.dtype)
            o_ref[:, first + gq * tg:first + (gq + 1) * tg, :] = piece.reshape(nb, tg, MIX_WIDTH)

    def scan(xs, xre, xim):
        for t in range(tl):
            xre, xim = _s5_scan_step(ab_scr, xs, slice(t * nb, (t + 1) * nb), xre, xim)
        return xre, xim

    @pl.when(g == 0)
    def _():
        st_scr[...] = x0_ref[...]
        ab_scr[...] = jnp.broadcast_to(a_ref[...], ab_scr.shape)
        shift = nb.bit_length() - 1
        ro = lax.broadcasted_iota(jnp.int32, (rows_g, rows_g), 0)
        ci = lax.broadcasted_iota(jnp.int32, (rows_g, rows_g), 1)
        seq_major = lambda r: (r & (nb - 1)) * tg + (r >> shift)
        p_in[...] = jnp.where(ci == seq_major(ro), 1.0, 0.0).astype(BF16)
        p_out[...] = jnp.where(ro == seq_major(ci), 1.0, 0.0).astype(BF16)
        u0 = to_token_major(u_ref, 0)
        ut0[...] = u0
        _s5_project_in(u0, xs0, bre_ref, bim_ref)

    u1 = to_token_major(u_ref, tl)
    ut1[...] = u1
    _s5_project_in(u1, xs1, bre_ref, bim_ref)
    xre, xim = scan(xs0, st_scr[:, :S5_N], st_scr[:, S5_N:])
    store_sequence_major(_s5_project_out(xs0, ut0[...], cre_ref, cimn_ref, d_ref, wglu_ref), 0)
    xre, xim = scan(xs1, xre, xim)
    st_scr[:, :S5_N] = xre
    st_scr[:, S5_N:] = xim
    un = to_token_major(un_ref, 0)
    ut0[...] = un
    _s5_project_in(un, xs0, bre_ref, bim_ref)
    store_sequence_major(_s5_project_out(xs1, u1, cre_ref, cimn_ref, d_ref, wglu_ref), tl)
    xl_ref[...] = st_scr[...]


def _s5_weight_specs(layer):
    lay3 = lambda i: (layer, 0, 0)
    lay4 = lambda i: (layer, 0, 0, 0)
    return [_resident((None, 1, 2 * S5_N), lay3),
            _resident((None, 2, S5_HALF, S5_HALF_N), lay4),
            _resident((None, 2, S5_HALF, S5_HALF_N), lay4),
            _resident((None, 2, S5_HALF_N, S5_HALF), lay4),
            _resident((None, 2, S5_HALF_N, S5_HALF), lay4),
            _resident((None, 1, MIX_WIDTH), lay3),
            _resident((None, MIX_WIDTH, MIX_WIDTH), lay3)]


S5_WEIGHT_BYTES = 4 * S5_HALF * S5_HALF_N * 2 * 2 + MIX_WIDTH * MIX_WIDTH * 2 + (2 * S5_N + MIX_WIDTH) * 4


def _s5_step(u, x0, s5w, layer):
    nb = u.shape[0]
    blocks = 4 * nb * MIX_WIDTH * 4 + 5 * nb * 2 * S5_N * 4 + S5_WEIGHT_BYTES + 6 * nb * MIX_WIDTH * 4
    whole = lambda w: pl.BlockSpec((nb, w), lambda i: (0, 0))
    return pl.pallas_call(
        _s5_step_kernel,
        out_shape=(jax.ShapeDtypeStruct(u.shape, F32), jax.ShapeDtypeStruct((nb, 2 * S5_N), F32)),
        grid=(1,),
        in_specs=[whole(MIX_WIDTH), pl.BlockSpec((None, nb, 2 * S5_N), lambda i: (layer, 0, 0))] + _s5_weight_specs(layer),
        out_specs=(whole(MIX_WIDTH), whole(2 * S5_N)),
        scratch_shapes=[pltpu.VMEM((nb, 2 * S5_N), F32)],
        compiler_params=_params(blocks, 1),
        name="s5_step",
    )(u, x0, *s5w)


def _s5_prompt(u, x0, s5w, layer):
    nb, seq, _ = u.shape
    tl = S5_TOKENS
    m = nb * tl
    steps = seq // (2 * tl)
    perm_rows = V7X_MXU_DIM
    assert nb & (nb - 1) == 0 and seq % (2 * tl) == 0 and m % perm_rows == 0
    blocks = (2 * 2 * m * MIX_WIDTH * 4 + 2 * m * MIX_WIDTH * 4 + 2 * 2 * m * MIX_WIDTH * 2 + 2 * m * 2 * S5_N * 4
              + 2 * m * MIX_WIDTH * 4 + 6 * nb * 2 * S5_N * 4 + S5_WEIGHT_BYTES + 2 * perm_rows * perm_rows * 2
              + m * 2 * S5_N * 2 + 6 * m * MIX_WIDTH * 4)
    pair = pl.BlockSpec((nb, 2 * tl, MIX_WIDTH), lambda i: (0, i, 0))
    nxt = pl.BlockSpec((nb, tl, MIX_WIDTH), lambda i: (0, jnp.minimum(2 * i + 2, 2 * steps - 1), 0))
    state = pl.BlockSpec((nb, 2 * S5_N), lambda i: (0, 0))
    return pl.pallas_call(
        functools.partial(_s5_prompt_kernel, nb=nb, tl=tl),
        out_shape=(jax.ShapeDtypeStruct(u.shape, BF16), jax.ShapeDtypeStruct((nb, 2 * S5_N), F32)),
        grid=(steps,),
        in_specs=[pair, nxt, _resident((nb, 2 * S5_N), lambda i: (0, 0))] + _s5_weight_specs(layer),
        out_specs=(pair, state),
        scratch_shapes=[pltpu.VMEM((m, 2 * S5_N), F32), pltpu.VMEM((m, 2 * S5_N), F32),
                        pltpu.VMEM((m, MIX_WIDTH), F32), pltpu.VMEM((m, MIX_WIDTH), F32),
                        pltpu.VMEM((nb, 2 * S5_N), F32), pltpu.VMEM((nb, 2 * S5_N), F32),
                        pltpu.VMEM((perm_rows, perm_rows), BF16), pltpu.VMEM((perm_rows, perm_rows), BF16)],
        compiler_params=_params(blocks, 1),
        name="s5",
    )(u, u, x0, *s5w)


def _ret_kernel(p_ref, cos_ref, sin_ref, o_ref, s_out_ref, s_scr, *, seq):
    c = CHUNK
    heads = range(HEADS)
    row = lax.broadcasted_iota(jnp.int32, (c, c), 0)
    col = lax.broadcasted_iota(jnp.int32, (c, c), 1)
    diff = (row - col).astype(F32)
    ivec = lax.broadcasted_iota(jnp.int32, (c, HEAD_DIM), 0).astype(F32)
    dmats = [jnp.where(diff >= 0, jnp.exp(lg * jnp.maximum(diff, 0.0)), 0.0) * QK_SCALE for lg in LOG_GAMMA]
    q_decs = [jnp.exp(lg * (ivec + 1.0)) for lg in LOG_GAMMA]
    k_decs = [jnp.exp(lg * (c - 1.0 - ivec)) * QK_SCALE for lg in LOG_GAMMA]
    s_scr[...] = jnp.zeros_like(s_scr)

    def chunk(j, carry):
        r = pl.ds(pl.multiple_of(j * c, c), c)
        cs = cos_ref[r, :]
        sn = sin_ref[r, :]
        sl = lambda part, h: p_ref[r, part * MIX_WIDTH + h * HEAD_DIM:part * MIX_WIDTH + (h + 1) * HEAD_DIM]
        q = [_rope(sl(0, h).astype(F32), cs, sn) for h in heads]
        k = [_rope(sl(1, h).astype(F32), cs, sn) for h in heads]
        v = [sl(2, h).astype(BF16) for h in heads]
        s = [s_scr[h] for h in heads]
        qb = [q[h].astype(BF16) for h in heads]
        kb = [k[h].astype(BF16) for h in heads]
        att = [_dot_nt(qb[h], kb[h]) for h in heads]
        inter = [_dot((q[h] * q_decs[h]).astype(BF16), s[h].astype(BF16)) for h in heads]
        upd = [_dot_tn((k[h] * k_decs[h]).astype(BF16), v[h]) for h in heads]
        attb = [(att[h] * dmats[h]).astype(BF16) for h in heads]
        o = [_dot(attb[h], v[h]) + inter[h] for h in heads]
        for h in heads:
            s_scr[h] = s[h] * math.exp(LOG_GAMMA[h] * c) + upd[h]
        on = [_head_norm(o[h]) for h in heads]
        for h in heads:
            g = sl(3, h).astype(F32)
            o_ref[r, h * HEAD_DIM:(h + 1) * HEAD_DIM] = (on[h] * (g * jax.nn.sigmoid(g))).astype(o_ref.dtype)
        return carry

    lax.fori_loop(0, seq // c, chunk, 0, unroll=2)
    s_out_ref[...] = s_scr[...]


def _retention_prompt(p, cos, sin, b, seq):
    half = 4 * MIX_WIDTH
    blocks = 2 * seq * half * 2 + 2 * seq * V7X_LANES * 4 + 2 * seq * MIX_WIDTH * 2 + 3 * HEADS * HEAD_DIM * HEAD_DIM * 4
    return pl.pallas_call(
        functools.partial(_ret_kernel, seq=seq),
        out_shape=(jax.ShapeDtypeStruct((b * seq, MIX_WIDTH), BF16),
                   jax.ShapeDtypeStruct((b, HEADS, HEAD_DIM, HEAD_DIM), F32)),
        grid=(b,),
        in_specs=[pl.BlockSpec((seq, half), lambda i: (i, 0)),
                  _resident((seq, HEAD_DIM), lambda i: (0, 0)),
                  _resident((seq, HEAD_DIM), lambda i: (0, 0))],
        out_specs=(pl.BlockSpec((seq, MIX_WIDTH), lambda i: (i, 0)),
                   pl.BlockSpec((None, HEADS, HEAD_DIM, HEAD_DIM), lambda i: (i, 0, 0, 0))),
        scratch_shapes=[pltpu.VMEM((HEADS, HEAD_DIM, HEAD_DIM), F32)],
        compiler_params=_params(blocks, 1),
        name="retention",
    )(p, cos, sin)


N_REP = 3 * HEADS


def _split_dot(x, w):
    hi = x.astype(BF16)
    lo = (x - hi.astype(F32)).astype(BF16)
    return _dot(hi, w) + _dot(lo, w)


def _split_dot_tn(x, w):
    hi = x.astype(BF16)
    lo = (x - hi.astype(F32)).astype(BF16)
    return _dot_tn(hi, w) + _dot_tn(lo, w)


def _mlstm_kernel(p_ref, gif_ref, bias_ref, o_ref, c_out_ref, n_out_ref, m_out_ref,
                  cn_scr, m_scr, rows_scr, rep_scr, *, seq):
    c = CHUNK
    nch = seq // c
    nrow = 8 * nch
    heads = range(HEADS)
    neg_inf = float("-inf")
    row = lax.broadcasted_iota(jnp.int32, (c, c), 0)
    col = lax.broadcasted_iota(jnp.int32, (c, c), 1)
    causal = row >= col
    triu = jnp.where(row <= col, 1.0, 0.0).astype(BF16)
    ones = jnp.ones((c, V7X_LANES), BF16)
    er = lax.broadcasted_iota(jnp.int32, (16, N_REP * V7X_LANES), 0)
    ec = lax.broadcasted_iota(jnp.int32, (16, N_REP * V7X_LANES), 1)
    expand = jnp.where(ec // V7X_LANES == er, 1.0, 0.0).astype(BF16)
    cn_scr[...] = jnp.zeros_like(cn_scr)
    m_scr[...] = jnp.zeros_like(m_scr)

    pieces = []
    sub8 = lax.broadcasted_iota(jnp.int32, (8, c), 0)
    for j in range(nch):
        ft = gif_ref[j * c:(j + 1) * c, :].T[0:8, :] + bias_ref[...]
        pieces.append(jnp.where(sub8 < HEADS, ft, jax.nn.log_sigmoid(ft)))
    x = jnp.concatenate(pieces, axis=0)
    cum = _split_dot(x, triu)
    sub = lax.broadcasted_iota(jnp.int32, (nrow, c), 0) % 8
    tok = lax.broadcasted_iota(jnp.int32, (nrow, c), 1)
    a = x - pltpu.roll(cum, nrow - HEADS, 0)
    d = 1
    while d < c:
        a = jnp.maximum(a, jnp.where(tok >= d, pltpu.roll(a, d, 1), neg_inf))
        d *= 2
    rows_scr[0] = jnp.where(sub < HEADS, x, cum)
    rows_scr[1] = a

    def replicate(j):
        r8 = pl.ds(pl.multiple_of(j * 8, 8), 8)
        r16 = jnp.concatenate([rows_scr[0, r8, :], rows_scr[1, r8, :]], axis=0)
        return _split_dot_tn(r16, expand)

    rep_scr[0] = replicate(0)

    def chunk(j, carry):
        r = pl.ds(pl.multiple_of(j * c, c), c)
        r8 = pl.ds(pl.multiple_of(j * 8, 8), 8)
        slot = j % 2
        rep_next = replicate(jnp.minimum(j + 1, nch - 1))
        sl = lambda part, h: p_ref[r, part * MIX_WIDTH + h * HEAD_DIM:part * MIX_WIDTH + (h + 1) * HEAD_DIM]
        blk = lambda n, h: rep_scr[slot, :, (n * HEADS + h) * V7X_LANES:(n * HEADS + h + 1) * V7X_LANES]
        y8 = rows_scr[0, r8, :]
        q = [sl(0, h).astype(BF16) for h in heads]
        k = [sl(1, h).astype(BF16) for h in heads]
        v1 = [jnp.concatenate([sl(2, h).astype(BF16), ones], axis=1) for h in heads]
        cn = [cn_scr[h] for h in heads]
        m_prev = [m_scr[h] for h in heads]
        ic = [blk(0, h) for h in heads]
        bc = [blk(1, h) for h in heads]
        cmc = [blk(2, h) for h in heads]
        qk = [_dot_nt(q[h], k[h]) for h in heads]
        qcn = [_dot(q[h], cn[h].astype(BF16)) for h in heads]
        b_last = [bc[h][c - 1:c, :] for h in heads]
        wk_log = [b_last[h] - bc[h] + ic[h] for h in heads]
        m_new = [jnp.maximum(b_last[h] + m_prev[h], jnp.max(wk_log[h], axis=0, keepdims=True)) for h in heads]
        wk = [jnp.exp(wk_log[h] - m_new[h]) * QK_SCALE for h in heads]
        decay = [jnp.exp(b_last[h] + m_prev[h] - m_new[h]) for h in heads]
        kw = [(k[h].astype(F32) * wk[h]).astype(BF16) for h in heads]
        upd = [_dot_tn(kw[h], v1[h]) for h in heads]
        m_t = [bc[h] + jnp.maximum(m_prev[h], cmc[h]) for h in heads]
        dlog = [jnp.where(causal, bc[h] - y8[HEADS + h:HEADS + h + 1, :] + y8[h:h + 1, :], neg_inf) for h in heads]
        w_intra = [jnp.exp(dlog[h] - m_t[h]) * QK_SCALE for h in heads]
        w_inter = [jnp.exp(bc[h] + m_prev[h] - m_t[h]) for h in heads]
        s = [(qk[h] * w_intra[h]).astype(BF16) for h in heads]
        sv = [_dot(s[h], v1[h]) for h in heads]
        hh = []
        for h in heads:
            num = sv[h][:, :HEAD_DIM] + w_inter[h] * qcn[h][:, :HEAD_DIM]
            den = sv[h][:, HEAD_DIM:] + w_inter[h] * qcn[h][:, HEAD_DIM:]
            hh.append(num / jnp.maximum(jnp.abs(den), jnp.exp(-m_t[h])))
        for h in heads:
            cn_scr[h] = jnp.concatenate([decay[h], decay[h]], axis=1) * cn[h] + upd[h]
            m_scr[h] = m_new[h]
        on = [_head_norm(hh[h]) for h in heads]
        for h in heads:
            o_ref[r, h * HEAD_DIM:(h + 1) * HEAD_DIM] = (on[h] * jax.nn.sigmoid(sl(3, h).astype(F32))).astype(o_ref.dtype)
        rep_scr[1 - slot] = rep_next
        return carry

    lax.fori_loop(0, nch, chunk, 0, unroll=2)
    for h in heads:
        cnh = cn_scr[h]
        c_out_ref[h] = cnh[:, :HEAD_DIM]
        n_out_ref[h:h + 1, :] = cnh[:, HEAD_DIM:].T[0:1, :]
        m_out_ref[h:h + 1, :] = m_scr[h]


def _mlstm_prompt(p, gif, bias, layer, b, seq):
    half = 4 * MIX_WIDTH
    nch = seq // CHUNK
    blocks = (2 * seq * half * 2 + 2 * seq * N_GIF * 4 + 2 * seq * MIX_WIDTH * 2
              + 5 * HEADS * HEAD_DIM * 2 * HEAD_DIM * 4 + 4 * CHUNK * N_REP * V7X_LANES * 4)
    return pl.pallas_call(
        functools.partial(_mlstm_kernel, seq=seq),
        out_shape=(jax.ShapeDtypeStruct((b * seq, MIX_WIDTH), BF16),
                   jax.ShapeDtypeStruct((b, HEADS, HEAD_DIM, HEAD_DIM), F32),
                   jax.ShapeDtypeStruct((b, HEADS, HEAD_DIM), F32),
                   jax.ShapeDtypeStruct((b, HEADS, V7X_LANES), F32)),
        grid=(b,),
        in_specs=[pl.BlockSpec((seq, half), lambda i: (i, 1)),
                  pl.BlockSpec((seq, N_GIF), lambda i: (i, 0)),
                  _resident((None, 2 * HEADS, CHUNK), lambda i: (layer, 0, 0))],
        out_specs=(pl.BlockSpec((seq, MIX_WIDTH), lambda i: (i, 0)),
                   pl.BlockSpec((None, HEADS, HEAD_DIM, HEAD_DIM), lambda i: (i, 0, 0, 0)),
                   pl.BlockSpec((None, HEADS, HEAD_DIM), lambda i: (i, 0, 0)),
                   pl.BlockSpec((None, HEADS, V7X_LANES), lambda i: (i, 0, 0))),
        scratch_shapes=[pltpu.VMEM((HEADS, HEAD_DIM, 2 * HEAD_DIM), F32),
                        pltpu.VMEM((HEADS, 1, V7X_LANES), F32),
                        pltpu.VMEM((2, 8 * nch, CHUNK), F32),
                        pltpu.VMEM((2, CHUNK, N_REP * V7X_LANES), F32)],
        compiler_params=_params(blocks, 1),
        name="mlstm",
    )(p, gif, bias)


SEQ_BLOCK_FIRST = 8
SEQ_BLOCK = 16


def _column_expander(sb):
    er = lax.broadcasted_iota(jnp.int32, (sb, sb * V7X_LANES), 0)
    ec = lax.broadcasted_iota(jnp.int32, (sb, sb * V7X_LANES), 1)
    return jnp.where(ec // V7X_LANES == er, 1.0, 0.0).astype(BF16)


def _columns(x, expander):
    return _dot_tn(x.astype(BF16), expander)


def _ret_step_kernel(p_ref, cos_ref, sin_ref, s_ref, *rest, layer, first):
    o_ref, s_out_ref, row_scr = rest[-3:]
    s_out_ref = _own_layer(s_out_ref, layer, first)
    sb = p_ref.shape[0]
    cs = cos_ref[...]
    sn = sin_ref[...]
    expander = _column_expander(sb)
    for h in range(HEADS):
        lo = h * HEAD_DIM
        gamma = math.exp(LOG_GAMMA[h])
        q = _rope(p_ref[:, lo:lo + HEAD_DIM], cs, sn)
        k = _rope(p_ref[:, MIX_WIDTH + lo:MIX_WIDTH + lo + HEAD_DIM], cs, sn) * QK_SCALE
        v = p_ref[:, 2 * MIX_WIDTH + lo:2 * MIX_WIDTH + lo + HEAD_DIM]
        g = p_ref[:, 3 * MIX_WIDTH + lo:3 * MIX_WIDTH + lo + HEAD_DIM]
        qt = _columns(q * gamma, expander)
        kt = _columns(k, expander)
        for b in range(sb):
            s = s_ref[b, h]
            blk = slice(b * V7X_LANES, (b + 1) * V7X_LANES)
            row_scr[b:b + 1, :] = jnp.sum(qt[:, blk] * s, axis=0, keepdims=True)
            s_out_ref[b, h] = gamma * s + kt[:, blk] * v[b:b + 1, :]
        o = jnp.sum(q * k, axis=-1, keepdims=True) * v + row_scr[...]
        o_ref[:, lo:lo + HEAD_DIM] = _head_norm(o) * (g * jax.nn.sigmoid(g))


def _stacked_state(acc, layer, sb):
    tail = (HEADS, HEAD_DIM, HEAD_DIM)
    if acc is None:
        return (), [], pl.BlockSpec((DEPTH, sb) + tail, lambda i: (0, i, 0, 0, 0))
    return (acc,), [pl.BlockSpec(memory_space=pl.ANY)], pl.BlockSpec((None, sb) + tail, lambda i: (layer, i, 0, 0, 0))


def _own_layer(state_out_ref, layer, first):
    if not first:
        return state_out_ref
    for other in range(DEPTH):
        if other != layer:
            state_out_ref[other] = jnp.zeros(state_out_ref.shape[1:], F32)
    return state_out_ref.at[layer]


def _retention_step(p, cos, sin, state, layer, acc):
    nseq = p.shape[0]
    sb = SEQ_BLOCK_FIRST if acc is None else SEQ_BLOCK
    half = 4 * MIX_WIDTH
    sblk = sb * HEADS * HEAD_DIM * HEAD_DIM * 4
    blocks = (2 + 2 * (DEPTH if acc is None else 1)) * sblk + 2 * sb * half * 4 + 2 * sb * MIX_WIDTH * 4
    state_blk = pl.BlockSpec((None, sb, HEADS, HEAD_DIM, HEAD_DIM), lambda i: (layer, i, 0, 0, 0))
    extra, extra_specs, out_state_blk = _stacked_state(acc, layer, sb)
    return pl.pallas_call(
        functools.partial(_ret_step_kernel, layer=layer, first=acc is None),
        out_shape=(jax.ShapeDtypeStruct((nseq, MIX_WIDTH), F32),
                   jax.ShapeDtypeStruct(state.shape, F32)),
        grid=(nseq // sb,),
        in_specs=[pl.BlockSpec((sb, half), lambda i: (i, 0)),
                  _resident((1, HEAD_DIM), lambda i: (0, 0)),
                  _resident((1, HEAD_DIM), lambda i: (0, 0)),
                  state_blk] + extra_specs,
        out_specs=(pl.BlockSpec((sb, MIX_WIDTH), lambda i: (i, 0)), out_state_blk),
        input_output_aliases={4: 1} if extra else {},
        scratch_shapes=[pltpu.VMEM((sb, HEAD_DIM), F32)],
        compiler_params=_params(blocks, 1),
        name="retention_step",
    )(p, cos, sin, state, *extra)


def _mlstm_step_kernel(p_ref, gif_ref, bias_ref, c_ref, n_ref, m_ref, *rest, layer, first):
    o_ref, c_out_ref, n_out_ref, m_out_ref, row_scr = rest[-5:]
    c_out_ref = _own_layer(c_out_ref, layer, first)
    sb = p_ref.shape[0]
    fb = gif_ref[...] + bias_ref[...]
    lf_all = jax.nn.log_sigmoid(fb)
    lane = lax.broadcasted_iota(jnp.int32, (sb, V7X_LANES), 1)
    m_all = m_ref[...]
    m_acc = jnp.zeros((sb, V7X_LANES), F32)
    expander = _column_expander(sb)
    for h in range(HEADS):
        lo = h * HEAD_DIM
        q = p_ref[:, lo:lo + HEAD_DIM]
        k = p_ref[:, MIX_WIDTH + lo:MIX_WIDTH + lo + HEAD_DIM] * QK_SCALE
        v = p_ref[:, 2 * MIX_WIDTH + lo:2 * MIX_WIDTH + lo + HEAD_DIM]
        og = p_ref[:, 3 * MIX_WIDTH + lo:3 * MIX_WIDTH + lo + HEAD_DIM]
        ig = fb[:, h:h + 1]
        inter = lf_all[:, HEADS + h:HEADS + h + 1] + m_all[:, h:h + 1]
        m_t = jnp.maximum(inter, ig)
        w_intra = jnp.exp(ig - m_t)
        w_inter = jnp.exp(inter - m_t)
        nv = n_ref[:, lo:lo + HEAD_DIM]
        kw = k * w_intra
        qt = _columns(q, expander)
        kt = _columns(kw, expander)
        for b in range(sb):
            cm = c_ref[b, h]
            blk = slice(b * V7X_LANES, (b + 1) * V7X_LANES)
            row_scr[b:b + 1, :] = jnp.sum(qt[:, blk] * cm, axis=0, keepdims=True)
            c_out_ref[b, h] = w_inter[b:b + 1, :] * cm + kt[:, blk] * v[b:b + 1, :]
        s = jnp.sum(q * k, axis=-1, keepdims=True) * w_intra
        num = s * v + w_inter * row_scr[...]
        den = s + w_inter * jnp.sum(q * nv, axis=-1, keepdims=True)
        hh = num / jnp.maximum(jnp.abs(den), jnp.exp(-m_t))
        n_out_ref[:, lo:lo + HEAD_DIM] = w_inter * nv + kw
        m_acc = jnp.where(lane == h, m_t, m_acc)
        o_ref[:, lo:lo + HEAD_DIM] = _head_norm(hh) * jax.nn.sigmoid(og)
    m_out_ref[...] = m_acc


def _mlstm_step(p, gif, bias, c_state, n_state, m_state, layer, acc):
    nseq = p.shape[0]
    sb = SEQ_BLOCK_FIRST if acc is None else SEQ_BLOCK
    half = 4 * MIX_WIDTH
    sblk = sb * HEADS * HEAD_DIM * HEAD_DIM * 4
    blocks = (2 + 2 * (DEPTH if acc is None else 1)) * sblk + 2 * sb * half * 4 + 8 * sb * MIX_WIDTH * 4
    rows = lambda w: pl.BlockSpec((sb, w), lambda i: (i, 0))
    state_blk = pl.BlockSpec((None, sb, HEADS, HEAD_DIM, HEAD_DIM), lambda i: (layer, i, 0, 0, 0))
    extra, extra_specs, out_state_blk = _stacked_state(acc, layer, sb)
    return pl.pallas_call(
        functools.partial(_mlstm_step_kernel, layer=layer, first=acc is None),
        out_shape=(jax.ShapeDtypeStruct((nseq, MIX_WIDTH), F32),
                   jax.ShapeDtypeStruct(c_state.shape, F32),
                   jax.ShapeDtypeStruct((nseq, MIX_WIDTH), F32),
                   jax.ShapeDtypeStruct((nseq, V7X_LANES), F32)),
        grid=(nseq // sb,),
        in_specs=[pl.BlockSpec((sb, half), lambda i: (i, 1)),
                  rows(N_GIF),
                  _resident((None, 1, N_GIF), lambda i: (layer, 0, 0)),
                  state_blk,
                  pl.BlockSpec((None, sb, MIX_WIDTH), lambda i: (layer, i, 0)),
                  pl.BlockSpec((None, sb, HEADS), lambda i: (layer, i, 0))] + extra_specs,
        out_specs=(rows(MIX_WIDTH), out_state_blk, rows(MIX_WIDTH), rows(V7X_LANES)),
        input_output_aliases={6: 1} if extra else {},
        scratch_shapes=[pltpu.VMEM((sb, HEAD_DIM), F32)],
        compiler_params=_params(blocks, 1),
        name="mlstm_step",
    )(p, gif, bias, c_state, n_state, m_state, *extra)


def _rope_tables(pos):
    inv = ROPE_THETA ** (-jnp.arange(0, HEAD_DIM, 2, dtype=F32) / HEAD_DIM)
    ang = pos.astype(F32)[:, None] * inv[None, :]
    cos, sin = jnp.cos(ang), jnp.sin(ang)
    return jnp.concatenate([cos, cos], axis=-1), jnp.concatenate([-sin, sin], axis=-1)


W_IN_BLOCKS = N_IN_PADDED // W_IN_ROWS


def _w_in_source_row(j):
    o = j * W_IN_ROWS
    off_gif = N_MAIN
    off_su = off_gif + 2 * HEADS
    off_gates = off_su + MIX_WIDTH
    return jnp.where(o < OFF_GATES, o,
                     jnp.where(o < OFF_SU, o - OFF_GATES + off_gates,
                               jnp.where(o < OFF_GIF, o - OFF_SU + off_su, off_gif)))


def _w_in_layout_kernel(w_hbm, o_ref, buf, sem):
    l, j = pl.program_id(0), pl.program_id(1)
    step = l * W_IN_BLOCKS + j
    total = pl.num_programs(0) * W_IN_BLOCKS

    def read(s):
        src = pl.multiple_of(_w_in_source_row(s % W_IN_BLOCKS), 8)
        return pltpu.make_async_copy(w_hbm.at[s // W_IN_BLOCKS, pl.ds(src, W_IN_ROWS), :], buf.at[s % 2], sem.at[s % 2])

    @pl.when(step == 0)
    def _():
        read(step).start()

    @pl.when(step + 1 < total)
    def _():
        read(step + 1).start()

    read(step).wait()
    x = buf[step % 2]
    row = lax.broadcasted_iota(jnp.int32, x.shape, 0)
    keep = jnp.logical_or(j < W_IN_BLOCKS - 1, row < 2 * HEADS)
    o_ref[...] = jnp.where(keep, x, 0.0).astype(BF16)


def _w_in_layout(w_in):
    depth = w_in.shape[0]
    w_t = jnp.swapaxes(w_in, 1, 2)
    blocks = 2 * W_IN_ROWS * D_MODEL * 4 + 2 * W_IN_ROWS * D_MODEL * 2 + 2 * W_IN_ROWS * D_MODEL * 4
    return pl.pallas_call(
        _w_in_layout_kernel,
        out_shape=jax.ShapeDtypeStruct((depth, N_IN_PADDED, D_MODEL), BF16),
        grid=(depth, W_IN_BLOCKS),
        in_specs=[pl.BlockSpec(memory_space=pl.ANY)],
        out_specs=pl.BlockSpec((None, W_IN_ROWS, D_MODEL), lambda l, j: (l, j, 0)),
        scratch_shapes=[pltpu.VMEM((2, W_IN_ROWS, D_MODEL), F32), pltpu.SemaphoreType.DMA((2,))],
        compiler_params=_params(blocks, 2),
        name="w_in_layout",
    )(w_t)


def _prepare_weights(w_ffn1_gu, w_ffn1_down, w_in, w_s5_glu, w_branch, w_out, w_ffn2_gu, w_ffn2_down):
    cast = lambda w: w.astype(BF16)
    return dict(gu1=cast(w_ffn1_gu), d1=cast(w_ffn1_down), w_in=_w_in_layout(w_in), glu=cast(w_s5_glu),
                branch=cast(w_branch), out=cast(w_out), gu2=cast(w_ffn2_gu), d2=cast(w_ffn2_down))


def _run_trunk(x, nseq, seq, states, wts, norms, s5w, bias, ln_final, prompt):
    t = nseq * seq
    tm = 512 if t % 512 == 0 else t
    act_dtype = BF16 if prompt else F32
    if prompt:
        cos, sin = _rope_tables(jnp.arange(seq, dtype=jnp.int32))
    else:
        cos, sin = _rope_tables(PAST_LEN + jnp.arange(seq, dtype=jnp.int32))
    if prompt:
        x0 = jnp.zeros((nseq, 2 * S5_N), F32)
    else:
        st_ret, st_mc, st_mn, st_mm, st_sre, st_sim = states
        st_mn = st_mn.reshape(DEPTH, nseq, MIX_WIDTH)
        x0 = jnp.concatenate([st_sre.reshape(DEPTH, nseq, S5_N), st_sim.reshape(DEPTH, nseq, S5_N)], axis=-1)
    rets, mcs, mns, mms, xls = [], [], [], [], []
    ret_acc = mc_acc = None
    for l in range(DEPTH):
        x = _ffn(x, norms["ffn1"], wts["gu1"], wts["d1"], ln_final, l, tm, final=False)
        main, gates, su, gif = _inproj(x, norms["mix"], wts["w_in"], l, tm, act_dtype)
        if prompt:
            r_out, new_ret = _retention_prompt(main, cos, sin, nseq, seq)
            m_out, new_mc, new_mn, new_mm = _mlstm_prompt(main, gif, bias, l, nseq, seq)
            s_out, x_last = _s5_prompt(su.reshape(nseq, seq, MIX_WIDTH), x0, s5w, l)
            rets.append(new_ret)
            mcs.append(new_mc)
        else:
            r_out, ret_acc = _retention_step(main, cos, sin, st_ret, l, ret_acc)
            m_out, mc_acc, new_mn, new_mm = _mlstm_step(main, gif, bias, st_mc, st_mn, st_mm, l, mc_acc)
            s_out, x_last = _s5_step(su, x0, s5w, l)
        mns.append(new_mn)
        mms.append(new_mm)
        xls.append(x_last)
        x = _merge_ffn(x, r_out, m_out, s_out, gates, wts["branch"], wts["out"], norms["ffn2"], wts["gu2"], wts["d2"],
                       ln_final, l, tm, final=(l == DEPTH - 1))
    new_ret, new_mc = (jnp.stack(rets), jnp.stack(mcs)) if prompt else (ret_acc, mc_acc)
    new_mn = jnp.stack(mns).reshape(DEPTH, nseq, HEADS, HEAD_DIM)
    new_mm = jnp.stack(mms)[:, :, :, 0] if prompt else jnp.stack(mms)[:, :, :HEADS]
    xl = jnp.stack(xls)
    new_sre = xl[:, :, :S5_N].reshape(DEPTH, nseq, S5_GROUPS, S5_STATE)
    new_sim = xl[:, :, S5_N:].reshape(DEPTH, nseq, S5_GROUPS, S5_STATE)
    return x, (new_ret, new_mc, new_mn, new_mm, new_sre, new_sim)


def kernel(x_prompt, x_sample, state_ret, state_mlstm_c, state_mlstm_n, state_mlstm_m, state_s5_re, state_s5_im, ln_ffn1, w_ffn1_gu, w_ffn1_down, ln_mix, w_in, b_gates, s5_a_re, s5_a_im, s5_b_re, s5_b_im, s5_c_re, s5_c_im, s5_d, s5_log_dt, w_s5_glu, w_branch, w_out, ln_ffn2, w_ffn2_gu, w_ffn2_down, ln_final):
    wts = _prepare_weights(w_ffn1_gu, w_ffn1_down, w_in, w_s5_glu, w_branch, w_out, w_ffn2_gu, w_ffn2_down)
    norms = dict(ffn1=ln_ffn1.reshape(DEPTH, 1, D_MODEL), mix=ln_mix.reshape(DEPTH, 1, D_MODEL),
                 ffn2=ln_ffn2.reshape(DEPTH, 1, D_MODEL))
    lnf = ln_final.reshape(1, D_MODEL)
    a_row, bbr, bbi = _s5_discretise(s5_a_re, s5_a_im, s5_log_dt, s5_b_re, s5_b_im)
    s5w = (a_row, _block_diag_in(bbr), _block_diag_in(bbi), _block_diag_out(s5_c_re), _block_diag_out(-s5_c_im),
           s5_d.reshape(DEPTH, 1, MIX_WIDTH), wts["glu"])
    bias = jnp.pad(b_gates, ((0, 0), (0, N_GIF - 2 * HEADS))).reshape(DEPTH, 1, N_GIF)
    bias_rows = jnp.broadcast_to(b_gates[:, :, None], (DEPTH, 2 * HEADS, CHUNK))

    pb, pl_len, _ = x_prompt.shape
    sb, sl, _ = x_sample.shape
    y_p, st_p = _run_trunk(x_prompt.reshape(pb * pl_len, D_MODEL), pb, pl_len, None, wts, norms, s5w, bias_rows, lnf, True)
    sample_states = (state_ret, state_mlstm_c, state_mlstm_n, state_mlstm_m, state_s5_re, state_s5_im)
    y_s, st_s = _run_trunk(x_sample.reshape(sb * sl, D_MODEL), sb, sl, sample_states, wts, norms, s5w, bias, lnf, False)
    return (y_p.reshape(pb, pl_len, D_MODEL), y_s.reshape(sb, sl, D_MODEL)) + st_p + st_s
```

```python
import functools
import math

import jax
import jax.numpy as jnp
from jax import lax
from jax.experimental import pallas as pl
from jax.experimental.pallas import tpu as pltpu

F32 = jnp.float32
BF16 = jnp.bfloat16

D_MODEL = 1024
DEPTH = 4
MIX_WIDTH = D_MODEL // 2
N_BRANCH = 3
HEADS = 4
HEAD_DIM = MIX_WIDTH // HEADS
S5_GROUP = 16
S5_GROUPS = MIX_WIDTH // S5_GROUP
S5_STATE = 64
S5_N = S5_GROUPS * S5_STATE
D_FF = 2816
CHUNK = 128
PAST_LEN = 16384
ROPE_THETA = 10000.0
EPS = 1e-6

N_MAIN = 8 * MIX_WIDTH
N_GATES = N_BRANCH * D_MODEL
N_GIF = 128
W_IN_ROWS = 512
N_IN_PADDED = N_MAIN + N_GATES + MIX_WIDTH + W_IN_ROWS
OFF_GATES = N_MAIN
OFF_SU = N_MAIN + N_GATES
OFF_GIF = OFF_SU + MIX_WIDTH

V7X_LANES = 128
V7X_MXU_DIM = 256
V7X_VMEM_BYTES = 64 * 1024 * 1024

FF_CHUNK = V7X_MXU_DIM
PROJ_CHUNK = 2 * V7X_MXU_DIM
S5_HALF = MIX_WIDTH // 2
S5_HALF_N = S5_N // 2
S5_TOKENS = 64
LOG_GAMMA = tuple(math.log1p(-(2.0 ** (-5.0 - h))) for h in range(HEADS))
QK_SCALE = HEAD_DIM ** -0.5


def _vmem_limit(block_bytes):
    want = int(block_bytes * 1.25) + (8 << 20)
    return min(want, V7X_VMEM_BYTES - (6 << 20))


def _params(block_bytes, n_grid):
    return pltpu.CompilerParams(
        dimension_semantics=("arbitrary",) * n_grid,
        vmem_limit_bytes=_vmem_limit(block_bytes))


def _resident(shape, index_map):
    return pl.BlockSpec(shape, index_map, pipeline_mode=pl.Buffered(1))


def _dot(a, b):
    return jnp.dot(a, b, preferred_element_type=F32)


def _dot_nt(a, b):
    return lax.dot_general(a, b, (((1,), (1,)), ((), ())), preferred_element_type=F32)


def _dot_tn(a, b):
    return lax.dot_general(a, b, (((0,), (0,)), ((), ())), preferred_element_type=F32)


def _rms(x, g):
    return x * lax.rsqrt(jnp.mean(x * x, axis=-1, keepdims=True) + EPS) * g


def _head_norm(x):
    mu = jnp.mean(x, axis=-1, keepdims=True)
    xc = x - mu
    var = jnp.mean(xc * xc, axis=-1, keepdims=True)
    return xc * lax.rsqrt(var + EPS)


def _rope(x, cos, sin_signed):
    return x * cos + pltpu.roll(x, HEAD_DIM // 2, 1) * sin_signed


def _ffn_tile(x, ln_ref, wgu_ref, wd_ref, lnf_ref, o_ref, acc_ref, final):
    xn = _rms(x, ln_ref[...]).astype(BF16)
    for c in range(D_FF // FF_CHUNK):
        lo = c * FF_CHUNK
        g = _dot(xn, wgu_ref[:, lo:lo + FF_CHUNK])
        u = _dot(xn, wgu_ref[:, D_FF + lo:D_FF + lo + FF_CHUNK])
        a = (g * jax.nn.sigmoid(g) * u).astype(BF16)
        d = _dot(a, wd_ref[lo:lo + FF_CHUNK, :])
        if c == 0:
            acc_ref[...] = d
        else:
            acc_ref[...] += d
    y = x + 0.5 * acc_ref[...]
    if final:
        y = _rms(y, lnf_ref[...])
    o_ref[...] = y


def _ffn_kernel(x_ref, ln_ref, wgu_ref, wd_ref, lnf_ref, o_ref, acc_ref, *, final):
    _ffn_tile(x_ref[...], ln_ref, wgu_ref, wd_ref, lnf_ref, o_ref, acc_ref, final)


def _ffn(x, ln, wgu, wd, lnf, layer, tm, final):
    t = x.shape[0]
    blocks = (4 * tm * D_MODEL * 4 + tm * D_MODEL * 4 + D_MODEL * 2 * D_FF * 2 + D_FF * D_MODEL * 2
              + 4 * tm * FF_CHUNK * 4)
    return pl.pallas_call(
        functools.partial(_ffn_kernel, final=final),
        out_shape=jax.ShapeDtypeStruct((t, D_MODEL), F32),
        grid=(t // tm,),
        in_specs=[
            pl.BlockSpec((tm, D_MODEL), lambda i: (i, 0)),
            _resident((None, 1, D_MODEL), lambda i: (layer, 0, 0)),
            _resident((None, D_MODEL, 2 * D_FF), lambda i: (layer, 0, 0)),
            _resident((None, D_FF, D_MODEL), lambda i: (layer, 0, 0)),
            _resident((1, D_MODEL), lambda i: (0, 0)),
        ],
        out_specs=pl.BlockSpec((tm, D_MODEL), lambda i: (i, 0)),
        scratch_shapes=[pltpu.VMEM((tm, D_MODEL), F32)],
        compiler_params=_params(blocks, 1),
        name="ffn",
    )(x, ln, wgu, wd, lnf)


def _inproj_kernel(x_ref, ln_ref, w_ref, main_ref, gates_ref, su_ref, gif_ref):
    xn = _rms(x_ref[...], ln_ref[...]).astype(BF16)
    for c in range(N_MAIN // PROJ_CHUNK):
        lo = c * PROJ_CHUNK
        main_ref[:, lo:lo + PROJ_CHUNK] = _dot_nt(xn, w_ref[lo:lo + PROJ_CHUNK, :]).astype(main_ref.dtype)
    for c in range(N_GATES // PROJ_CHUNK):
        lo = c * PROJ_CHUNK
        gates_ref[:, lo:lo + PROJ_CHUNK] = _dot_nt(
            xn, w_ref[OFF_GATES + lo:OFF_GATES + lo + PROJ_CHUNK, :]).astype(gates_ref.dtype)
    su_ref[...] = _dot_nt(xn, w_ref[OFF_SU:OFF_SU + MIX_WIDTH, :])
    gif_ref[...] = _dot_nt(xn, w_ref[OFF_GIF:OFF_GIF + N_GIF, :])


def _inproj(x, ln, w, layer, tm, act_dtype):
    t = x.shape[0]
    ab = jnp.dtype(act_dtype).itemsize
    blocks = (2 * tm * D_MODEL * 4 + D_MODEL * N_IN_PADDED * 2
              + 2 * tm * (N_MAIN + N_GATES) * ab + 2 * tm * (MIX_WIDTH + N_GIF) * 4
              + 4 * tm * PROJ_CHUNK * 4)
    return pl.pallas_call(
        _inproj_kernel,
        out_shape=(jax.ShapeDtypeStruct((t, N_MAIN), act_dtype),
                   jax.ShapeDtypeStruct((t, N_GATES), act_dtype),
                   jax.ShapeDtypeStruct((t, MIX_WIDTH), F32),
                   jax.ShapeDtypeStruct((t, N_GIF), F32)),
        grid=(t // tm,),
        in_specs=[
            pl.BlockSpec((tm, D_MODEL), lambda i: (i, 0)),
            _resident((None, 1, D_MODEL), lambda i: (layer, 0, 0)),
            _resident((None, N_IN_PADDED, D_MODEL), lambda i: (layer, 0, 0)),
        ],
        out_specs=(pl.BlockSpec((tm, N_MAIN), lambda i: (i, 0)),
                   pl.BlockSpec((tm, N_GATES), lambda i: (i, 0)),
                   pl.BlockSpec((tm, MIX_WIDTH), lambda i: (i, 0)),
                   pl.BlockSpec((tm, N_GIF), lambda i: (i, 0))),
        compiler_params=_params(blocks, 1),
        name="inproj",
    )(x, ln, w)


def _merge_ffn_kernel(x_ref, r_ref, m_ref, s_ref, g_ref, wb_ref, wo_ref, ln_ref, wgu_ref, wd_ref, lnf_ref,
                      o_ref, acc_ref, *, final):
    merged = None
    for n, b_ref in enumerate((r_ref, m_ref, s_ref)):
        up = _dot(b_ref[...].astype(BF16), wb_ref[n])
        gate = jax.nn.sigmoid(g_ref[:, n * D_MODEL:(n + 1) * D_MODEL].astype(F32))
        merged = gate * up if merged is None else merged + gate * up
    x1 = x_ref[...] + _dot(merged.astype(BF16), wo_ref[...])
    _ffn_tile(x1, ln_ref, wgu_ref, wd_ref, lnf_ref, o_ref, acc_ref, final)


def _merge_ffn(x, r, m, s, gates, wb, wo, ln, wgu, wd, lnf, layer, tm, final):
    t = x.shape[0]
    blocks = (4 * tm * D_MODEL * 4 + 6 * tm * MIX_WIDTH * 4 + 2 * tm * N_GATES * 4
              + N_BRANCH * MIX_WIDTH * D_MODEL * 2 + D_MODEL * D_MODEL * 2 + 4 * tm * D_MODEL * 4
              + tm * D_MODEL * 4 + D_MODEL * 2 * D_FF * 2 + D_FF * D_MODEL * 2 + 4 * tm * FF_CHUNK * 4)
    row = lambda w: pl.BlockSpec((tm, w), lambda i: (i, 0))
    if s.ndim == 3:
        assert s.shape[1] % tm == 0
        per_seq = s.shape[1] // tm
        s_spec = pl.BlockSpec((None, tm, MIX_WIDTH), lambda i: (i // per_seq, i % per_seq, 0))
    else:
        s_spec = row(MIX_WIDTH)
    return pl.pallas_call(
        functools.partial(_merge_ffn_kernel, final=final),
        out_shape=jax.ShapeDtypeStruct((t, D_MODEL), F32),
        grid=(t // tm,),
        in_specs=[row(D_MODEL), row(MIX_WIDTH), row(MIX_WIDTH), s_spec, row(N_GATES),
                  _resident((None, N_BRANCH, MIX_WIDTH, D_MODEL), lambda i: (layer, 0, 0, 0)),
                  _resident((None, D_MODEL, D_MODEL), lambda i: (layer, 0, 0)),
                  _resident((None, 1, D_MODEL), lambda i: (layer, 0, 0)),
                  _resident((None, D_MODEL, 2 * D_FF), lambda i: (layer, 0, 0)),
                  _resident((None, D_FF, D_MODEL), lambda i: (layer, 0, 0)),
                  _resident((1, D_MODEL), lambda i: (0, 0))],
        out_specs=row(D_MODEL),
        scratch_shapes=[pltpu.VMEM((tm, D_MODEL), F32)],
        compiler_params=_params(blocks, 1),
        name="merge_ffn",
    )(x, r, m, s, gates, wb, wo, ln, wgu, wd, lnf)


def _s5_disc_kernel(are_ref, aim_ref, ldt_ref, bre_ref, bim_ref, abr_ref, abi_ref, bbr_ref, bbi_ref):
    ar, ai = are_ref[...], aim_ref[...]
    dt = jnp.exp(ldt_ref[...])
    mag = jnp.exp(ar * dt)
    abr = mag * jnp.cos(ai * dt)
    abi = mag * jnp.sin(ai * dt)
    nr, ni = abr - 1.0, abi
    den = ar * ar + ai * ai
    cr = (nr * ar + ni * ai) / den
    ci = (ni * ar - nr * ai) / den
    br, bi = bre_ref[...], bim_ref[...]
    abr_ref[...] = abr
    abi_ref[...] = abi
    bbr_ref[...] = cr * br - ci * bi
    bbi_ref[...] = cr * bi + ci * br


def _s5_discretise(a_re, a_im, log_dt, b_re, b_im):
    rows = DEPTH * S5_N
    col = lambda a: a.reshape(rows, 1)
    ldt = jnp.broadcast_to(log_dt[:, :, None], (DEPTH, S5_GROUPS, S5_STATE))
    outs = pl.pallas_call(
        _s5_disc_kernel,
        out_shape=(jax.ShapeDtypeStruct((rows, 1), F32), jax.ShapeDtypeStruct((rows, 1), F32),
                   jax.ShapeDtypeStruct((rows, S5_GROUP), F32), jax.ShapeDtypeStruct((rows, S5_GROUP), F32)),
        name="s5_disc",
    )(col(a_re), col(a_im), col(ldt), b_re.reshape(rows, S5_GROUP), b_im.reshape(rows, S5_GROUP))
    abr, abi, bbr, bbi = outs
    a_row = jnp.concatenate([abr.reshape(DEPTH, 1, S5_N), abi.reshape(DEPTH, 1, S5_N)], axis=-1)
    return a_row, bbr.reshape(DEPTH, S5_GROUPS, S5_STATE, S5_GROUP), bbi.reshape(DEPTH, S5_GROUPS, S5_STATE, S5_GROUP)


def _block_diag_in(b):
    gh = S5_GROUPS // 2
    on_diag = jnp.eye(gh, dtype=bool)[None, None, :, None, :, None]
    src = b.reshape(DEPTH, 2, gh, S5_STATE, S5_GROUP).transpose(0, 1, 2, 4, 3)
    m = jnp.where(on_diag, src[:, :, :, :, None, :], 0.0)
    return m.reshape(DEPTH, 2, S5_HALF, S5_HALF_N).astype(BF16)


def _block_diag_out(c):
    gh = S5_GROUPS // 2
    on_diag = jnp.eye(gh, dtype=bool)[None, None, :, None, :, None]
    src = c.reshape(DEPTH, 2, gh, S5_GROUP, S5_STATE).transpose(0, 1, 4, 2, 3)
    m = jnp.where(on_diag, src[:, :, None, :, :, :], 0.0)
    return m.reshape(DEPTH, 2, S5_HALF_N, S5_HALF).astype(BF16)


def _s5_project_in(u, xs_ref, bre_ref, bim_ref):
    ub = u.astype(BF16)
    for k in range(2):
        uk = ub[:, k * S5_HALF:(k + 1) * S5_HALF]
        xs_ref[:, k * S5_HALF_N:(k + 1) * S5_HALF_N] = _dot(uk, bre_ref[k])
        xs_ref[:, S5_N + k * S5_HALF_N:S5_N + (k + 1) * S5_HALF_N] = _dot(uk, bim_ref[k])


def _s5_project_out(xs_ref, u, cre_ref, cimn_ref, d_ref, wglu_ref):
    ys = []
    for k in range(2):
        xr = xs_ref[:, k * S5_HALF_N:(k + 1) * S5_HALF_N].astype(BF16)
        xi = xs_ref[:, S5_N + k * S5_HALF_N:S5_N + (k + 1) * S5_HALF_N].astype(BF16)
        ys.append(_dot(xr, cre_ref[k]) + _dot(xi, cimn_ref[k]))
    y = jnp.concatenate(ys, axis=1) + d_ref[...] * u
    y = jax.nn.gelu(y)
    return y * jax.nn.sigmoid(_dot(y.astype(BF16), wglu_ref[...]))


def _s5_scan_step(ab_ref, xs_ref, rows, xre, xim):
    are = ab_ref[:, :S5_N]
    aim = ab_ref[:, S5_N:]
    nre = are * xre - aim * xim + xs_ref[rows, :S5_N]
    nim = are * xim + aim * xre + xs_ref[rows, S5_N:]
    xs_ref[rows, :S5_N] = nre
    xs_ref[rows, S5_N:] = nim
    return nre, nim


def _s5_step_kernel(u_ref, x0_ref, a_ref, bre_ref, bim_ref, cre_ref, cimn_ref, d_ref, wglu_ref,
                    o_ref, xl_ref, xs_scr):
    u = u_ref[...]
    _s5_project_in(u, xs_scr, bre_ref, bim_ref)
    xre, xim = _s5_scan_step(a_ref, xs_scr, slice(None), x0_ref[:, :S5_N], x0_ref[:, S5_N:])
    xl_ref[:, :S5_N] = xre
    xl_ref[:, S5_N:] = xim
    o_ref[...] = _s5_project_out(xs_scr, u, cre_ref, cimn_ref, d_ref, wglu_ref)


def _s5_prompt_kernel(u_ref, un_ref, x0_ref, a_ref, bre_ref, bim_ref, cre_ref, cimn_ref, d_ref, wglu_ref,
                      o_ref, xl_ref, xs0, xs1, ut0, ut1, st_scr, ab_scr, p_in, p_out, *, nb, tl):
    g = pl.program_id(0)
    m = nb * tl

    tg = V7X_MXU_DIM // nb
    rows_g = nb * tg

    def to_token_major(ref, first):
        pm = p_in[...]
        groups = []
        for lo_t in range(first, first + tl, tg):
            us = ref[:, lo_t:lo_t + tg, :].reshape(rows_g, MIX_WIDTH)
            hi = us.astype(BF16)
            lo = (us - hi.astype(F32)).astype(BF16)
            groups.append(_dot(pm, hi) + _dot(pm, lo))
        return jnp.concatenate(groups, axis=0)

    def store_sequence_major(out, first):
        pm = p_out[...]
        for gq in range(tl // tg):
            piece = _dot(pm, out[gq * rows_g:(gq + 1) * rows_g].astype(BF16)).astype(o_ref.dtype)
            o_ref[:, first + gq * tg:first + (gq + 1) * tg, :] = piece.reshape(nb, tg, MIX_WIDTH)

    def scan(xs, xre, xim):
        for t in range(tl):
            xre, xim = _s5_scan_step(ab_scr, xs, slice(t * nb, (t + 1) * nb), xre, xim)
        return xre, xim

    @pl.when(g == 0)
    def _():
        st_scr[...] = x0_ref[...]
        ab_scr[...] = jnp.broadcast_to(a_ref[...], ab_scr.shape)
        shift = nb.bit_length() - 1
        ro = lax.broadcasted_iota(jnp.int32, (rows_g, rows_g), 0)
        ci = lax.broadcasted_iota(jnp.int32, (rows_g, rows_g), 1)
        seq_major = lambda r: (r & (nb - 1)) * tg + (r >> shift)
        p_in[...] = jnp.where(ci == seq_major(ro), 1.0, 0.0).astype(BF16)
        p_out[...] = jnp.where(ro == seq_major(ci), 1.0, 0.0).astype(BF16)
        u0 = to_token_major(u_ref, 0)
        ut0[...] = u0
        _s5_project_in(u0, xs0, bre_ref, bim_ref)

    u1 = to_token_major(u_ref, tl)
    ut1[...] = u1
    _s5_project_in(u1, xs1, bre_ref, bim_ref)
    xre, xim = scan(xs0, st_scr[:, :S5_N], st_scr[:, S5_N:])
    store_sequence_major(_s5_project_out(xs0, ut0[...], cre_ref, cimn_ref, d_ref, wglu_ref), 0)
    xre, xim = scan(xs1, xre, xim)
    st_scr[:, :S5_N] = xre
    st_scr[:, S5_N:] = xim
    un = to_token_major(un_ref, 0)
    ut0[...] = un
    _s5_project_in(un, xs0, bre_ref, bim_ref)
    store_sequence_major(_s5_project_out(xs1, u1, cre_ref, cimn_ref, d_ref, wglu_ref), tl)
    xl_ref[...] = st_scr[...]


def _s5_weight_specs(layer):
    lay3 = lambda i: (layer, 0, 0)
    lay4 = lambda i: (layer, 0, 0, 0)
    return [_resident((None, 1, 2 * S5_N), lay3),
            _resident((None, 2, S5_HALF, S5_HALF_N), lay4),
            _resident((None, 2, S5_HALF, S5_HALF_N), lay4),
            _resident((None, 2, S5_HALF_N, S5_HALF), lay4),
            _resident((None, 2, S5_HALF_N, S5_HALF), lay4),
            _resident((None, 1, MIX_WIDTH), lay3),
            _resident((None, MIX_WIDTH, MIX_WIDTH), lay3)]


S5_WEIGHT_BYTES = 4 * S5_HALF * S5_HALF_N * 2 * 2 + MIX_WIDTH * MIX_WIDTH * 2 + (2 * S5_N + MIX_WIDTH) * 4


def _s5_step(u, x0, s5w, layer):
    nb = u.shape[0]
    blocks = 4 * nb * MIX_WIDTH * 4 + 5 * nb * 2 * S5_N * 4 + S5_WEIGHT_BYTES + 6 * nb * MIX_WIDTH * 4
    whole = lambda w: pl.BlockSpec((nb, w), lambda i: (0, 0))
    return pl.pallas_call(
        _s5_step_kernel,
        out_shape=(jax.ShapeDtypeStruct(u.shape, F32), jax.ShapeDtypeStruct((nb, 2 * S5_N), F32)),
        grid=(1,),
        in_specs=[whole(MIX_WIDTH), pl.BlockSpec((None, nb, 2 * S5_N), lambda i: (layer, 0, 0))] + _s5_weight_specs(layer),
        out_specs=(whole(MIX_WIDTH), whole(2 * S5_N)),
        scratch_shapes=[pltpu.VMEM((nb, 2 * S5_N), F32)],
        compiler_params=_params(blocks, 1),
        name="s5_step",
    )(u, x0, *s5w)


def _s5_prompt(u, x0, s5w, layer):
    nb, seq, _ = u.shape
    tl = S5_TOKENS
    m = nb * tl
    steps = seq // (2 * tl)
    perm_rows = V7X_MXU_DIM
    assert nb & (nb - 1) == 0 and seq % (2 * tl) == 0 and m % perm_rows == 0
    blocks = (2 * 2 * m * MIX_WIDTH * 4 + 2 * m * MIX_WIDTH * 4 + 2 * 2 * m * MIX_WIDTH * 2 + 2 * m * 2 * S5_N * 4
              + 2 * m * MIX_WIDTH * 4 + 6 * nb * 2 * S5_N * 4 + S5_WEIGHT_BYTES + 2 * perm_rows * perm_rows * 2
              + m * 2 * S5_N * 2 + 6 * m * MIX_WIDTH * 4)
    pair = pl.BlockSpec((nb, 2 * tl, MIX_WIDTH), lambda i: (0, i, 0))
    nxt = pl.BlockSpec((nb, tl, MIX_WIDTH), lambda i: (0, jnp.minimum(2 * i + 2, 2 * steps - 1), 0))
    state = pl.BlockSpec((nb, 2 * S5_N), lambda i: (0, 0))
    return pl.pallas_call(
        functools.partial(_s5_prompt_kernel, nb=nb, tl=tl),
        out_shape=(jax.ShapeDtypeStruct(u.shape, BF16), jax.ShapeDtypeStruct((nb, 2 * S5_N), F32)),
        grid=(steps,),
        in_specs=[pair, nxt, _resident((nb, 2 * S5_N), lambda i: (0, 0))] + _s5_weight_specs(layer),
        out_specs=(pair, state),
        scratch_shapes=[pltpu.VMEM((m, 2 * S5_N), F32), pltpu.VMEM((m, 2 * S5_N), F32),
                        pltpu.VMEM((m, MIX_WIDTH), F32), pltpu.VMEM((m, MIX_WIDTH), F32),
                        pltpu.VMEM((nb, 2 * S5_N), F32), pltpu.VMEM((nb, 2 * S5_N), F32),
                        pltpu.VMEM((perm_rows, perm_rows), BF16), pltpu.VMEM((perm_rows, perm_rows), BF16)],
        compiler_params=_params(blocks, 1),
        name="s5",
    )(u, u, x0, *s5w)


def _ret_kernel(p_ref, cos_ref, sin_ref, o_ref, s_out_ref, s_scr, *, seq):
    c = CHUNK
    heads = range(HEADS)
    row = lax.broadcasted_iota(jnp.int32, (c, c), 0)
    col = lax.broadcasted_iota(jnp.int32, (c, c), 1)
    diff = (row - col).astype(F32)
    ivec = lax.broadcasted_iota(jnp.int32, (c, HEAD_DIM), 0).astype(F32)
    dmats = [jnp.where(diff >= 0, jnp.exp(lg * jnp.maximum(diff, 0.0)), 0.0) * QK_SCALE for lg in LOG_GAMMA]
    q_decs = [jnp.exp(lg * (ivec + 1.0)) for lg in LOG_GAMMA]
    k_decs = [jnp.exp(lg * (c - 1.0 - ivec)) * QK_SCALE for lg in LOG_GAMMA]
    s_scr[...] = jnp.zeros_like(s_scr)

    def chunk(j, carry):
        r = pl.ds(pl.multiple_of(j * c, c), c)
        cs = cos_ref[r, :]
        sn = sin_ref[r, :]
        sl = lambda part, h: p_ref[r, part * MIX_WIDTH + h * HEAD_DIM:part * MIX_WIDTH + (h + 1) * HEAD_DIM]
        q = [_rope(sl(0, h).astype(F32), cs, sn) for h in heads]
        k = [_rope(sl(1, h).astype(F32), cs, sn) for h in heads]
        v = [sl(2, h).astype(BF16) for h in heads]
        s = [s_scr[h] for h in heads]
        qb = [q[h].astype(BF16) for h in heads]
        kb = [k[h].astype(BF16) for h in heads]
        att = [_dot_nt(qb[h], kb[h]) for h in heads]
        inter = [_dot((q[h] * q_decs[h]).astype(BF16), s[h].astype(BF16)) for h in heads]
        upd = [_dot_tn((k[h] * k_decs[h]).astype(BF16), v[h]) for h in heads]
        attb = [(att[h] * dmats[h]).astype(BF16) for h in heads]
        o = [_dot(attb[h], v[h]) + inter[h] for h in heads]
        for h in heads:
            s_scr[h] = s[h] * math.exp(LOG_GAMMA[h] * c) + upd[h]
        on = [_head_norm(o[h]) for h in heads]
        for h in heads:
            g = sl(3, h).astype(F32)
            o_ref[r, h * HEAD_DIM:(h + 1) * HEAD_DIM] = (on[h] * (g * jax.nn.sigmoid(g))).astype(o_ref.dtype)
        return carry

    lax.fori_loop(0, seq // c, chunk, 0, unroll=2)
    s_out_ref[...] = s_scr[...]


def _retention_prompt(p, cos, sin, b, seq):
    half = 4 * MIX_WIDTH
    blocks = 2 * seq * half * 2 + 2 * seq * V7X_LANES * 4 + 2 * seq * MIX_WIDTH * 2 + 3 * HEADS * HEAD_DIM * HEAD_DIM * 4
    return pl.pallas_call(
        functools.partial(_ret_kernel, seq=seq),
        out_shape=(jax.ShapeDtypeStruct((b * seq, MIX_WIDTH), BF16),
                   jax.ShapeDtypeStruct((b, HEADS, HEAD_DIM, HEAD_DIM), F32)),
        grid=(b,),
        in_specs=[pl.BlockSpec((seq, half), lambda i: (i, 0)),
                  _resident((seq, HEAD_DIM), lambda i: (0, 0)),
                  _resident((seq, HEAD_DIM), lambda i: (0, 0))],
        out_specs=(pl.BlockSpec((seq, MIX_WIDTH), lambda i: (i, 0)),
                   pl.BlockSpec((None, HEADS, HEAD_DIM, HEAD_DIM), lambda i: (i, 0, 0, 0))),
        scratch_shapes=[pltpu.VMEM((HEADS, HEAD_DIM, HEAD_DIM), F32)],
        compiler_params=_params(blocks, 1),
        name="retention",
    )(p, cos, sin)


N_REP = 3 * HEADS


def _split_dot(x, w):
    hi = x.astype(BF16)
    lo = (x - hi.astype(F32)).astype(BF16)
    return _dot(hi, w) + _dot(lo, w)


def _split_dot_tn(x, w):
    hi = x.astype(BF16)
    lo = (x - hi.astype(F32)).astype(BF16)
    return _dot_tn(hi, w) + _dot_tn(lo, w)


def _mlstm_kernel(p_ref, gif_ref, bias_ref, o_ref, c_out_ref, n_out_ref, m_out_ref,
                  cn_scr, m_scr, rows_scr, rep_scr, *, seq):
    c = CHUNK
    nch = seq // c
    nrow = 8 * nch
    heads = range(HEADS)
    neg_inf = float("-inf")
    row = lax.broadcasted_iota(jnp.int32, (c, c), 0)
    col = lax.broadcasted_iota(jnp.int32, (c, c), 1)
    causal = row >= col
    triu = jnp.where(row <= col, 1.0, 0.0).astype(BF16)
    ones = jnp.ones((c, V7X_LANES), BF16)
    er = lax.broadcasted_iota(jnp.int32, (16, N_REP * V7X_LANES), 0)
    ec = lax.broadcasted_iota(jnp.int32, (16, N_REP * V7X_LANES), 1)
    expand = jnp.where(ec // V7X_LANES == er, 1.0, 0.0).astype(BF16)
    cn_scr[...] = jnp.zeros_like(cn_scr)
    m_scr[...] = jnp.zeros_like(m_scr)

    pieces = []
    sub8 = lax.broadcasted_iota(jnp.int32, (8, c), 0)
    for j in range(nch):
        ft = gif_ref[j * c:(j + 1) * c, :].T[0:8, :] + bias_ref[...]
        pieces.append(jnp.where(sub8 < HEADS, ft, jax.nn.log_sigmoid(ft)))
    x = jnp.concatenate(pieces, axis=0)
    cum = _split_dot(x, triu)
    sub = lax.broadcasted_iota(jnp.int32, (nrow, c), 0) % 8
    tok = lax.broadcasted_iota(jnp.int32, (nrow, c), 1)
    a = x - pltpu.roll(cum, nrow - HEADS, 0)
    d = 1
    while d < c:
        a = jnp.maximum(a, jnp.where(tok >= d, pltpu.roll(a, d, 1), neg_inf))
        d *= 2
    rows_scr[0] = jnp.where(sub < HEADS, x, cum)
    rows_scr[1] = a

    def replicate(j):
        r8 = pl.ds(pl.multiple_of(j * 8, 8), 8)
        r16 = jnp.concatenate([rows_scr[0, r8, :], rows_scr[1, r8, :]], axis=0)
        return _split_dot_tn(r16, expand)

    rep_scr[0] = replicate(0)

    def chunk(j, carry):
        r = pl.ds(pl.multiple_of(j * c, c), c)
        r8 = pl.ds(pl.multiple_of(j * 8, 8), 8)
        slot = j % 2
        rep_next = replicate(jnp.minimum(j + 1, nch - 1))
        sl = lambda part, h: p_ref[r, part * MIX_WIDTH + h * HEAD_DIM:part * MIX_WIDTH + (h + 1) * HEAD_DIM]
        blk = lambda n, h: rep_scr[slot, :, (n * HEADS + h) * V7X_LANES:(n * HEADS + h + 1) * V7X_LANES]
        y8 = rows_scr[0, r8, :]
        q = [sl(0, h).astype(BF16) for h in heads]
        k = [sl(1, h).astype(BF16) for h in heads]
        v1 = [jnp.concatenate([sl(2, h).astype(BF16), ones], axis=1) for h in heads]
        cn = [cn_scr[h] for h in heads]
        m_prev = [m_scr[h] for h in heads]
        ic = [blk(0, h) for h in heads]
        bc = [blk(1, h) for h in heads]
        cmc = [blk(2, h) for h in heads]
        qk = [_dot_nt(q[h], k[h]) for h in heads]
        qcn = [_dot(q[h], cn[h].astype(BF16)) for h in heads]
        b_last = [bc[h][c - 1:c, :] for h in heads]
        wk_log = [b_last[h] - bc[h] + ic[h] for h in heads]
        m_new = [jnp.maximum(b_last[h] + m_prev[h], jnp.max(wk_log[h], axis=0, keepdims=True)) for h in heads]
        wk = [jnp.exp(wk_log[h] - m_new[h]) * QK_SCALE for h in heads]
        decay = [jnp.exp(b_last[h] + m_prev[h] - m_new[h]) for h in heads]
        kw = [(k[h].astype(F32) * wk[h]).astype(BF16) for h in heads]
        upd = [_dot_tn(kw[h], v1[h]) for h in heads]
        m_t = [bc[h] + jnp.maximum(m_prev[h], cmc[h]) for h in heads]
        dlog = [jnp.where(causal, bc[h] - y8[HEADS + h:HEADS + h + 1, :] + y8[h:h + 1, :], neg_inf) for h in heads]
        w_intra = [jnp.exp(dlog[h] - m_t[h]) * QK_SCALE for h in heads]
        w_inter = [jnp.exp(bc[h] + m_prev[h] - m_t[h]) for h in heads]
        s = [(qk[h] * w_intra[h]).astype(BF16) for h in heads]
        sv = [_dot(s[h], v1[h]) for h in heads]
        hh = []
        for h in heads:
            num = sv[h][:, :HEAD_DIM] + w_inter[h] * qcn[h][:, :HEAD_DIM]
            den = sv[h][:, HEAD_DIM:] + w_inter[h] * qcn[h][:, HEAD_DIM:]
            hh.append(num / jnp.maximum(jnp.abs(den), jnp.exp(-m_t[h])))
        for h in heads:
            cn_scr[h] = jnp.concatenate([decay[h], decay[h]], axis=1) * cn[h] + upd[h]
            m_scr[h] = m_new[h]
        on = [_head_norm(hh[h]) for h in heads]
        for h in heads:
            o_ref[r, h * HEAD_DIM:(h + 1) * HEAD_DIM] = (on[h] * jax.nn.sigmoid(sl(3, h).astype(F32))).astype(o_ref.dtype)
        rep_scr[1 - slot] = rep_next
        return carry

    lax.fori_loop(0, nch, chunk, 0, unroll=2)
    for h in heads:
        cnh = cn_scr[h]
        c_out_ref[h] = cnh[:, :HEAD_DIM]
        n_out_ref[h:h + 1, :] = cnh[:, HEAD_DIM:].T[0:1, :]
        m_out_ref[h:h + 1, :] = m_scr[h]


def _mlstm_prompt(p, gif, bias, layer, b, seq):
    half = 4 * MIX_WIDTH
    nch = seq // CHUNK
    blocks = (2 * seq * half * 2 + 2 * seq * N_GIF * 4 + 2 * seq * MIX_WIDTH * 2
              + 5 * HEADS * HEAD_DIM * 2 * HEAD_DIM * 4 + 4 * CHUNK * N_REP * V7X_LANES * 4)
    return pl.pallas_call(
        functools.partial(_mlstm_kernel, seq=seq),
        out_shape=(jax.ShapeDtypeStruct((b * seq, MIX_WIDTH), BF16),
                   jax.ShapeDtypeStruct((b, HEADS, HEAD_DIM, HEAD_DIM), F32),
                   jax.ShapeDtypeStruct((b, HEADS, HEAD_DIM), F32),
                   jax.ShapeDtypeStruct((b, HEADS, V7X_LANES), F32)),
        grid=(b,),
        in_specs=[pl.BlockSpec((seq, half), lambda i: (i, 1)),
                  pl.BlockSpec((seq, N_GIF), lambda i: (i, 0)),
                  _resident((None, 2 * HEADS, CHUNK), lambda i: (layer, 0, 0))],
        out_specs=(pl.BlockSpec((seq, MIX_WIDTH), lambda i: (i, 0)),
                   pl.BlockSpec((None, HEADS, HEAD_DIM, HEAD_DIM), lambda i: (i, 0, 0, 0)),
                   pl.BlockSpec((None, HEADS, HEAD_DIM), lambda i: (i, 0, 0)),
                   pl.BlockSpec((None, HEADS, V7X_LANES), lambda i: (i, 0, 0))),
        scratch_shapes=[pltpu.VMEM((HEADS, HEAD_DIM, 2 * HEAD_DIM), F32),
                        pltpu.VMEM((HEADS, 1, V7X_LANES), F32),
                        pltpu.VMEM((2, 8 * nch, CHUNK), F32),
                        pltpu.VMEM((2, CHUNK, N_REP * V7X_LANES), F32)],
        compiler_params=_params(blocks, 1),
        name="mlstm",
    )(p, gif, bias)


SEQ_BLOCK_FIRST = 8
SEQ_BLOCK = 16


def _column_expander(sb):
    er = lax.broadcasted_iota(jnp.int32, (sb, sb * V7X_LANES), 0)
    ec = lax.broadcasted_iota(jnp.int32, (sb, sb * V7X_LANES), 1)
    return jnp.where(ec // V7X_LANES == er, 1.0, 0.0).astype(BF16)


def _columns(x, expander):
    return _dot_tn(x.astype(BF16), expander)


def _ret_step_kernel(p_ref, cos_ref, sin_ref, s_ref, *rest, layer, first):
    o_ref, s_out_ref, row_scr = rest[-3:]
    s_out_ref = _own_layer(s_out_ref, layer, first)
    sb = p_ref.shape[0]
    cs = cos_ref[...]
    sn = sin_ref[...]
    expander = _column_expander(sb)
    for h in range(HEADS):
        lo = h * HEAD_DIM
        gamma = math.exp(LOG_GAMMA[h])
        q = _rope(p_ref[:, lo:lo + HEAD_DIM], cs, sn)
        k = _rope(p_ref[:, MIX_WIDTH + lo:MIX_WIDTH + lo + HEAD_DIM], cs, sn) * QK_SCALE
        v = p_ref[:, 2 * MIX_WIDTH + lo:2 * MIX_WIDTH + lo + HEAD_DIM]
        g = p_ref[:, 3 * MIX_WIDTH + lo:3 * MIX_WIDTH + lo + HEAD_DIM]
        qt = _columns(q * gamma, expander)
        kt = _columns(k, expander)
        for b in range(sb):
            s = s_ref[b, h]
            blk = slice(b * V7X_LANES, (b + 1) * V7X_LANES)
            row_scr[b:b + 1, :] = jnp.sum(qt[:, blk] * s, axis=0, keepdims=True)
            s_out_ref[b, h] = gamma * s + kt[:, blk] * v[b:b + 1, :]
        o = jnp.sum(q * k, axis=-1, keepdims=True) * v + row_scr[...]
        o_ref[:, lo:lo + HEAD_DIM] = _head_norm(o) * (g * jax.nn.sigmoid(g))


def _stacked_state(acc, layer, sb):
    tail = (HEADS, HEAD_DIM, HEAD_DIM)
    if acc is None:
        return (), [], pl.BlockSpec((DEPTH, sb) + tail, lambda i: (0, i, 0, 0, 0))
    return (acc,), [pl.BlockSpec(memory_space=pl.ANY)], pl.BlockSpec((None, sb) + tail, lambda i: (layer, i, 0, 0, 0))


def _own_layer(state_out_ref, layer, first):
    if not first:
        return state_out_ref
    for other in range(DEPTH):
        if other != layer:
            state_out_ref[other] = jnp.zeros(state_out_ref.shape[1:], F32)
    return state_out_ref.at[layer]


def _retention_step(p, cos, sin, state, layer, acc):
    nseq = p.shape[0]
    sb = SEQ_BLOCK_FIRST if acc is None else SEQ_BLOCK
    half = 4 * MIX_WIDTH
    sblk = sb * HEADS * HEAD_DIM * HEAD_DIM * 4
    blocks = (2 + 2 * (DEPTH if acc is None else 1)) * sblk + 2 * sb * half * 4 + 2 * sb * MIX_WIDTH * 4
    state_blk = pl.BlockSpec((None, sb, HEADS, HEAD_DIM, HEAD_DIM), lambda i: (layer, i, 0, 0, 0))
    extra, extra_specs, out_state_blk = _stacked_state(acc, layer, sb)
    return pl.pallas_call(
        functools.partial(_ret_step_kernel, layer=layer, first=acc is None),
        out_shape=(jax.ShapeDtypeStruct((nseq, MIX_WIDTH), F32),
                   jax.ShapeDtypeStruct(state.shape, F32)),
        grid=(nseq // sb,),
        in_specs=[pl.BlockSpec((sb, half), lambda i: (i, 0)),
                  _resident((1, HEAD_DIM), lambda i: (0, 0)),
                  _resident((1, HEAD_DIM), lambda i: (0, 0)),
                  state_blk] + extra_specs,
        out_specs=(pl.BlockSpec((sb, MIX_WIDTH), lambda i: (i, 0)), out_state_blk),
        input_output_aliases={4: 1} if extra else {},
        scratch_shapes=[pltpu.VMEM((sb, HEAD_DIM), F32)],
        compiler_params=_params(blocks, 1),
        name="retention_step",
    )(p, cos, sin, state, *extra)


def _mlstm_step_kernel(p_ref, gif_ref, bias_ref, c_ref, n_ref, m_ref, *rest, layer, first):
    o_ref, c_out_ref, n_out_ref, m_out_ref, row_scr = rest[-5:]
    c_out_ref = _own_layer(c_out_ref, layer, first)
    sb = p_ref.shape[0]
    fb = gif_ref[...] + bias_ref[...]
    lf_all = jax.nn.log_sigmoid(fb)
    lane = lax.broadcasted_iota(jnp.int32, (sb, V7X_LANES), 1)
    m_all = m_ref[...]
    m_acc = jnp.zeros((sb, V7X_LANES), F32)
    expander = _column_expander(sb)
    for h in range(HEADS):
        lo = h * HEAD_DIM
        q = p_ref[:, lo:lo + HEAD_DIM]
        k = p_ref[:, MIX_WIDTH + lo:MIX_WIDTH + lo + HEAD_DIM] * QK_SCALE
        v = p_ref[:, 2 * MIX_WIDTH + lo:2 * MIX_WIDTH + lo + HEAD_DIM]
        og = p_ref[:, 3 * MIX_WIDTH + lo:3 * MIX_WIDTH + lo + HEAD_DIM]
        ig = fb[:, h:h + 1]
        inter = lf_all[:, HEADS + h:HEADS + h + 1] + m_all[:, h:h + 1]
        m_t = jnp.maximum(inter, ig)
        w_intra = jnp.exp(ig - m_t)
        w_inter = jnp.exp(inter - m_t)
        nv = n_ref[:, lo:lo + HEAD_DIM]
        kw = k * w_intra
        qt = _columns(q, expander)
        kt = _columns(kw, expander)
        for b in range(sb):
            cm = c_ref[b, h]
            blk = slice(b * V7X_LANES, (b + 1) * V7X_LANES)
            row_scr[b:b + 1, :] = jnp.sum(qt[:, blk] * cm, axis=0, keepdims=True)
            c_out_ref[b, h] = w_inter[b:b + 1, :] * cm + kt[:, blk] * v[b:b + 1, :]
        s = jnp.sum(q * k, axis=-1, keepdims=True) * w_intra
        num = s * v + w_inter * row_scr[...]
        den = s + w_inter * jnp.sum(q * nv, axis=-1, keepdims=True)
        hh = num / jnp.maximum(jnp.abs(den), jnp.exp(-m_t))
        n_out_ref[:, lo:lo + HEAD_DIM] = w_inter * nv + kw
        m_acc = jnp.where(lane == h, m_t, m_acc)
        o_ref[:, lo:lo + HEAD_DIM] = _head_norm(hh) * jax.nn.sigmoid(og)
    m_out_ref[...] = m_acc


def _mlstm_step(p, gif, bias, c_state, n_state, m_state, layer, acc):
    nseq = p.shape[0]
    sb = SEQ_BLOCK_FIRST if acc is None else SEQ_BLOCK
    half = 4 * MIX_WIDTH
    sblk = sb * HEADS * HEAD_DIM * HEAD_DIM * 4
    blocks = (2 + 2 * (DEPTH if acc is None else 1)) * sblk + 2 * sb * half * 4 + 8 * sb * MIX_WIDTH * 4
    rows = lambda w: pl.BlockSpec((sb, w), lambda i: (i, 0))
    state_blk = pl.BlockSpec((None, sb, HEADS, HEAD_DIM, HEAD_DIM), lambda i: (layer, i, 0, 0, 0))
    extra, extra_specs, out_state_blk = _stacked_state(acc, layer, sb)
    return pl.pallas_call(
        functools.partial(_mlstm_step_kernel, layer=layer, first=acc is None),
        out_shape=(jax.ShapeDtypeStruct((nseq, MIX_WIDTH), F32),
                   jax.ShapeDtypeStruct(c_state.shape, F32),
                   jax.ShapeDtypeStruct((nseq, MIX_WIDTH), F32),
                   jax.ShapeDtypeStruct((nseq, V7X_LANES), F32)),
        grid=(nseq // sb,),
        in_specs=[pl.BlockSpec((sb, half), lambda i: (i, 1)),
                  rows(N_GIF),
                  _resident((None, 1, N_GIF), lambda i: (layer, 0, 0)),
                  state_blk,
                  pl.BlockSpec((None, sb, MIX_WIDTH), lambda i: (layer, i, 0)),
                  pl.BlockSpec((None, sb, HEADS), lambda i: (layer, i, 0))] + extra_specs,
        out_specs=(rows(MIX_WIDTH), out_state_blk, rows(MIX_WIDTH), rows(V7X_LANES)),
        input_output_aliases={6: 1} if extra else {},
        scratch_shapes=[pltpu.VMEM((sb, HEAD_DIM), F32)],
        compiler_params=_params(blocks, 1),
        name="mlstm_step",
    )(p, gif, bias, c_state, n_state, m_state, *extra)


def _rope_tables(pos):
    inv = ROPE_THETA ** (-jnp.arange(0, HEAD_DIM, 2, dtype=F32) / HEAD_DIM)
    ang = pos.astype(F32)[:, None] * inv[None, :]
    cos, sin = jnp.cos(ang), jnp.sin(ang)
    return jnp.concatenate([cos, cos], axis=-1), jnp.concatenate([-sin, sin], axis=-1)


W_IN_BLOCKS = N_IN_PADDED // W_IN_ROWS


def _w_in_source_row(j):
    o = j * W_IN_ROWS
    off_gif = N_MAIN
    off_su = off_gif + 2 * HEADS
    off_gates = off_su + MIX_WIDTH
    return jnp.where(o < OFF_GATES, o,
                     jnp.where(o < OFF_SU, o - OFF_GATES + off_gates,
                               jnp.where(o < OFF_GIF, o - OFF_SU + off_su, off_gif)))


def _w_in_layout_kernel(w_hbm, o_ref, buf, sem):
    l, j = pl.program_id(0), pl.program_id(1)
    step = l * W_IN_BLOCKS + j
    total = pl.num_programs(0) * W_IN_BLOCKS

    def read(s):
        src = pl.multiple_of(_w_in_source_row(s % W_IN_BLOCKS), 8)
        return pltpu.make_async_copy(w_hbm.at[s // W_IN_BLOCKS, pl.ds(src, W_IN_ROWS), :], buf.at[s % 2], sem.at[s % 2])

    @pl.when(step == 0)
    def _():
        read(step).start()

    @pl.when(step + 1 < total)
    def _():
        read(step + 1).start()

    read(step).wait()
    x = buf[step % 2]
    row = lax.broadcasted_iota(jnp.int32, x.shape, 0)
    keep = jnp.logical_or(j < W_IN_BLOCKS - 1, row < 2 * HEADS)
    o_ref[...] = jnp.where(keep, x, 0.0).astype(BF16)


def _w_in_layout(w_in):
    depth = w_in.shape[0]
    w_t = jnp.swapaxes(w_in, 1, 2)
    blocks = 2 * W_IN_ROWS * D_MODEL * 4 + 2 * W_IN_ROWS * D_MODEL * 2 + 2 * W_IN_ROWS * D_MODEL * 4
    return pl.pallas_call(
        _w_in_layout_kernel,
        out_shape=jax.ShapeDtypeStruct((depth, N_IN_PADDED, D_MODEL), BF16),
        grid=(depth, W_IN_BLOCKS),
        in_specs=[pl.BlockSpec(memory_space=pl.ANY)],
        out_specs=pl.BlockSpec((None, W_IN_ROWS, D_MODEL), lambda l, j: (l, j, 0)),
        scratch_shapes=[pltpu.VMEM((2, W_IN_ROWS, D_MODEL), F32), pltpu.SemaphoreType.DMA((2,))],
        compiler_params=_params(blocks, 2),
        name="w_in_layout",
    )(w_t)


def _prepare_weights(w_ffn1_gu, w_ffn1_down, w_in, w_s5_glu, w_branch, w_out, w_ffn2_gu, w_ffn2_down):
    cast = lambda w: w.astype(BF16)
    return dict(gu1=cast(w_ffn1_gu), d1=cast(w_ffn1_down), w_in=_w_in_layout(w_in), glu=cast(w_s5_glu),
                branch=cast(w_branch), out=cast(w_out), gu2=cast(w_ffn2_gu), d2=cast(w_ffn2_down))


def _run_trunk(x, nseq, seq, states, wts, norms, s5w, bias, ln_final, prompt):
    t = nseq * seq
    tm = 512 if t % 512 == 0 else t
    tm_ffn = 1024 if t % 1024 == 0 else tm
    act_dtype = BF16 if prompt else F32
    if prompt:
        cos, sin = _rope_tables(jnp.arange(seq, dtype=jnp.int32))
    else:
        cos, sin = _rope_tables(PAST_LEN + jnp.arange(seq, dtype=jnp.int32))
    if prompt:
        x0 = jnp.zeros((nseq, 2 * S5_N), F32)
    else:
        st_ret, st_mc, st_mn, st_mm, st_sre, st_sim = states
        st_mn = st_mn.reshape(DEPTH, nseq, MIX_WIDTH)
        x0 = jnp.concatenate([st_sre.reshape(DEPTH, nseq, S5_N), st_sim.reshape(DEPTH, nseq, S5_N)], axis=-1)
    rets, mcs, mns, mms, xls = [], [], [], [], []
    ret_acc = mc_acc = None
    for l in range(DEPTH):
        x = _ffn(x, norms["ffn1"], wts["gu1"], wts["d1"], ln_final, l, tm_ffn, final=False)
        main, gates, su, gif = _inproj(x, norms["mix"], wts["w_in"], l, tm, act_dtype)
        if prompt:
            r_out, new_ret = _retention_prompt(main, cos, sin, nseq, seq)
            m_out, new_mc, new_mn, new_mm = _mlstm_prompt(main, gif, bias, l, nseq, seq)
            s_out, x_last = _s5_prompt(su.reshape(nseq, seq, MIX_WIDTH), x0, s5w, l)
            rets.append(new_ret)
            mcs.append(new_mc)
        else:
            r_out, ret_acc = _retention_step(main, cos, sin, st_ret, l, ret_acc)
            m_out, mc_acc, new_mn, new_mm = _mlstm_step(main, gif, bias, st_mc, st_mn, st_mm, l, mc_acc)
            s_out, x_last = _s5_step(su, x0, s5w, l)
        mns.append(new_mn)
        mms.append(new_mm)
        xls.append(x_last)
        x = _merge_ffn(x, r_out, m_out, s_out, gates, wts["branch"], wts["out"], norms["ffn2"], wts["gu2"], wts["d2"],
                       ln_final, l, tm, final=(l == DEPTH - 1))
    new_ret, new_mc = (jnp.stack(rets), jnp.stack(mcs)) if prompt else (ret_acc, mc_acc)
    new_mn = jnp.stack(mns).reshape(DEPTH, nseq, HEADS, HEAD_DIM)
    new_mm = jnp.stack(mms)[:, :, :, 0] if prompt else jnp.stack(mms)[:, :, :HEADS]
    xl = jnp.stack(xls)
    new_sre = xl[:, :, :S5_N].reshape(DEPTH, nseq, S5_GROUPS, S5_STATE)
    new_sim = xl[:, :, S5_N:].reshape(DEPTH, nseq, S5_GROUPS, S5_STATE)
    return x, (new_ret, new_mc, new_mn, new_mm, new_sre, new_sim)


def kernel(x_prompt, x_sample, state_ret, state_mlstm_c, state_mlstm_n, state_mlstm_m, state_s5_re, state_s5_im, ln_ffn1, w_ffn1_gu, w_ffn1_down, ln_mix, w_in, b_gates, s5_a_re, s5_a_im, s5_b_re, s5_b_im, s5_c_re, s5_c_im, s5_d, s5_log_dt, w_s5_glu, w_branch, w_out, ln_ffn2, w_ffn2_gu, w_ffn2_down, ln_final):
    wts = _prepare_weights(w_ffn1_gu, w_ffn1_down, w_in, w_s5_glu, w_branch, w_out, w_ffn2_gu, w_ffn2_down)
    norms = dict(ffn1=ln_ffn1.reshape(DEPTH, 1, D_MODEL), mix=ln_mix.reshape(DEPTH, 1, D_MODEL),
                 ffn2=ln_ffn2.reshape(DEPTH, 1, D_MODEL))
    lnf = ln_final.reshape(1, D_MODEL)
    a_row, bbr, bbi = _s5_discretise(s5_a_re, s5_a_im, s5_log_dt, s5_b_re, s5_b_im)
    s5w = (a_row, _block_diag_in(bbr), _block_diag_in(bbi), _block_diag_out(s5_c_re), _block_diag_out(-s5_c_im),
           s5_d.reshape(DEPTH, 1, MIX_WIDTH), wts["glu"])
    bias = jnp.pad(b_gates, ((0, 0), (0, N_GIF - 2 * HEADS))).reshape(DEPTH, 1, N_GIF)
    bias_rows = jnp.broadcast_to(b_gates[:, :, None], (DEPTH, 2 * HEADS, CHUNK))

    pb, pl_len, _ = x_prompt.shape
    sb, sl, _ = x_sample.shape
    y_p, st_p = _run_trunk(x_prompt.reshape(pb * pl_len, D_MODEL), pb, pl_len, None, wts, norms, s5w, bias_rows, lnf, True)
    sample_states = (state_ret, state_mlstm_c, state_mlstm_n, state_mlstm_m, state_s5_re, state_s5_im)
    y_s, st_s = _run_trunk(x_sample.reshape(sb * sl, D_MODEL), sb, sl, sample_states, wts, norms, s5w, bias, lnf, False)
    return (y_p.reshape(pb, pl_len, D_MODEL), y_s.reshape(sb, sl, D_MODEL)) + st_p + st_s
```

```python
import functools
import math

import jax
import jax.numpy as jnp
from jax import lax
from jax.experimental import pallas as pl
from jax.experimental.pallas import tpu as pltpu

F32 = jnp.float32
BF16 = jnp.bfloat16

D_MODEL = 1024
DEPTH = 4
MIX_WIDTH = D_MODEL // 2
N_BRANCH = 3
HEADS = 4
HEAD_DIM = MIX_WIDTH // HEADS
S5_GROUP = 16
S5_GROUPS = MIX_WIDTH // S5_GROUP
S5_STATE = 64
S5_N = S5_GROUPS * S5_STATE
D_FF = 2816
CHUNK = 128
PAST_LEN = 16384
ROPE_THETA = 10000.0
EPS = 1e-6

N_MAIN = 8 * MIX_WIDTH
N_GATES = N_BRANCH * D_MODEL
N_GIF = 128
W_IN_ROWS = 512
N_IN_PADDED = N_MAIN + N_GATES + MIX_WIDTH + W_IN_ROWS
OFF_GATES = N_MAIN
OFF_SU = N_MAIN + N_GATES
OFF_GIF = OFF_SU + MIX_WIDTH

V7X_LANES = 128
V7X_MXU_DIM = 256
V7X_VMEM_BYTES = 64 * 1024 * 1024

FF_CHUNK = V7X_MXU_DIM
PROJ_CHUNK = 2 * V7X_MXU_DIM
S5_HALF = MIX_WIDTH // 2
S5_HALF_N = S5_N // 2
S5_TOKENS = 64
LOG_GAMMA = tuple(math.log1p(-(2.0 ** (-5.0 - h))) for h in range(HEADS))
QK_SCALE = HEAD_DIM ** -0.5


VMEM_TEMPORARIES_BYTES = 8 << 20
VMEM_TEMPORARIES_SCALE = 1.25
V7X_VMEM_RESERVED_BYTES = 6 << 20


def _vmem_limit(block_bytes):
    want = int(block_bytes * VMEM_TEMPORARIES_SCALE) + VMEM_TEMPORARIES_BYTES
    return min(want, V7X_VMEM_BYTES - V7X_VMEM_RESERVED_BYTES)


def _params(block_bytes, n_grid):
    return pltpu.CompilerParams(
        dimension_semantics=("arbitrary",) * n_grid,
        vmem_limit_bytes=_vmem_limit(block_bytes))


def _resident(shape, index_map):
    return pl.BlockSpec(shape, index_map, pipeline_mode=pl.Buffered(1))


def _dot(a, b):
    return jnp.dot(a, b, preferred_element_type=F32)


def _dot_nt(a, b):
    return lax.dot_general(a, b, (((1,), (1,)), ((), ())), preferred_element_type=F32)


def _dot_tn(a, b):
    return lax.dot_general(a, b, (((0,), (0,)), ((), ())), preferred_element_type=F32)


def _rms(x, g):
    return x * lax.rsqrt(jnp.mean(x * x, axis=-1, keepdims=True) + EPS) * g


def _head_norm(x):
    mu = jnp.mean(x, axis=-1, keepdims=True)
    xc = x - mu
    var = jnp.mean(xc * xc, axis=-1, keepdims=True)
    return xc * lax.rsqrt(var + EPS)


def _rope(x, cos, sin_signed):
    return x * cos + pltpu.roll(x, HEAD_DIM // 2, 1) * sin_signed


def _ffn_tile(x, ln_ref, wgu_ref, wd_ref, lnf_ref, o_ref, acc_ref, final):
    xn = _rms(x, ln_ref[...]).astype(BF16)
    for c in range(D_FF // FF_CHUNK):
        lo = c * FF_CHUNK
        g = _dot(xn, wgu_ref[:, lo:lo + FF_CHUNK])
        u = _dot(xn, wgu_ref[:, D_FF + lo:D_FF + lo + FF_CHUNK])
        a = (g * jax.nn.sigmoid(g) * u).astype(BF16)
        d = _dot(a, wd_ref[lo:lo + FF_CHUNK, :])
        if c == 0:
            acc_ref[...] = d
        else:
            acc_ref[...] += d
    y = x + 0.5 * acc_ref[...]
    if final:
        y = _rms(y, lnf_ref[...])
    o_ref[...] = y


def _ffn_kernel(x_ref, ln_ref, wgu_ref, wd_ref, lnf_ref, o_ref, acc_ref, *, final):
    _ffn_tile(x_ref[...], ln_ref, wgu_ref, wd_ref, lnf_ref, o_ref, acc_ref, final)


def _ffn(x, ln, wgu, wd, lnf, layer, tm, final):
    t = x.shape[0]
    blocks = (4 * tm * D_MODEL * 4 + tm * D_MODEL * 4 + D_MODEL * 2 * D_FF * 2 + D_FF * D_MODEL * 2
              + 4 * tm * FF_CHUNK * 4)
    return pl.pallas_call(
        functools.partial(_ffn_kernel, final=final),
        out_shape=jax.ShapeDtypeStruct((t, D_MODEL), F32),
        grid=(t // tm,),
        in_specs=[
            pl.BlockSpec((tm, D_MODEL), lambda i: (i, 0)),
            _resident((None, 1, D_MODEL), lambda i: (layer, 0, 0)),
            _resident((None, D_MODEL, 2 * D_FF), lambda i: (layer, 0, 0)),
            _resident((None, D_FF, D_MODEL), lambda i: (layer, 0, 0)),
            _resident((1, D_MODEL), lambda i: (0, 0)),
        ],
        out_specs=pl.BlockSpec((tm, D_MODEL), lambda i: (i, 0)),
        scratch_shapes=[pltpu.VMEM((tm, D_MODEL), F32)],
        compiler_params=_params(blocks, 1),
        name="ffn",
    )(x, ln, wgu, wd, lnf)


def _inproj_kernel(x_ref, ln_ref, w_ref, main_ref, gates_ref, su_ref, gif_ref):
    xn = _rms(x_ref[...], ln_ref[...]).astype(BF16)
    for c in range(N_MAIN // PROJ_CHUNK):
        lo = c * PROJ_CHUNK
        main_ref[:, lo:lo + PROJ_CHUNK] = _dot_nt(xn, w_ref[lo:lo + PROJ_CHUNK, :]).astype(main_ref.dtype)
    for c in range(N_GATES // PROJ_CHUNK):
        lo = c * PROJ_CHUNK
        gates_ref[:, lo:lo + PROJ_CHUNK] = _dot_nt(
            xn, w_ref[OFF_GATES + lo:OFF_GATES + lo + PROJ_CHUNK, :]).astype(gates_ref.dtype)
    su_ref[...] = _dot_nt(xn, w_ref[OFF_SU:OFF_SU + MIX_WIDTH, :])
    gif_ref[...] = _dot_nt(xn, w_ref[OFF_GIF:OFF_GIF + N_GIF, :])


def _inproj(x, ln, w, layer, tm, act_dtype):
    t = x.shape[0]
    ab = jnp.dtype(act_dtype).itemsize
    blocks = (2 * tm * D_MODEL * 4 + D_MODEL * N_IN_PADDED * 2
              + 2 * tm * (N_MAIN + N_GATES) * ab + 2 * tm * (MIX_WIDTH + N_GIF) * 4
              + 4 * tm * PROJ_CHUNK * 4)
    return pl.pallas_call(
        _inproj_kernel,
        out_shape=(jax.ShapeDtypeStruct((t, N_MAIN), act_dtype),
                   jax.ShapeDtypeStruct((t, N_GATES), act_dtype),
                   jax.ShapeDtypeStruct((t, MIX_WIDTH), F32),
                   jax.ShapeDtypeStruct((t, N_GIF), F32)),
        grid=(t // tm,),
        in_specs=[
            pl.BlockSpec((tm, D_MODEL), lambda i: (i, 0)),
            _resident((None, 1, D_MODEL), lambda i: (layer, 0, 0)),
            _resident((None, N_IN_PADDED, D_MODEL), lambda i: (layer, 0, 0)),
        ],
        out_specs=(pl.BlockSpec((tm, N_MAIN), lambda i: (i, 0)),
                   pl.BlockSpec((tm, N_GATES), lambda i: (i, 0)),
                   pl.BlockSpec((tm, MIX_WIDTH), lambda i: (i, 0)),
                   pl.BlockSpec((tm, N_GIF), lambda i: (i, 0))),
        compiler_params=_params(blocks, 1),
        name="inproj",
    )(x, ln, w)


def _merge_ffn_kernel(x_ref, r_ref, m_ref, s_ref, g_ref, wb_ref, wo_ref, ln_ref, wgu_ref, wd_ref, lnf_ref,
                      o_ref, acc_ref, *, final):
    merged = None
    for n, b_ref in enumerate((r_ref, m_ref, s_ref)):
        up = _dot(b_ref[...].astype(BF16), wb_ref[n])
        gate = jax.nn.sigmoid(g_ref[:, n * D_MODEL:(n + 1) * D_MODEL].astype(F32))
        merged = gate * up if merged is None else merged + gate * up
    x1 = x_ref[...] + _dot(merged.astype(BF16), wo_ref[...])
    _ffn_tile(x1, ln_ref, wgu_ref, wd_ref, lnf_ref, o_ref, acc_ref, final)


def _merge_ffn(x, r, m, s, gates, wb, wo, ln, wgu, wd, lnf, layer, tm, final):
    t = x.shape[0]
    blocks = (4 * tm * D_MODEL * 4 + 6 * tm * MIX_WIDTH * 4 + 2 * tm * N_GATES * 4
              + N_BRANCH * MIX_WIDTH * D_MODEL * 2 + D_MODEL * D_MODEL * 2 + 4 * tm * D_MODEL * 4
              + tm * D_MODEL * 4 + D_MODEL * 2 * D_FF * 2 + D_FF * D_MODEL * 2 + 4 * tm * FF_CHUNK * 4)
    row = lambda w: pl.BlockSpec((tm, w), lambda i: (i, 0))
    if s.ndim == 3:
        assert s.shape[1] % tm == 0
        per_seq = s.shape[1] // tm
        s_spec = pl.BlockSpec((None, tm, MIX_WIDTH), lambda i: (i // per_seq, i % per_seq, 0))
    else:
        s_spec = row(MIX_WIDTH)
    return pl.pallas_call(
        functools.partial(_merge_ffn_kernel, final=final),
        out_shape=jax.ShapeDtypeStruct((t, D_MODEL), F32),
        grid=(t // tm,),
        in_specs=[row(D_MODEL), row(MIX_WIDTH), row(MIX_WIDTH), s_spec, row(N_GATES),
                  _resident((None, N_BRANCH, MIX_WIDTH, D_MODEL), lambda i: (layer, 0, 0, 0)),
                  _resident((None, D_MODEL, D_MODEL), lambda i: (layer, 0, 0)),
                  _resident((None, 1, D_MODEL), lambda i: (layer, 0, 0)),
                  _resident((None, D_MODEL, 2 * D_FF), lambda i: (layer, 0, 0)),
                  _resident((None, D_FF, D_MODEL), lambda i: (layer, 0, 0)),
                  _resident((1, D_MODEL), lambda i: (0, 0))],
        out_specs=row(D_MODEL),
        scratch_shapes=[pltpu.VMEM((tm, D_MODEL), F32)],
        compiler_params=_params(blocks, 1),
        name="merge_ffn",
    )(x, r, m, s, gates, wb, wo, ln, wgu, wd, lnf)


def _s5_disc_kernel(are_ref, aim_ref, ldt_ref, bre_ref, bim_ref, abr_ref, abi_ref, bbr_ref, bbi_ref):
    ar, ai = are_ref[...], aim_ref[...]
    dt = jnp.exp(ldt_ref[...])
    mag = jnp.exp(ar * dt)
    abr = mag * jnp.cos(ai * dt)
    abi = mag * jnp.sin(ai * dt)
    nr, ni = abr - 1.0, abi
    den = ar * ar + ai * ai
    cr = (nr * ar + ni * ai) / den
    ci = (ni * ar - nr * ai) / den
    br, bi = bre_ref[...], bim_ref[...]
    abr_ref[...] = abr
    abi_ref[...] = abi
    bbr_ref[...] = cr * br - ci * bi
    bbi_ref[...] = cr * bi + ci * br


def _s5_discretise(a_re, a_im, log_dt, b_re, b_im):
    rows = DEPTH * S5_N
    col = lambda a: a.reshape(rows, 1)
    ldt = jnp.broadcast_to(log_dt[:, :, None], (DEPTH, S5_GROUPS, S5_STATE))
    outs = pl.pallas_call(
        _s5_disc_kernel,
        out_shape=(jax.ShapeDtypeStruct((rows, 1), F32), jax.ShapeDtypeStruct((rows, 1), F32),
                   jax.ShapeDtypeStruct((rows, S5_GROUP), F32), jax.ShapeDtypeStruct((rows, S5_GROUP), F32)),
        name="s5_disc",
    )(col(a_re), col(a_im), col(ldt), b_re.reshape(rows, S5_GROUP), b_im.reshape(rows, S5_GROUP))
    abr, abi, bbr, bbi = outs
    a_row = jnp.concatenate([abr.reshape(DEPTH, 1, S5_N), abi.reshape(DEPTH, 1, S5_N)], axis=-1)
    return a_row, bbr.reshape(DEPTH, S5_GROUPS, S5_STATE, S5_GROUP), bbi.reshape(DEPTH, S5_GROUPS, S5_STATE, S5_GROUP)


def _block_diag_in(b):
    gh = S5_GROUPS // 2
    on_diag = jnp.eye(gh, dtype=bool)[None, None, :, None, :, None]
    src = b.reshape(DEPTH, 2, gh, S5_STATE, S5_GROUP).transpose(0, 1, 2, 4, 3)
    m = jnp.where(on_diag, src[:, :, :, :, None, :], 0.0)
    return m.reshape(DEPTH, 2, S5_HALF, S5_HALF_N).astype(BF16)


def _block_diag_out(c):
    gh = S5_GROUPS // 2
    on_diag = jnp.eye(gh, dtype=bool)[None, None, :, None, :, None]
    src = c.reshape(DEPTH, 2, gh, S5_GROUP, S5_STATE).transpose(0, 1, 4, 2, 3)
    m = jnp.where(on_diag, src[:, :, None, :, :, :], 0.0)
    return m.reshape(DEPTH, 2, S5_HALF_N, S5_HALF).astype(BF16)


def _s5_project_in(u, xs_ref, bre_ref, bim_ref):
    ub = u.astype(BF16)
    for k in range(2):
        uk = ub[:, k * S5_HALF:(k + 1) * S5_HALF]
        xs_ref[:, k * S5_HALF_N:(k + 1) * S5_HALF_N] = _dot(uk, bre_ref[k])
        xs_ref[:, S5_N + k * S5_HALF_N:S5_N + (k + 1) * S5_HALF_N] = _dot(uk, bim_ref[k])


def _s5_project_out(xs_ref, u, cre_ref, cimn_ref, d_ref, wglu_ref):
    ys = []
    for k in range(2):
        xr = xs_ref[:, k * S5_HALF_N:(k + 1) * S5_HALF_N].astype(BF16)
        xi = xs_ref[:, S5_N + k * S5_HALF_N:S5_N + (k + 1) * S5_HALF_N].astype(BF16)
        ys.append(_dot(xr, cre_ref[k]) + _dot(xi, cimn_ref[k]))
    y = jnp.concatenate(ys, axis=1) + d_ref[...] * u
    y = jax.nn.gelu(y)
    return y * jax.nn.sigmoid(_dot(y.astype(BF16), wglu_ref[...]))


def _s5_scan_step(ab_ref, xs_ref, rows, xre, xim):
    are = ab_ref[:, :S5_N]
    aim = ab_ref[:, S5_N:]
    nre = are * xre - aim * xim + xs_ref[rows, :S5_N]
    nim = are * xim + aim * xre + xs_ref[rows, S5_N:]
    xs_ref[rows, :S5_N] = nre
    xs_ref[rows, S5_N:] = nim
    return nre, nim


def _s5_step_kernel(u_ref, x0_ref, a_ref, bre_ref, bim_ref, cre_ref, cimn_ref, d_ref, wglu_ref,
                    o_ref, xl_ref, xs_scr):
    u = u_ref[...]
    _s5_project_in(u, xs_scr, bre_ref, bim_ref)
    xre, xim = _s5_scan_step(a_ref, xs_scr, slice(None), x0_ref[:, :S5_N], x0_ref[:, S5_N:])
    xl_ref[:, :S5_N] = xre
    xl_ref[:, S5_N:] = xim
    o_ref[...] = _s5_project_out(xs_scr, u, cre_ref, cimn_ref, d_ref, wglu_ref)


def _s5_prompt_kernel(u_ref, un_ref, x0_ref, a_ref, bre_ref, bim_ref, cre_ref, cimn_ref, d_ref, wglu_ref,
                      o_ref, xl_ref, xs0, xs1, ut0, ut1, st_scr, ab_scr, p_in, p_out, *, nb, tl):
    g = pl.program_id(0)
    m = nb * tl

    tg = V7X_MXU_DIM // nb
    rows_g = nb * tg

    def to_token_major(ref, first):
        pm = p_in[...]
        groups = []
        for lo_t in range(first, first + tl, tg):
            us = ref[:, lo_t:lo_t + tg, :].reshape(rows_g, MIX_WIDTH)
            hi = us.astype(BF16)
            lo = (us - hi.astype(F32)).astype(BF16)
            groups.append(_dot(pm, hi) + _dot(pm, lo))
        return jnp.concatenate(groups, axis=0)

    def store_sequence_major(out, first):
        pm = p_out[...]
        for gq in range(tl // tg):
            piece = _dot(pm, out[gq * rows_g:(gq + 1) * rows_g].astype(BF16)).astype(o_ref.dtype)
            o_ref[:, first + gq * tg:first + (gq + 1) * tg, :] = piece.reshape(nb, tg, MIX_WIDTH)

    def scan(xs, xre, xim):
        for t in range(tl):
            xre, xim = _s5_scan_step(ab_scr, xs, slice(t * nb, (t + 1) * nb), xre, xim)
        return xre, xim

    @pl.when(g == 0)
    def _():
        st_scr[...] = x0_ref[...]
        ab_scr[...] = jnp.broadcast_to(a_ref[...], ab_scr.shape)
        shift = nb.bit_length() - 1
        ro = lax.broadcasted_iota(jnp.int32, (rows_g, rows_g), 0)
        ci = lax.broadcasted_iota(jnp.int32, (rows_g, rows_g), 1)
        seq_major = lambda r: (r & (nb - 1)) * tg + (r >> shift)
        p_in[...] = jnp.where(ci == seq_major(ro), 1.0, 0.0).astype(BF16)
        p_out[...] = jnp.where(ro == seq_major(ci), 1.0, 0.0).astype(BF16)
        u0 = to_token_major(u_ref, 0)
        ut0[...] = u0
        _s5_project_in(u0, xs0, bre_ref, bim_ref)

    u1 = to_token_major(u_ref, tl)
    ut1[...] = u1
    _s5_project_in(u1, xs1, bre_ref, bim_ref)
    xre, xim = scan(xs0, st_scr[:, :S5_N], st_scr[:, S5_N:])
    store_sequence_major(_s5_project_out(xs0, ut0[...], cre_ref, cimn_ref, d_ref, wglu_ref), 0)
    xre, xim = scan(xs1, xre, xim)
    st_scr[:, :S5_N] = xre
    st_scr[:, S5_N:] = xim
    un = to_token_major(un_ref, 0)
    ut0[...] = un
    _s5_project_in(un, xs0, bre_ref, bim_ref)
    store_sequence_major(_s5_project_out(xs1, u1, cre_ref, cimn_ref, d_ref, wglu_ref), tl)
    xl_ref[...] = st_scr[...]


def _s5_weight_specs(layer):
    lay3 = lambda i: (layer, 0, 0)
    lay4 = lambda i: (layer, 0, 0, 0)
    return [_resident((None, 1, 2 * S5_N), lay3),
            _resident((None, 2, S5_HALF, S5_HALF_N), lay4),
            _resident((None, 2, S5_HALF, S5_HALF_N), lay4),
            _resident((None, 2, S5_HALF_N, S5_HALF), lay4),
            _resident((None, 2, S5_HALF_N, S5_HALF), lay4),
            _resident((None, 1, MIX_WIDTH), lay3),
            _resident((None, MIX_WIDTH, MIX_WIDTH), lay3)]


S5_WEIGHT_BYTES = 4 * S5_HALF * S5_HALF_N * 2 * 2 + MIX_WIDTH * MIX_WIDTH * 2 + (2 * S5_N + MIX_WIDTH) * 4


def _s5_step(u, x0, s5w, layer):
    nb = u.shape[0]
    blocks = 4 * nb * MIX_WIDTH * 4 + 5 * nb * 2 * S5_N * 4 + S5_WEIGHT_BYTES + 6 * nb * MIX_WIDTH * 4
    whole = lambda w: pl.BlockSpec((nb, w), lambda i: (0, 0))
    return pl.pallas_call(
        _s5_step_kernel,
        out_shape=(jax.ShapeDtypeStruct(u.shape, F32), jax.ShapeDtypeStruct((nb, 2 * S5_N), F32)),
        grid=(1,),
        in_specs=[whole(MIX_WIDTH), pl.BlockSpec((None, nb, 2 * S5_N), lambda i: (layer, 0, 0))] + _s5_weight_specs(layer),
        out_specs=(whole(MIX_WIDTH), whole(2 * S5_N)),
        scratch_shapes=[pltpu.VMEM((nb, 2 * S5_N), F32)],
        compiler_params=_params(blocks, 1),
        name="s5_step",
    )(u, x0, *s5w)


def _s5_prompt(u, x0, s5w, layer):
    nb, seq, _ = u.shape
    tl = S5_TOKENS
    m = nb * tl
    steps = seq // (2 * tl)
    perm_rows = V7X_MXU_DIM
    assert nb & (nb - 1) == 0 and seq % (2 * tl) == 0 and m % perm_rows == 0
    blocks = (2 * 2 * m * MIX_WIDTH * 4 + 2 * m * MIX_WIDTH * 4 + 2 * 2 * m * MIX_WIDTH * 2 + 2 * m * 2 * S5_N * 4
              + 2 * m * MIX_WIDTH * 4 + 6 * nb * 2 * S5_N * 4 + S5_WEIGHT_BYTES + 2 * perm_rows * perm_rows * 2
              + m * 2 * S5_N * 2 + 6 * m * MIX_WIDTH * 4)
    pair = pl.BlockSpec((nb, 2 * tl, MIX_WIDTH), lambda i: (0, i, 0))
    nxt = pl.BlockSpec((nb, tl, MIX_WIDTH), lambda i: (0, jnp.minimum(2 * i + 2, 2 * steps - 1), 0))
    state = pl.BlockSpec((nb, 2 * S5_N), lambda i: (0, 0))
    return pl.pallas_call(
        functools.partial(_s5_prompt_kernel, nb=nb, tl=tl),
        out_shape=(jax.ShapeDtypeStruct(u.shape, BF16), jax.ShapeDtypeStruct((nb, 2 * S5_N), F32)),
        grid=(steps,),
        in_specs=[pair, nxt, _resident((nb, 2 * S5_N), lambda i: (0, 0))] + _s5_weight_specs(layer),
        out_specs=(pair, state),
        scratch_shapes=[pltpu.VMEM((m, 2 * S5_N), F32), pltpu.VMEM((m, 2 * S5_N), F32),
                        pltpu.VMEM((m, MIX_WIDTH), F32), pltpu.VMEM((m, MIX_WIDTH), F32),
                        pltpu.VMEM((nb, 2 * S5_N), F32), pltpu.VMEM((nb, 2 * S5_N), F32),
                        pltpu.VMEM((perm_rows, perm_rows), BF16), pltpu.VMEM((perm_rows, perm_rows), BF16)],
        compiler_params=_params(blocks, 1),
        name="s5",
    )(u, u, x0, *s5w)


def _ret_kernel(p_ref, cos_ref, sin_ref, o_ref, s_out_ref, s_scr, *, seq):
    c = CHUNK
    heads = range(HEADS)
    row = lax.broadcasted_iota(jnp.int32, (c, c), 0)
    col = lax.broadcasted_iota(jnp.int32, (c, c), 1)
    diff = (row - col).astype(F32)
    ivec = lax.broadcasted_iota(jnp.int32, (c, HEAD_DIM), 0).astype(F32)
    dmats = [jnp.where(diff >= 0, jnp.exp(lg * jnp.maximum(diff, 0.0)), 0.0) * QK_SCALE for lg in LOG_GAMMA]
    q_decs = [jnp.exp(lg * (ivec + 1.0)) for lg in LOG_GAMMA]
    k_decs = [jnp.exp(lg * (c - 1.0 - ivec)) * QK_SCALE for lg in LOG_GAMMA]
    s_scr[...] = jnp.zeros_like(s_scr)

    def chunk(j, carry):
        r = pl.ds(pl.multiple_of(j * c, c), c)
        cs = cos_ref[r, :]
        sn = sin_ref[r, :]
        sl = lambda part, h: p_ref[r, part * MIX_WIDTH + h * HEAD_DIM:part * MIX_WIDTH + (h + 1) * HEAD_DIM]
        q = [_rope(sl(0, h).astype(F32), cs, sn) for h in heads]
        k = [_rope(sl(1, h).astype(F32), cs, sn) for h in heads]
        v = [sl(2, h).astype(BF16) for h in heads]
        s = [s_scr[h] for h in heads]
        qb = [q[h].astype(BF16) for h in heads]
        kb = [k[h].astype(BF16) for h in heads]
        att = [_dot_nt(qb[h], kb[h]) for h in heads]
        inter = [_dot((q[h] * q_decs[h]).astype(BF16), s[h].astype(BF16)) for h in heads]
        upd = [_dot_tn((k[h] * k_decs[h]).astype(BF16), v[h]) for h in heads]
        attb = [(att[h] * dmats[h]).astype(BF16) for h in heads]
        o = [_dot(attb[h], v[h]) + inter[h] for h in heads]
        for h in heads:
            s_scr[h] = s[h] * math.exp(LOG_GAMMA[h] * c) + upd[h]
        on = [_head_norm(o[h]) for h in heads]
        for h in heads:
            g = sl(3, h).astype(F32)
            o_ref[r, h * HEAD_DIM:(h + 1) * HEAD_DIM] = (on[h] * (g * jax.nn.sigmoid(g))).astype(o_ref.dtype)
        return carry

    lax.fori_loop(0, seq // c, chunk, 0, unroll=2)
    s_out_ref[...] = s_scr[...]


def _retention_prompt(p, cos, sin, b, seq):
    half = 4 * MIX_WIDTH
    blocks = 2 * seq * half * 2 + 2 * seq * V7X_LANES * 4 + 2 * seq * MIX_WIDTH * 2 + 3 * HEADS * HEAD_DIM * HEAD_DIM * 4
    return pl.pallas_call(
        functools.partial(_ret_kernel, seq=seq),
        out_shape=(jax.ShapeDtypeStruct((b * seq, MIX_WIDTH), BF16),
                   jax.ShapeDtypeStruct((b, HEADS, HEAD_DIM, HEAD_DIM), F32)),
        grid=(b,),
        in_specs=[pl.BlockSpec((seq, half), lambda i: (i, 0)),
                  _resident((seq, HEAD_DIM), lambda i: (0, 0)),
                  _resident((seq, HEAD_DIM), lambda i: (0, 0))],
        out_specs=(pl.BlockSpec((seq, MIX_WIDTH), lambda i: (i, 0)),
                   pl.BlockSpec((None, HEADS, HEAD_DIM, HEAD_DIM), lambda i: (i, 0, 0, 0))),
        scratch_shapes=[pltpu.VMEM((HEADS, HEAD_DIM, HEAD_DIM), F32)],
        compiler_params=_params(blocks, 1),
        name="retention",
    )(p, cos, sin)


N_REP = 3 * HEADS


def _split_dot(x, w):
    hi = x.astype(BF16)
    lo = (x - hi.astype(F32)).astype(BF16)
    return _dot(hi, w) + _dot(lo, w)


def _split_dot_tn(x, w):
    hi = x.astype(BF16)
    lo = (x - hi.astype(F32)).astype(BF16)
    return _dot_tn(hi, w) + _dot_tn(lo, w)


def _mlstm_kernel(p_ref, gif_ref, bias_ref, o_ref, c_out_ref, n_out_ref, m_out_ref,
                  cn_scr, m_scr, rows_scr, rep_scr, *, seq):
    c = CHUNK
    nch = seq // c
    nrow = 8 * nch
    heads = range(HEADS)
    neg_inf = float("-inf")
    row = lax.broadcasted_iota(jnp.int32, (c, c), 0)
    col = lax.broadcasted_iota(jnp.int32, (c, c), 1)
    causal = row >= col
    triu = jnp.where(row <= col, 1.0, 0.0).astype(BF16)
    ones = jnp.ones((c, V7X_LANES), BF16)
    er = lax.broadcasted_iota(jnp.int32, (16, N_REP * V7X_LANES), 0)
    ec = lax.broadcasted_iota(jnp.int32, (16, N_REP * V7X_LANES), 1)
    expand = jnp.where(ec // V7X_LANES == er, 1.0, 0.0).astype(BF16)
    cn_scr[...] = jnp.zeros_like(cn_scr)
    m_scr[...] = jnp.zeros_like(m_scr)

    pieces = []
    sub8 = lax.broadcasted_iota(jnp.int32, (8, c), 0)
    for j in range(nch):
        ft = gif_ref[j * c:(j + 1) * c, :].T[0:8, :] + bias_ref[...]
        pieces.append(jnp.where(sub8 < HEADS, ft, jax.nn.log_sigmoid(ft)))
    x = jnp.concatenate(pieces, axis=0)
    cum = _split_dot(x, triu)
    sub = lax.broadcasted_iota(jnp.int32, (nrow, c), 0) % 8
    tok = lax.broadcasted_iota(jnp.int32, (nrow, c), 1)
    a = x - pltpu.roll(cum, nrow - HEADS, 0)
    d = 1
    while d < c:
        a = jnp.maximum(a, jnp.where(tok >= d, pltpu.roll(a, d, 1), neg_inf))
        d *= 2
    rows_scr[0] = jnp.where(sub < HEADS, x, cum)
    rows_scr[1] = a

    def replicate(j):
        r8 = pl.ds(pl.multiple_of(j * 8, 8), 8)
        r16 = jnp.concatenate([rows_scr[0, r8, :], rows_scr[1, r8, :]], axis=0)
        return _split_dot_tn(r16, expand)

    rep_scr[0] = replicate(0)

    def chunk(j, carry):
        r = pl.ds(pl.multiple_of(j * c, c), c)
        r8 = pl.ds(pl.multiple_of(j * 8, 8), 8)
        slot = j % 2
        rep_next = replicate(jnp.minimum(j + 1, nch - 1))
        sl = lambda part, h: p_ref[r, part * MIX_WIDTH + h * HEAD_DIM:part * MIX_WIDTH + (h + 1) * HEAD_DIM]
        blk = lambda n, h: rep_scr[slot, :, (n * HEADS + h) * V7X_LANES:(n * HEADS + h + 1) * V7X_LANES]
        y8 = rows_scr[0, r8, :]
        q = [sl(0, h).astype(BF16) for h in heads]
        k = [sl(1, h).astype(BF16) for h in heads]
        v1 = [jnp.concatenate([sl(2, h).astype(BF16), ones], axis=1) for h in heads]
        cn = [cn_scr[h] for h in heads]
        m_prev = [m_scr[h] for h in heads]
        ic = [blk(0, h) for h in heads]
        bc = [blk(1, h) for h in heads]
        cmc = [blk(2, h) for h in heads]
        qk = [_dot_nt(q[h], k[h]) for h in heads]
        qcn = [_dot(q[h], cn[h].astype(BF16)) for h in heads]
        b_last = [bc[h][c - 1:c, :] for h in heads]
        wk_log = [b_last[h] - bc[h] + ic[h] for h in heads]
        m_new = [jnp.maximum(b_last[h] + m_prev[h], jnp.max(wk_log[h], axis=0, keepdims=True)) for h in heads]
        wk = [jnp.exp(wk_log[h] - m_new[h]) * QK_SCALE for h in heads]
        decay = [jnp.exp(b_last[h] + m_prev[h] - m_new[h]) for h in heads]
        kw = [(k[h].astype(F32) * wk[h]).astype(BF16) for h in heads]
        upd = [_dot_tn(kw[h], v1[h]) for h in heads]
        m_t = [bc[h] + jnp.maximum(m_prev[h], cmc[h]) for h in heads]
        dlog = [jnp.where(causal, bc[h] - y8[HEADS + h:HEADS + h + 1, :] + y8[h:h + 1, :], neg_inf) for h in heads]
        w_intra = [jnp.exp(dlog[h] - m_t[h]) * QK_SCALE for h in heads]
        w_inter = [jnp.exp(bc[h] + m_prev[h] - m_t[h]) for h in heads]
        s = [(qk[h] * w_intra[h]).astype(BF16) for h in heads]
        sv = [_dot(s[h], v1[h]) for h in heads]
        hh = []
        for h in heads:
            num = sv[h][:, :HEAD_DIM] + w_inter[h] * qcn[h][:, :HEAD_DIM]
            den = sv[h][:, HEAD_DIM:] + w_inter[h] * qcn[h][:, HEAD_DIM:]
            hh.append(num / jnp.maximum(jnp.abs(den), jnp.exp(-m_t[h])))
        for h in heads:
            cn_scr[h] = jnp.concatenate([decay[h], decay[h]], axis=1) * cn[h] + upd[h]
            m_scr[h] = m_new[h]
        on = [_head_norm(hh[h]) for h in heads]
        for h in heads:
            o_ref[r, h * HEAD_DIM:(h + 1) * HEAD_DIM] = (on[h] * jax.nn.sigmoid(sl(3, h).astype(F32))).astype(o_ref.dtype)
        rep_scr[1 - slot] = rep_next
        return carry

    lax.fori_loop(0, nch, chunk, 0, unroll=2)
    for h in heads:
        cnh = cn_scr[h]
        c_out_ref[h] = cnh[:, :HEAD_DIM]
        n_out_ref[h:h + 1, :] = cnh[:, HEAD_DIM:].T[0:1, :]
        m_out_ref[h:h + 1, :] = m_scr[h]


def _mlstm_prompt(p, gif, bias, layer, b, seq):
    half = 4 * MIX_WIDTH
    nch = seq // CHUNK
    blocks = (2 * seq * half * 2 + 2 * seq * N_GIF * 4 + 2 * seq * MIX_WIDTH * 2
              + 5 * HEADS * HEAD_DIM * 2 * HEAD_DIM * 4 + 4 * CHUNK * N_REP * V7X_LANES * 4)
    return pl.pallas_call(
        functools.partial(_mlstm_kernel, seq=seq),
        out_shape=(jax.ShapeDtypeStruct((b * seq, MIX_WIDTH), BF16),
                   jax.ShapeDtypeStruct((b, HEADS, HEAD_DIM, HEAD_DIM), F32),
                   jax.ShapeDtypeStruct((b, HEADS, HEAD_DIM), F32),
                   jax.ShapeDtypeStruct((b, HEADS, V7X_LANES), F32)),
        grid=(b,),
        in_specs=[pl.BlockSpec((seq, half), lambda i: (i, 1)),
                  pl.BlockSpec((seq, N_GIF), lambda i: (i, 0)),
                  _resident((None, 2 * HEADS, CHUNK), lambda i: (layer, 0, 0))],
        out_specs=(pl.BlockSpec((seq, MIX_WIDTH), lambda i: (i, 0)),
                   pl.BlockSpec((None, HEADS, HEAD_DIM, HEAD_DIM), lambda i: (i, 0, 0, 0)),
                   pl.BlockSpec((None, HEADS, HEAD_DIM), lambda i: (i, 0, 0)),
                   pl.BlockSpec((None, HEADS, V7X_LANES), lambda i: (i, 0, 0))),
        scratch_shapes=[pltpu.VMEM((HEADS, HEAD_DIM, 2 * HEAD_DIM), F32),
                        pltpu.VMEM((HEADS, 1, V7X_LANES), F32),
                        pltpu.VMEM((2, 8 * nch, CHUNK), F32),
                        pltpu.VMEM((2, CHUNK, N_REP * V7X_LANES), F32)],
        compiler_params=_params(blocks, 1),
        name="mlstm",
    )(p, gif, bias)


SEQ_BLOCK_FIRST = 8
SEQ_BLOCK = 16


def _column_expander(sb):
    er = lax.broadcasted_iota(jnp.int32, (sb, sb * V7X_LANES), 0)
    ec = lax.broadcasted_iota(jnp.int32, (sb, sb * V7X_LANES), 1)
    return jnp.where(ec // V7X_LANES == er, 1.0, 0.0).astype(BF16)


def _columns(x, expander):
    return _dot_tn(x.astype(BF16), expander)


def _ret_step_kernel(p_ref, cos_ref, sin_ref, s_ref, *rest, layer, first):
    o_ref, s_out_ref, row_scr = rest[-3:]
    s_out_ref = _own_layer(s_out_ref, layer, first)
    sb = p_ref.shape[0]
    cs = cos_ref[...]
    sn = sin_ref[...]
    expander = _column_expander(sb)
    for h in range(HEADS):
        lo = h * HEAD_DIM
        gamma = math.exp(LOG_GAMMA[h])
        q = _rope(p_ref[:, lo:lo + HEAD_DIM], cs, sn)
        k = _rope(p_ref[:, MIX_WIDTH + lo:MIX_WIDTH + lo + HEAD_DIM], cs, sn) * QK_SCALE
        v = p_ref[:, 2 * MIX_WIDTH + lo:2 * MIX_WIDTH + lo + HEAD_DIM]
        g = p_ref[:, 3 * MIX_WIDTH + lo:3 * MIX_WIDTH + lo + HEAD_DIM]
        qt = _columns(q * gamma, expander)
        kt = _columns(k, expander)
        for b in range(sb):
            s = s_ref[b, h]
            blk = slice(b * V7X_LANES, (b + 1) * V7X_LANES)
            row_scr[b:b + 1, :] = jnp.sum(qt[:, blk] * s, axis=0, keepdims=True)
            s_out_ref[b, h] = gamma * s + kt[:, blk] * v[b:b + 1, :]
        o = jnp.sum(q * k, axis=-1, keepdims=True) * v + row_scr[...]
        o_ref[:, lo:lo + HEAD_DIM] = _head_norm(o) * (g * jax.nn.sigmoid(g))


def _stacked_state(acc, layer, sb):
    tail = (HEADS, HEAD_DIM, HEAD_DIM)
    if acc is None:
        return (), [], pl.BlockSpec((DEPTH, sb) + tail, lambda i: (0, i, 0, 0, 0))
    return (acc,), [pl.BlockSpec(memory_space=pl.ANY)], pl.BlockSpec((None, sb) + tail, lambda i: (layer, i, 0, 0, 0))


def _own_layer(state_out_ref, layer, first):
    if not first:
        return state_out_ref
    for other in range(DEPTH):
        if other != layer:
            state_out_ref[other] = jnp.zeros(state_out_ref.shape[1:], F32)
    return state_out_ref.at[layer]


def _retention_step(p, cos, sin, state, layer, acc):
    nseq = p.shape[0]
    sb = SEQ_BLOCK_FIRST if acc is None else SEQ_BLOCK
    half = 4 * MIX_WIDTH
    sblk = sb * HEADS * HEAD_DIM * HEAD_DIM * 4
    blocks = (2 + 2 * (DEPTH if acc is None else 1)) * sblk + 2 * sb * half * 4 + 2 * sb * MIX_WIDTH * 4
    state_blk = pl.BlockSpec((None, sb, HEADS, HEAD_DIM, HEAD_DIM), lambda i: (layer, i, 0, 0, 0))
    extra, extra_specs, out_state_blk = _stacked_state(acc, layer, sb)
    return pl.pallas_call(
        functools.partial(_ret_step_kernel, layer=layer, first=acc is None),
        out_shape=(jax.ShapeDtypeStruct((nseq, MIX_WIDTH), F32),
                   jax.ShapeDtypeStruct(state.shape, F32)),
        grid=(nseq // sb,),
        in_specs=[pl.BlockSpec((sb, half), lambda i: (i, 0)),
                  _resident((1, HEAD_DIM), lambda i: (0, 0)),
                  _resident((1, HEAD_DIM), lambda i: (0, 0)),
                  state_blk] + extra_specs,
        out_specs=(pl.BlockSpec((sb, MIX_WIDTH), lambda i: (i, 0)), out_state_blk),
        input_output_aliases={4: 1} if extra else {},
        scratch_shapes=[pltpu.VMEM((sb, HEAD_DIM), F32)],
        compiler_params=_params(blocks, 1),
        name="retention_step",
    )(p, cos, sin, state, *extra)


def _mlstm_step_kernel(p_ref, gif_ref, bias_ref, c_ref, n_ref, m_ref, *rest, layer, first):
    o_ref, c_out_ref, n_out_ref, m_out_ref, row_scr = rest[-5:]
    c_out_ref = _own_layer(c_out_ref, layer, first)
    sb = p_ref.shape[0]
    fb = gif_ref[...] + bias_ref[...]
    lf_all = jax.nn.log_sigmoid(fb)
    lane = lax.broadcasted_iota(jnp.int32, (sb, V7X_LANES), 1)
    m_all = m_ref[...]
    m_acc = jnp.zeros((sb, V7X_LANES), F32)
    expander = _column_expander(sb)
    for h in range(HEADS):
        lo = h * HEAD_DIM
        q = p_ref[:, lo:lo + HEAD_DIM]
        k = p_ref[:, MIX_WIDTH + lo:MIX_WIDTH + lo + HEAD_DIM] * QK_SCALE
        v = p_ref[:, 2 * MIX_WIDTH + lo:2 * MIX_WIDTH + lo + HEAD_DIM]
        og = p_ref[:, 3 * MIX_WIDTH + lo:3 * MIX_WIDTH + lo + HEAD_DIM]
        ig = fb[:, h:h + 1]
        inter = lf_all[:, HEADS + h:HEADS + h + 1] + m_all[:, h:h + 1]
        m_t = jnp.maximum(inter, ig)
        w_intra = jnp.exp(ig - m_t)
        w_inter = jnp.exp(inter - m_t)
        nv = n_ref[:, lo:lo + HEAD_DIM]
        kw = k * w_intra
        qt = _columns(q, expander)
        kt = _columns(kw, expander)
        for b in range(sb):
            cm = c_ref[b, h]
            blk = slice(b * V7X_LANES, (b + 1) * V7X_LANES)
            row_scr[b:b + 1, :] = jnp.sum(qt[:, blk] * cm, axis=0, keepdims=True)
            c_out_ref[b, h] = w_inter[b:b + 1, :] * cm + kt[:, blk] * v[b:b + 1, :]
        s = jnp.sum(q * k, axis=-1, keepdims=True) * w_intra
        num = s * v + w_inter * row_scr[...]
        den = s + w_inter * jnp.sum(q * nv, axis=-1, keepdims=True)
        hh = num / jnp.maximum(jnp.abs(den), jnp.exp(-m_t))
        n_out_ref[:, lo:lo + HEAD_DIM] = w_inter * nv + kw
        m_acc = jnp.where(lane == h, m_t, m_acc)
        o_ref[:, lo:lo + HEAD_DIM] = _head_norm(hh) * jax.nn.sigmoid(og)
    m_out_ref[...] = m_acc


def _mlstm_step(p, gif, bias, c_state, n_state, m_state, layer, acc):
    nseq = p.shape[0]
    sb = SEQ_BLOCK_FIRST if acc is None else SEQ_BLOCK
    half = 4 * MIX_WIDTH
    sblk = sb * HEADS * HEAD_DIM * HEAD_DIM * 4
    blocks = (2 + 2 * (DEPTH if acc is None else 1)) * sblk + 2 * sb * half * 4 + 8 * sb * MIX_WIDTH * 4
    rows = lambda w: pl.BlockSpec((sb, w), lambda i: (i, 0))
    state_blk = pl.BlockSpec((None, sb, HEADS, HEAD_DIM, HEAD_DIM), lambda i: (layer, i, 0, 0, 0))
    extra, extra_specs, out_state_blk = _stacked_state(acc, layer, sb)
    return pl.pallas_call(
        functools.partial(_mlstm_step_kernel, layer=layer, first=acc is None),
        out_shape=(jax.ShapeDtypeStruct((nseq, MIX_WIDTH), F32),
                   jax.ShapeDtypeStruct(c_state.shape, F32),
                   jax.ShapeDtypeStruct((nseq, MIX_WIDTH), F32),
                   jax.ShapeDtypeStruct((nseq, V7X_LANES), F32)),
        grid=(nseq // sb,),
        in_specs=[pl.BlockSpec((sb, half), lambda i: (i, 1)),
                  rows(N_GIF),
                  _resident((None, 1, N_GIF), lambda i: (layer, 0, 0)),
                  state_blk,
                  pl.BlockSpec((None, sb, MIX_WIDTH), lambda i: (layer, i, 0)),
                  pl.BlockSpec((None, sb, HEADS), lambda i: (layer, i, 0))] + extra_specs,
        out_specs=(rows(MIX_WIDTH), out_state_blk, rows(MIX_WIDTH), rows(V7X_LANES)),
        input_output_aliases={6: 1} if extra else {},
        scratch_shapes=[pltpu.VMEM((sb, HEAD_DIM), F32)],
        compiler_params=_params(blocks, 1),
        name="mlstm_step",
    )(p, gif, bias, c_state, n_state, m_state, *extra)


def _rope_tables(pos):
    inv = ROPE_THETA ** (-jnp.arange(0, HEAD_DIM, 2, dtype=F32) / HEAD_DIM)
    ang = pos.astype(F32)[:, None] * inv[None, :]
    cos, sin = jnp.cos(ang), jnp.sin(ang)
    return jnp.concatenate([cos, cos], axis=-1), jnp.concatenate([-sin, sin], axis=-1)


W_IN_BLOCKS = N_IN_PADDED // W_IN_ROWS
W_IN_PER_STEP = 2
W_IN_STEPS = W_IN_BLOCKS // W_IN_PER_STEP


def _w_in_source_row(j):
    o = j * W_IN_ROWS
    off_gif = N_MAIN
    off_su = off_gif + 2 * HEADS
    off_gates = off_su + MIX_WIDTH
    return jnp.where(o < OFF_GATES, o,
                     jnp.where(o < OFF_SU, o - OFF_GATES + off_gates,
                               jnp.where(o < OFF_GIF, o - OFF_SU + off_su, off_gif)))


def _w_in_layout_kernel(w_hbm, o_ref, buf, sem):
    l, j = pl.program_id(0), pl.program_id(1)
    step = l * W_IN_STEPS + j
    total = pl.num_programs(0) * W_IN_STEPS
    parts = range(W_IN_PER_STEP)

    def read(s, part):
        src = pl.multiple_of(_w_in_source_row((s % W_IN_STEPS) * W_IN_PER_STEP + part), 8)
        return pltpu.make_async_copy(w_hbm.at[s // W_IN_STEPS, pl.ds(src, W_IN_ROWS), :],
                                     buf.at[s % 2, part], sem.at[s % 2, part])

    @pl.when(step == 0)
    def _():
        for part in parts:
            read(step, part).start()

    @pl.when(step + 1 < total)
    def _():
        for part in parts:
            read(step + 1, part).start()

    for part in parts:
        read(step, part).wait()
        x = buf[step % 2, part]
        if part == W_IN_PER_STEP - 1:
            row = lax.broadcasted_iota(jnp.int32, x.shape, 0)
            x = jnp.where(jnp.logical_or(j < W_IN_STEPS - 1, row < 2 * HEADS), x, 0.0)
        o_ref[part * W_IN_ROWS:(part + 1) * W_IN_ROWS, :] = x.astype(BF16)


def _w_in_layout(w_in):
    depth = w_in.shape[0]
    w_t = jnp.swapaxes(w_in, 1, 2)
    step_rows = W_IN_PER_STEP * W_IN_ROWS
    blocks = 2 * step_rows * D_MODEL * 4 + 2 * step_rows * D_MODEL * 2 + 2 * W_IN_ROWS * D_MODEL * 4
    return pl.pallas_call(
        _w_in_layout_kernel,
        out_shape=jax.ShapeDtypeStruct((depth, N_IN_PADDED, D_MODEL), BF16),
        grid=(depth, W_IN_STEPS),
        in_specs=[pl.BlockSpec(memory_space=pl.ANY)],
        out_specs=pl.BlockSpec((None, step_rows, D_MODEL), lambda l, j: (l, j, 0)),
        scratch_shapes=[pltpu.VMEM((2, W_IN_PER_STEP, W_IN_ROWS, D_MODEL), F32),
                        pltpu.SemaphoreType.DMA((2, W_IN_PER_STEP))],
        compiler_params=_params(blocks, 2),
        name="w_in_layout",
    )(w_t)


def _prepare_weights(w_ffn1_gu, w_ffn1_down, w_in, w_s5_glu, w_branch, w_out, w_ffn2_gu, w_ffn2_down):
    cast = lambda w: w.astype(BF16)
    return dict(gu1=cast(w_ffn1_gu), d1=cast(w_ffn1_down), w_in=_w_in_layout(w_in), glu=cast(w_s5_glu),
                branch=cast(w_branch), out=cast(w_out), gu2=cast(w_ffn2_gu), d2=cast(w_ffn2_down))


def _run_trunk(x, nseq, seq, states, wts, norms, s5w, bias, ln_final, prompt):
    t = nseq * seq
    tm = 512 if t % 512 == 0 else t
    tm_ffn = 1024 if t % 1024 == 0 else tm
    act_dtype = BF16 if prompt else F32
    if prompt:
        cos, sin = _rope_tables(jnp.arange(seq, dtype=jnp.int32))
    else:
        cos, sin = _rope_tables(PAST_LEN + jnp.arange(seq, dtype=jnp.int32))
    if prompt:
        x0 = jnp.zeros((nseq, 2 * S5_N), F32)
    else:
        st_ret, st_mc, st_mn, st_mm, st_sre, st_sim = states
        st_mn = st_mn.reshape(DEPTH, nseq, MIX_WIDTH)
        x0 = jnp.concatenate([st_sre.reshape(DEPTH, nseq, S5_N), st_sim.reshape(DEPTH, nseq, S5_N)], axis=-1)
    rets, mcs, mns, mms, xls = [], [], [], [], []
    ret_acc = mc_acc = None
    for l in range(DEPTH):
        x = _ffn(x, norms["ffn1"], wts["gu1"], wts["d1"], ln_final, l, tm_ffn, final=False)
        main, gates, su, gif = _inproj(x, norms["mix"], wts["w_in"], l, tm, act_dtype)
        if prompt:
            r_out, new_ret = _retention_prompt(main, cos, sin, nseq, seq)
            m_out, new_mc, new_mn, new_mm = _mlstm_prompt(main, gif, bias, l, nseq, seq)
            s_out, x_last = _s5_prompt(su.reshape(nseq, seq, MIX_WIDTH), x0, s5w, l)
            rets.append(new_ret)
            mcs.append(new_mc)
        else:
            r_out, ret_acc = _retention_step(main, cos, sin, st_ret, l, ret_acc)
            m_out, mc_acc, new_mn, new_mm = _mlstm_step(main, gif, bias, st_mc, st_mn, st_mm, l, mc_acc)
            s_out, x_last = _s5_step(su, x0, s5w, l)
        mns.append(new_mn)
        mms.append(new_mm)
        xls.append(x_last)
        x = _merge_ffn(x, r_out, m_out, s_out, gates, wts["branch"], wts["out"], norms["ffn2"], wts["gu2"], wts["d2"],
                       ln_final, l, tm, final=(l == DEPTH - 1))
    new_ret, new_mc = (jnp.stack(rets), jnp.stack(mcs)) if prompt else (ret_acc, mc_acc)
    new_mn = jnp.stack(mns).reshape(DEPTH, nseq, HEADS, HEAD_DIM)
    new_mm = jnp.stack(mms)[:, :, :, 0] if prompt else jnp.stack(mms)[:, :, :HEADS]
    xl = jnp.stack(xls)
    new_sre = xl[:, :, :S5_N].reshape(DEPTH, nseq, S5_GROUPS, S5_STATE)
    new_sim = xl[:, :, S5_N:].reshape(DEPTH, nseq, S5_GROUPS, S5_STATE)
    return x, (new_ret, new_mc, new_mn, new_mm, new_sre, new_sim)


def kernel(x_prompt, x_sample, state_ret, state_mlstm_c, state_mlstm_n, state_mlstm_m, state_s5_re, state_s5_im, ln_ffn1, w_ffn1_gu, w_ffn1_down, ln_mix, w_in, b_gates, s5_a_re, s5_a_im, s5_b_re, s5_b_im, s5_c_re, s5_c_im, s5_d, s5_log_dt, w_s5_glu, w_branch, w_out, ln_ffn2, w_ffn2_gu, w_ffn2_down, ln_final):
    wts = _prepare_weights(w_ffn1_gu, w_ffn1_down, w_in, w_s5_glu, w_branch, w_out, w_ffn2_gu, w_ffn2_down)
    norms = dict(ffn1=ln_ffn1.reshape(DEPTH, 1, D_MODEL), mix=ln_mix.reshape(DEPTH, 1, D_MODEL),
                 ffn2=ln_ffn2.reshape(DEPTH, 1, D_MODEL))
    lnf = ln_final.reshape(1, D_MODEL)
    a_row, bbr, bbi = _s5_discretise(s5_a_re, s5_a_im, s5_log_dt, s5_b_re, s5_b_im)
    s5w = (a_row, _block_diag_in(bbr), _block_diag_in(bbi), _block_diag_out(s5_c_re), _block_diag_out(-s5_c_im),
           s5_d.reshape(DEPTH, 1, MIX_WIDTH), wts["glu"])
    bias = jnp.pad(b_gates, ((0, 0), (0, N_GIF - 2 * HEADS))).reshape(DEPTH, 1, N_GIF)
    bias_rows = jnp.broadcast_to(b_gates[:, :, None], (DEPTH, 2 * HEADS, CHUNK))

    pb, pl_len, _ = x_prompt.shape
    sb, sl, _ = x_sample.shape
    y_p, st_p = _run_trunk(x_prompt.reshape(pb * pl_len, D_MODEL), pb, pl_len, None, wts, norms, s5w, bias_rows, lnf, True)
    sample_states = (state_ret, state_mlstm_c, state_mlstm_n, state_mlstm_m, state_s5_re, state_s5_im)
    y_s, st_s = _run_trunk(x_sample.reshape(sb * sl, D_MODEL), sb, sl, sample_states, wts, norms, s5w, bias, lnf, False)
    return (y_p.reshape(pb, pl_len, D_MODEL), y_s.reshape(sb, sl, D_MODEL)) + st_p + st_s
```

```python
import functools
import math

import jax
import jax.numpy as jnp
from jax import lax
from jax.experimental import pallas as pl
from jax.experimental.pallas import tpu as pltpu

F32 = jnp.float32
BF16 = jnp.bfloat16

D_MODEL = 1024
DEPTH = 4
MIX_WIDTH = D_MODEL // 2
N_BRANCH = 3
HEADS = 4
HEAD_DIM = MIX_WIDTH // HEADS
S5_GROUP = 16
S5_GROUPS = MIX_WIDTH // S5_GROUP
S5_STATE = 64
S5_N = S5_GROUPS * S5_STATE
D_FF = 2816
CHUNK = 128
PAST_LEN = 16384
ROPE_THETA = 10000.0
EPS = 1e-6

N_MAIN = 8 * MIX_WIDTH
N_GATES = N_BRANCH * D_MODEL
N_GIF = 128
W_IN_ROWS = 512
N_IN_PADDED = N_MAIN + N_GATES + MIX_WIDTH + W_IN_ROWS
OFF_GATES = N_MAIN
OFF_SU = N_MAIN + N_GATES
OFF_GIF = OFF_SU + MIX_WIDTH

V7X_LANES = 128
V7X_MXU_DIM = 256
V7X_VMEM_BYTES = 64 * 1024 * 1024

FF_CHUNK = V7X_MXU_DIM
PROJ_CHUNK = 2 * V7X_MXU_DIM
S5_HALF = MIX_WIDTH // 2
S5_HALF_N = S5_N // 2
S5_TOKENS = 64
LOG_GAMMA = tuple(math.log1p(-(2.0 ** (-5.0 - h))) for h in range(HEADS))
QK_SCALE = HEAD_DIM ** -0.5


VMEM_TEMPORARIES_BYTES = 8 << 20
VMEM_TEMPORARIES_SCALE = 1.25
V7X_VMEM_RESERVED_BYTES = 6 << 20


def _vmem_limit(block_bytes):
    want = int(block_bytes * VMEM_TEMPORARIES_SCALE) + VMEM_TEMPORARIES_BYTES
    return min(want, V7X_VMEM_BYTES - V7X_VMEM_RESERVED_BYTES)


def _params(block_bytes, n_grid):
    return pltpu.CompilerParams(
        dimension_semantics=("arbitrary",) * n_grid,
        vmem_limit_bytes=_vmem_limit(block_bytes))


def _resident(shape, index_map):
    return pl.BlockSpec(shape, index_map, pipeline_mode=pl.Buffered(1))


def _dot(a, b):
    return jnp.dot(a, b, preferred_element_type=F32)


def _dot_nt(a, b):
    return lax.dot_general(a, b, (((1,), (1,)), ((), ())), preferred_element_type=F32)


def _dot_tn(a, b):
    return lax.dot_general(a, b, (((0,), (0,)), ((), ())), preferred_element_type=F32)


def _rms(x, g):
    return x * lax.rsqrt(jnp.mean(x * x, axis=-1, keepdims=True) + EPS) * g


def _head_norm(x):
    mu = jnp.mean(x, axis=-1, keepdims=True)
    xc = x - mu
    var = jnp.mean(xc * xc, axis=-1, keepdims=True)
    return xc * lax.rsqrt(var + EPS)


def _rope(x, cos, sin_signed):
    return x * cos + pltpu.roll(x, HEAD_DIM // 2, 1) * sin_signed


def _ffn_tile(x, ln_ref, wgu_ref, wd_ref, lnf_ref, o_ref, acc_ref, final):
    xn = _rms(x, ln_ref[...]).astype(BF16)
    for c in range(D_FF // FF_CHUNK):
        lo = c * FF_CHUNK
        g = _dot(xn, wgu_ref[:, lo:lo + FF_CHUNK])
        u = _dot(xn, wgu_ref[:, D_FF + lo:D_FF + lo + FF_CHUNK])
        a = (g * jax.nn.sigmoid(g) * u).astype(BF16)
        d = _dot(a, wd_ref[lo:lo + FF_CHUNK, :])
        if c == 0:
            acc_ref[...] = d
        else:
            acc_ref[...] += d
    y = x + 0.5 * acc_ref[...]
    if final:
        y = _rms(y, lnf_ref[...])
    o_ref[...] = y


def _ffn_kernel(x_ref, ln_ref, wgu_ref, wd_ref, lnf_ref, o_ref, acc_ref, *, final):
    _ffn_tile(x_ref[...], ln_ref, wgu_ref, wd_ref, lnf_ref, o_ref, acc_ref, final)


def _ffn(x, ln, wgu, wd, lnf, layer, tm, final):
    t = x.shape[0]
    blocks = (4 * tm * D_MODEL * 4 + tm * D_MODEL * 4 + D_MODEL * 2 * D_FF * 2 + D_FF * D_MODEL * 2
              + 4 * tm * FF_CHUNK * 4)
    return pl.pallas_call(
        functools.partial(_ffn_kernel, final=final),
        out_shape=jax.ShapeDtypeStruct((t, D_MODEL), F32),
        grid=(t // tm,),
        in_specs=[
            pl.BlockSpec((tm, D_MODEL), lambda i: (i, 0)),
            _resident((None, 1, D_MODEL), lambda i: (layer, 0, 0)),
            _resident((None, D_MODEL, 2 * D_FF), lambda i: (layer, 0, 0)),
            _resident((None, D_FF, D_MODEL), lambda i: (layer, 0, 0)),
            _resident((1, D_MODEL), lambda i: (0, 0)),
        ],
        out_specs=pl.BlockSpec((tm, D_MODEL), lambda i: (i, 0)),
        scratch_shapes=[pltpu.VMEM((tm, D_MODEL), F32)],
        compiler_params=_params(blocks, 1),
        name="ffn",
    )(x, ln, wgu, wd, lnf)


def _inproj_kernel(x_ref, ln_ref, w_ref, main_ref, gates_ref, su_ref, gif_ref):
    xn = _rms(x_ref[...], ln_ref[...]).astype(BF16)
    for c in range(N_MAIN // PROJ_CHUNK):
        lo = c * PROJ_CHUNK
        main_ref[:, lo:lo + PROJ_CHUNK] = _dot_nt(xn, w_ref[lo:lo + PROJ_CHUNK, :]).astype(main_ref.dtype)
    for c in range(N_GATES // PROJ_CHUNK):
        lo = c * PROJ_CHUNK
        gates_ref[:, lo:lo + PROJ_CHUNK] = _dot_nt(
            xn, w_ref[OFF_GATES + lo:OFF_GATES + lo + PROJ_CHUNK, :]).astype(gates_ref.dtype)
    su_ref[...] = _dot_nt(xn, w_ref[OFF_SU:OFF_SU + MIX_WIDTH, :])
    gif_ref[...] = _dot_nt(xn, w_ref[OFF_GIF:OFF_GIF + N_GIF, :])


def _inproj(x, ln, w, layer, tm, act_dtype):
    t = x.shape[0]
    ab = jnp.dtype(act_dtype).itemsize
    blocks = (2 * tm * D_MODEL * 4 + D_MODEL * N_IN_PADDED * 2
              + 2 * tm * (N_MAIN + N_GATES) * ab + 2 * tm * (MIX_WIDTH + N_GIF) * 4
              + 4 * tm * PROJ_CHUNK * 4)
    return pl.pallas_call(
        _inproj_kernel,
        out_shape=(jax.ShapeDtypeStruct((t, N_MAIN), act_dtype),
                   jax.ShapeDtypeStruct((t, N_GATES), act_dtype),
                   jax.ShapeDtypeStruct((t, MIX_WIDTH), F32),
                   jax.ShapeDtypeStruct((t, N_GIF), F32)),
        grid=(t // tm,),
        in_specs=[
            pl.BlockSpec((tm, D_MODEL), lambda i: (i, 0)),
            _resident((None, 1, D_MODEL), lambda i: (layer, 0, 0)),
            _resident((None, N_IN_PADDED, D_MODEL), lambda i: (layer, 0, 0)),
        ],
        out_specs=(pl.BlockSpec((tm, N_MAIN), lambda i: (i, 0)),
                   pl.BlockSpec((tm, N_GATES), lambda i: (i, 0)),
                   pl.BlockSpec((tm, MIX_WIDTH), lambda i: (i, 0)),
                   pl.BlockSpec((tm, N_GIF), lambda i: (i, 0))),
        compiler_params=_params(blocks, 1),
        name="inproj",
    )(x, ln, w)


def _merge_ffn_kernel(x_ref, r_ref, m_ref, s_ref, g_ref, wb_ref, wo_ref, ln_ref, wgu_ref, wd_ref, lnf_ref,
                      o_ref, acc_ref, *, final):
    merged = None
    for n, b_ref in enumerate((r_ref, m_ref, s_ref)):
        up = _dot(b_ref[...].astype(BF16), wb_ref[n])
        gate = jax.nn.sigmoid(g_ref[:, n * D_MODEL:(n + 1) * D_MODEL].astype(F32))
        merged = gate * up if merged is None else merged + gate * up
    x1 = x_ref[...] + _dot(merged.astype(BF16), wo_ref[...])
    _ffn_tile(x1, ln_ref, wgu_ref, wd_ref, lnf_ref, o_ref, acc_ref, final)


def _merge_ffn(x, r, m, s, gates, wb, wo, ln, wgu, wd, lnf, layer, tm, final):
    t = x.shape[0]
    blocks = (4 * tm * D_MODEL * 4 + 6 * tm * MIX_WIDTH * 4 + 2 * tm * N_GATES * 4
              + N_BRANCH * MIX_WIDTH * D_MODEL * 2 + D_MODEL * D_MODEL * 2 + 4 * tm * D_MODEL * 4
              + tm * D_MODEL * 4 + D_MODEL * 2 * D_FF * 2 + D_FF * D_MODEL * 2 + 4 * tm * FF_CHUNK * 4)
    row = lambda w: pl.BlockSpec((tm, w), lambda i: (i, 0))
    if s.ndim == 3:
        assert s.shape[1] % tm == 0
        per_seq = s.shape[1] // tm
        s_spec = pl.BlockSpec((None, tm, MIX_WIDTH), lambda i: (i // per_seq, i % per_seq, 0))
    else:
        s_spec = row(MIX_WIDTH)
    return pl.pallas_call(
        functools.partial(_merge_ffn_kernel, final=final),
        out_shape=jax.ShapeDtypeStruct((t, D_MODEL), F32),
        grid=(t // tm,),
        in_specs=[row(D_MODEL), row(MIX_WIDTH), row(MIX_WIDTH), s_spec, row(N_GATES),
                  _resident((None, N_BRANCH, MIX_WIDTH, D_MODEL), lambda i: (layer, 0, 0, 0)),
                  _resident((None, D_MODEL, D_MODEL), lambda i: (layer, 0, 0)),
                  _resident((None, 1, D_MODEL), lambda i: (layer, 0, 0)),
                  _resident((None, D_MODEL, 2 * D_FF), lambda i: (layer, 0, 0)),
                  _resident((None, D_FF, D_MODEL), lambda i: (layer, 0, 0)),
                  _resident((1, D_MODEL), lambda i: (0, 0))],
        out_specs=row(D_MODEL),
        scratch_shapes=[pltpu.VMEM((tm, D_MODEL), F32)],
        compiler_params=_params(blocks, 1),
        name="merge_ffn",
    )(x, r, m, s, gates, wb, wo, ln, wgu, wd, lnf)


def _s5_disc_kernel(are_ref, aim_ref, ldt_ref, bre_ref, bim_ref, abr_ref, abi_ref, bbr_ref, bbi_ref):
    ar, ai = are_ref[...], aim_ref[...]
    dt = jnp.exp(ldt_ref[...])
    mag = jnp.exp(ar * dt)
    abr = mag * jnp.cos(ai * dt)
    abi = mag * jnp.sin(ai * dt)
    nr, ni = abr - 1.0, abi
    den = ar * ar + ai * ai
    cr = (nr * ar + ni * ai) / den
    ci = (ni * ar - nr * ai) / den
    br, bi = bre_ref[...], bim_ref[...]
    abr_ref[...] = abr
    abi_ref[...] = abi
    bbr_ref[...] = cr * br - ci * bi
    bbi_ref[...] = cr * bi + ci * br


def _s5_discretise(a_re, a_im, log_dt, b_re, b_im):
    rows = DEPTH * S5_N
    col = lambda a: a.reshape(rows, 1)
    ldt = jnp.broadcast_to(log_dt[:, :, None], (DEPTH, S5_GROUPS, S5_STATE))
    outs = pl.pallas_call(
        _s5_disc_kernel,
        out_shape=(jax.ShapeDtypeStruct((rows, 1), F32), jax.ShapeDtypeStruct((rows, 1), F32),
                   jax.ShapeDtypeStruct((rows, S5_GROUP), F32), jax.ShapeDtypeStruct((rows, S5_GROUP), F32)),
        name="s5_disc",
    )(col(a_re), col(a_im), col(ldt), b_re.reshape(rows, S5_GROUP), b_im.reshape(rows, S5_GROUP))
    abr, abi, bbr, bbi = outs
    a_row = jnp.concatenate([abr.reshape(DEPTH, 1, S5_N), abi.reshape(DEPTH, 1, S5_N)], axis=-1)
    return a_row, bbr.reshape(DEPTH, S5_GROUPS, S5_STATE, S5_GROUP), bbi.reshape(DEPTH, S5_GROUPS, S5_STATE, S5_GROUP)


def _block_diag_in(b):
    gh = S5_GROUPS // 2
    on_diag = jnp.eye(gh, dtype=bool)[None, None, :, None, :, None]
    src = b.reshape(DEPTH, 2, gh, S5_STATE, S5_GROUP).transpose(0, 1, 2, 4, 3)
    m = jnp.where(on_diag, src[:, :, :, :, None, :], 0.0)
    return m.reshape(DEPTH, 2, S5_HALF, S5_HALF_N).astype(BF16)


def _block_diag_out(c):
    gh = S5_GROUPS // 2
    on_diag = jnp.eye(gh, dtype=bool)[None, None, :, None, :, None]
    src = c.reshape(DEPTH, 2, gh, S5_GROUP, S5_STATE).transpose(0, 1, 4, 2, 3)
    m = jnp.where(on_diag, src[:, :, None, :, :, :], 0.0)
    return m.reshape(DEPTH, 2, S5_HALF_N, S5_HALF).astype(BF16)


def _s5_project_in(u, xs_ref, bre_ref, bim_ref):
    ub = u.astype(BF16)
    for k in range(2):
        uk = ub[:, k * S5_HALF:(k + 1) * S5_HALF]
        xs_ref[:, k * S5_HALF_N:(k + 1) * S5_HALF_N] = _dot(uk, bre_ref[k])
        xs_ref[:, S5_N + k * S5_HALF_N:S5_N + (k + 1) * S5_HALF_N] = _dot(uk, bim_ref[k])


def _s5_project_out(xs_ref, u, cre_ref, cimn_ref, d_ref, wglu_ref):
    ys = []
    for k in range(2):
        xr = xs_ref[:, k * S5_HALF_N:(k + 1) * S5_HALF_N].astype(BF16)
        xi = xs_ref[:, S5_N + k * S5_HALF_N:S5_N + (k + 1) * S5_HALF_N].astype(BF16)
        ys.append(_dot(xr, cre_ref[k]) + _dot(xi, cimn_ref[k]))
    y = jnp.concatenate(ys, axis=1) + d_ref[...] * u
    y = jax.nn.gelu(y)
    return y * jax.nn.sigmoid(_dot(y.astype(BF16), wglu_ref[...]))


def _s5_scan_step(ab_ref, xs_ref, rows, xre, xim):
    are = ab_ref[:, :S5_N]
    aim = ab_ref[:, S5_N:]
    nre = are * xre - aim * xim + xs_ref[rows, :S5_N]
    nim = are * xim + aim * xre + xs_ref[rows, S5_N:]
    xs_ref[rows, :S5_N] = nre
    xs_ref[rows, S5_N:] = nim
    return nre, nim


def _s5_step_kernel(u_ref, x0_ref, a_ref, bre_ref, bim_ref, cre_ref, cimn_ref, d_ref, wglu_ref,
                    o_ref, xl_ref, xs_scr):
    u = u_ref[...]
    _s5_project_in(u, xs_scr, bre_ref, bim_ref)
    xre, xim = _s5_scan_step(a_ref, xs_scr, slice(None), x0_ref[:, :S5_N], x0_ref[:, S5_N:])
    xl_ref[:, :S5_N] = xre
    xl_ref[:, S5_N:] = xim
    o_ref[...] = _s5_project_out(xs_scr, u, cre_ref, cimn_ref, d_ref, wglu_ref)


def _s5_prompt_kernel(u_ref, un_ref, x0_ref, a_ref, bre_ref, bim_ref, cre_ref, cimn_ref, d_ref, wglu_ref,
                      o_ref, xl_ref, xs0, xs1, ut0, ut1, st_scr, ab_scr, p_in, p_out, *, nb, tl):
    g = pl.program_id(0)
    m = nb * tl

    tg = V7X_MXU_DIM // nb
    rows_g = nb * tg

    def to_token_major(ref, first):
        pm = p_in[...]
        groups = []
        for lo_t in range(first, first + tl, tg):
            us = ref[:, lo_t:lo_t + tg, :].reshape(rows_g, MIX_WIDTH)
            hi = us.astype(BF16)
            lo = (us - hi.astype(F32)).astype(BF16)
            groups.append(_dot(pm, hi) + _dot(pm, lo))
        return jnp.concatenate(groups, axis=0)

    def store_sequence_major(out, first):
        pm = p_out[...]
        for gq in range(tl // tg):
            piece = _dot(pm, out[gq * rows_g:(gq + 1) * rows_g].astype(BF16)).astype(o_ref.dtype)
            o_ref[:, first + gq * tg:first + (gq + 1) * tg, :] = piece.reshape(nb, tg, MIX_WIDTH)

    def scan(xs, xre, xim):
        for t in range(tl):
            xre, xim = _s5_scan_step(ab_scr, xs, slice(t * nb, (t + 1) * nb), xre, xim)
        return xre, xim

    @pl.when(g == 0)
    def _():
        st_scr[...] = x0_ref[...]
        ab_scr[...] = jnp.broadcast_to(a_ref[...], ab_scr.shape)
        shift = nb.bit_length() - 1
        ro = lax.broadcasted_iota(jnp.int32, (rows_g, rows_g), 0)
        ci = lax.broadcasted_iota(jnp.int32, (rows_g, rows_g), 1)
        seq_major = lambda r: (r & (nb - 1)) * tg + (r >> shift)
        p_in[...] = jnp.where(ci == seq_major(ro), 1.0, 0.0).astype(BF16)
        p_out[...] = jnp.where(ro == seq_major(ci), 1.0, 0.0).astype(BF16)
        u0 = to_token_major(u_ref, 0)
        ut0[...] = u0
        _s5_project_in(u0, xs0, bre_ref, bim_ref)

    u1 = to_token_major(u_ref, tl)
    ut1[...] = u1
    _s5_project_in(u1, xs1, bre_ref, bim_ref)
    xre, xim = scan(xs0, st_scr[:, :S5_N], st_scr[:, S5_N:])
    store_sequence_major(_s5_project_out(xs0, ut0[...], cre_ref, cimn_ref, d_ref, wglu_ref), 0)
    xre, xim = scan(xs1, xre, xim)
    st_scr[:, :S5_N] = xre
    st_scr[:, S5_N:] = xim
    un = to_token_major(un_ref, 0)
    ut0[...] = un
    _s5_project_in(un, xs0, bre_ref, bim_ref)
    store_sequence_major(_s5_project_out(xs1, u1, cre_ref, cimn_ref, d_ref, wglu_ref), tl)
    xl_ref[...] = st_scr[...]


def _s5_weight_specs(layer):
    lay3 = lambda i: (layer, 0, 0)
    lay4 = lambda i: (layer, 0, 0, 0)
    return [_resident((None, 1, 2 * S5_N), lay3),
            _resident((None, 2, S5_HALF, S5_HALF_N), lay4),
            _resident((None, 2, S5_HALF, S5_HALF_N), lay4),
            _resident((None, 2, S5_HALF_N, S5_HALF), lay4),
            _resident((None, 2, S5_HALF_N, S5_HALF), lay4),
            _resident((None, 1, MIX_WIDTH), lay3),
            _resident((None, MIX_WIDTH, MIX_WIDTH), lay3)]


S5_WEIGHT_BYTES = 4 * S5_HALF * S5_HALF_N * 2 * 2 + MIX_WIDTH * MIX_WIDTH * 2 + (2 * S5_N + MIX_WIDTH) * 4


def _s5_step(u, x0, s5w, layer):
    nb = u.shape[0]
    blocks = 4 * nb * MIX_WIDTH * 4 + 5 * nb * 2 * S5_N * 4 + S5_WEIGHT_BYTES + 6 * nb * MIX_WIDTH * 4
    whole = lambda w: pl.BlockSpec((nb, w), lambda i: (0, 0))
    return pl.pallas_call(
        _s5_step_kernel,
        out_shape=(jax.ShapeDtypeStruct(u.shape, F32), jax.ShapeDtypeStruct((nb, 2 * S5_N), F32)),
        grid=(1,),
        in_specs=[whole(MIX_WIDTH), pl.BlockSpec((None, nb, 2 * S5_N), lambda i: (layer, 0, 0))] + _s5_weight_specs(layer),
        out_specs=(whole(MIX_WIDTH), whole(2 * S5_N)),
        scratch_shapes=[pltpu.VMEM((nb, 2 * S5_N), F32)],
        compiler_params=_params(blocks, 1),
        name="s5_step",
    )(u, x0, *s5w)


def _s5_prompt(u, x0, s5w, layer):
    nb, seq, _ = u.shape
    tl = S5_TOKENS
    m = nb * tl
    steps = seq // (2 * tl)
    perm_rows = V7X_MXU_DIM
    assert nb & (nb - 1) == 0 and seq % (2 * tl) == 0 and m % perm_rows == 0
    blocks = (2 * 2 * m * MIX_WIDTH * 4 + 2 * m * MIX_WIDTH * 4 + 2 * 2 * m * MIX_WIDTH * 2 + 2 * m * 2 * S5_N * 4
              + 2 * m * MIX_WIDTH * 4 + 6 * nb * 2 * S5_N * 4 + S5_WEIGHT_BYTES + 2 * perm_rows * perm_rows * 2
              + m * 2 * S5_N * 2 + 6 * m * MIX_WIDTH * 4)
    pair = pl.BlockSpec((nb, 2 * tl, MIX_WIDTH), lambda i: (0, i, 0))
    nxt = pl.BlockSpec((nb, tl, MIX_WIDTH), lambda i: (0, jnp.minimum(2 * i + 2, 2 * steps - 1), 0))
    state = pl.BlockSpec((nb, 2 * S5_N), lambda i: (0, 0))
    return pl.pallas_call(
        functools.partial(_s5_prompt_kernel, nb=nb, tl=tl),
        out_shape=(jax.ShapeDtypeStruct(u.shape, BF16), jax.ShapeDtypeStruct((nb, 2 * S5_N), F32)),
        grid=(steps,),
        in_specs=[pair, nxt, _resident((nb, 2 * S5_N), lambda i: (0, 0))] + _s5_weight_specs(layer),
        out_specs=(pair, state),
        scratch_shapes=[pltpu.VMEM((m, 2 * S5_N), F32), pltpu.VMEM((m, 2 * S5_N), F32),
                        pltpu.VMEM((m, MIX_WIDTH), F32), pltpu.VMEM((m, MIX_WIDTH), F32),
                        pltpu.VMEM((nb, 2 * S5_N), F32), pltpu.VMEM((nb, 2 * S5_N), F32),
                        pltpu.VMEM((perm_rows, perm_rows), BF16), pltpu.VMEM((perm_rows, perm_rows), BF16)],
        compiler_params=_params(blocks, 1),
        name="s5",
    )(u, u, x0, *s5w)


def _ret_kernel(p_ref, cos_ref, sin_ref, o_ref, s_out_ref, s_scr, *, seq):
    c = CHUNK
    heads = range(HEADS)
    row = lax.broadcasted_iota(jnp.int32, (c, c), 0)
    col = lax.broadcasted_iota(jnp.int32, (c, c), 1)
    diff = (row - col).astype(F32)
    ivec = lax.broadcasted_iota(jnp.int32, (c, HEAD_DIM), 0).astype(F32)
    dmats = [jnp.where(diff >= 0, jnp.exp(lg * jnp.maximum(diff, 0.0)), 0.0) * QK_SCALE for lg in LOG_GAMMA]
    q_decs = [jnp.exp(lg * (ivec + 1.0)) for lg in LOG_GAMMA]
    k_decs = [jnp.exp(lg * (c - 1.0 - ivec)) * QK_SCALE for lg in LOG_GAMMA]
    s_scr[...] = jnp.zeros_like(s_scr)

    def chunk(j, carry):
        r = pl.ds(pl.multiple_of(j * c, c), c)
        cs = cos_ref[r, :]
        sn = sin_ref[r, :]
        sl = lambda part, h: p_ref[r, part * MIX_WIDTH + h * HEAD_DIM:part * MIX_WIDTH + (h + 1) * HEAD_DIM]
        q = [_rope(sl(0, h).astype(F32), cs, sn) for h in heads]
        k = [_rope(sl(1, h).astype(F32), cs, sn) for h in heads]
        v = [sl(2, h).astype(BF16) for h in heads]
        s = [s_scr[h] for h in heads]
        qb = [q[h].astype(BF16) for h in heads]
        kb = [k[h].astype(BF16) for h in heads]
        att = [_dot_nt(qb[h], kb[h]) for h in heads]
        inter = [_dot((q[h] * q_decs[h]).astype(BF16), s[h].astype(BF16)) for h in heads]
        upd = [_dot_tn((k[h] * k_decs[h]).astype(BF16), v[h]) for h in heads]
        attb = [(att[h] * dmats[h]).astype(BF16) for h in heads]
        o = [_dot(attb[h], v[h]) + inter[h] for h in heads]
        for h in heads:
            s_scr[h] = s[h] * math.exp(LOG_GAMMA[h] * c) + upd[h]
        on = [_head_norm(o[h]) for h in heads]
        for h in heads:
            g = sl(3, h).astype(F32)
            o_ref[r, h * HEAD_DIM:(h + 1) * HEAD_DIM] = (on[h] * (g * jax.nn.sigmoid(g))).astype(o_ref.dtype)
        return carry

    lax.fori_loop(0, seq // c, chunk, 0, unroll=2)
    s_out_ref[...] = s_scr[...]


def _retention_prompt(p, cos, sin, b, seq):
    half = 4 * MIX_WIDTH
    blocks = 2 * seq * half * 2 + 2 * seq * V7X_LANES * 4 + 2 * seq * MIX_WIDTH * 2 + 3 * HEADS * HEAD_DIM * HEAD_DIM * 4
    return pl.pallas_call(
        functools.partial(_ret_kernel, seq=seq),
        out_shape=(jax.ShapeDtypeStruct((b * seq, MIX_WIDTH), BF16),
                   jax.ShapeDtypeStruct((b, HEADS, HEAD_DIM, HEAD_DIM), F32)),
        grid=(b,),
        in_specs=[pl.BlockSpec((seq, half), lambda i: (i, 0)),
                  _resident((seq, HEAD_DIM), lambda i: (0, 0)),
                  _resident((seq, HEAD_DIM), lambda i: (0, 0))],
        out_specs=(pl.BlockSpec((seq, MIX_WIDTH), lambda i: (i, 0)),
                   pl.BlockSpec((None, HEADS, HEAD_DIM, HEAD_DIM), lambda i: (i, 0, 0, 0))),
        scratch_shapes=[pltpu.VMEM((HEADS, HEAD_DIM, HEAD_DIM), F32)],
        compiler_params=_params(blocks, 1),
        name="retention",
    )(p, cos, sin)


N_REP = 3 * HEADS


def _split_dot(x, w):
    hi = x.astype(BF16)
    lo = (x - hi.astype(F32)).astype(BF16)
    return _dot(hi, w) + _dot(lo, w)


def _split_dot_tn(x, w):
    hi = x.astype(BF16)
    lo = (x - hi.astype(F32)).astype(BF16)
    return _dot_tn(hi, w) + _dot_tn(lo, w)


def _mlstm_kernel(p_ref, gif_ref, bias_ref, o_ref, c_out_ref, n_out_ref, m_out_ref,
                  cn_scr, m_scr, rows_scr, rep_scr, *, seq):
    c = CHUNK
    nch = seq // c
    nrow = 8 * nch
    heads = range(HEADS)
    neg_inf = float("-inf")
    row = lax.broadcasted_iota(jnp.int32, (c, c), 0)
    col = lax.broadcasted_iota(jnp.int32, (c, c), 1)
    causal = row >= col
    triu = jnp.where(row <= col, 1.0, 0.0).astype(BF16)
    ones = jnp.ones((c, V7X_LANES), BF16)
    er = lax.broadcasted_iota(jnp.int32, (16, N_REP * V7X_LANES), 0)
    ec = lax.broadcasted_iota(jnp.int32, (16, N_REP * V7X_LANES), 1)
    expand = jnp.where(ec // V7X_LANES == er, 1.0, 0.0).astype(BF16)
    cn_scr[...] = jnp.zeros_like(cn_scr)
    m_scr[...] = jnp.zeros_like(m_scr)

    pieces = []
    sub8 = lax.broadcasted_iota(jnp.int32, (8, c), 0)
    for j in range(nch):
        ft = gif_ref[j * c:(j + 1) * c, :].T[0:8, :] + bias_ref[...]
        pieces.append(jnp.where(sub8 < HEADS, ft, jax.nn.log_sigmoid(ft)))
    x = jnp.concatenate(pieces, axis=0)
    cum = _split_dot(x, triu)
    sub = lax.broadcasted_iota(jnp.int32, (nrow, c), 0) % 8
    tok = lax.broadcasted_iota(jnp.int32, (nrow, c), 1)
    a = x - pltpu.roll(cum, nrow - HEADS, 0)
    d = 1
    while d < c:
        a = jnp.maximum(a, jnp.where(tok >= d, pltpu.roll(a, d, 1), neg_inf))
        d *= 2
    rows_scr[0] = jnp.where(sub < HEADS, x, cum)
    rows_scr[1] = a

    def replicate(j):
        r8 = pl.ds(pl.multiple_of(j * 8, 8), 8)
        r16 = jnp.concatenate([rows_scr[0, r8, :], rows_scr[1, r8, :]], axis=0)
        return _split_dot_tn(r16, expand)

    rep_scr[0] = replicate(0)

    def chunk(j, carry):
        r = pl.ds(pl.multiple_of(j * c, c), c)
        r8 = pl.ds(pl.multiple_of(j * 8, 8), 8)
        slot = j % 2
        rep_next = replicate(jnp.minimum(j + 1, nch - 1))
        sl = lambda part, h: p_ref[r, part * MIX_WIDTH + h * HEAD_DIM:part * MIX_WIDTH + (h + 1) * HEAD_DIM]
        blk = lambda n, h: rep_scr[slot, :, (n * HEADS + h) * V7X_LANES:(n * HEADS + h + 1) * V7X_LANES]
        y8 = rows_scr[0, r8, :]
        q = [sl(0, h).astype(BF16) for h in heads]
        k = [sl(1, h).astype(BF16) for h in heads]
        v1 = [jnp.concatenate([sl(2, h).astype(BF16), ones], axis=1) for h in heads]
        cn = [cn_scr[h] for h in heads]
        m_prev = [m_scr[h] for h in heads]
        ic = [blk(0, h) for h in heads]
        bc = [blk(1, h) for h in heads]
        cmc = [blk(2, h) for h in heads]
        qk = [_dot_nt(q[h], k[h]) for h in heads]
        qcn = [_dot(q[h], cn[h].astype(BF16)) for h in heads]
        b_last = [bc[h][c - 1:c, :] for h in heads]
        wk_log = [b_last[h] - bc[h] + ic[h] for h in heads]
        m_new = [jnp.maximum(b_last[h] + m_prev[h], jnp.max(wk_log[h], axis=0, keepdims=True)) for h in heads]
        wk = [jnp.exp(wk_log[h] - m_new[h]) * QK_SCALE for h in heads]
        decay = [jnp.exp(b_last[h] + m_prev[h] - m_new[h]) for h in heads]
        kw = [(k[h].astype(F32) * wk[h]).astype(BF16) for h in heads]
        upd = [_dot_tn(kw[h], v1[h]) for h in heads]
        m_t = [bc[h] + jnp.maximum(m_prev[h], cmc[h]) for h in heads]
        dlog = [jnp.where(causal, bc[h] - y8[HEADS + h:HEADS + h + 1, :] + y8[h:h + 1, :], neg_inf) for h in heads]
        w_intra = [jnp.exp(dlog[h] - m_t[h]) * QK_SCALE for h in heads]
        w_inter = [jnp.exp(bc[h] + m_prev[h] - m_t[h]) for h in heads]
        s = [(qk[h] * w_intra[h]).astype(BF16) for h in heads]
        sv = [_dot(s[h], v1[h]) for h in heads]
        hh = []
        for h in heads:
            num = sv[h][:, :HEAD_DIM] + w_inter[h] * qcn[h][:, :HEAD_DIM]
            den = sv[h][:, HEAD_DIM:] + w_inter[h] * qcn[h][:, HEAD_DIM:]
            hh.append(num / jnp.maximum(jnp.abs(den), jnp.exp(-m_t[h])))
        for h in heads:
            cn_scr[h] = jnp.concatenate([decay[h], decay[h]], axis=1) * cn[h] + upd[h]
            m_scr[h] = m_new[h]
        on = [_head_norm(hh[h]) for h in heads]
        for h in heads:
            o_ref[r, h * HEAD_DIM:(h + 1) * HEAD_DIM] = (on[h] * jax.nn.sigmoid(sl(3, h).astype(F32))).astype(o_ref.dtype)
        rep_scr[1 - slot] = rep_next
        return carry

    lax.fori_loop(0, nch, chunk, 0, unroll=2)
    for h in heads:
        cnh = cn_scr[h]
        c_out_ref[h] = cnh[:, :HEAD_DIM]
        n_out_ref[h:h + 1, :] = cnh[:, HEAD_DIM:].T[0:1, :]
        m_out_ref[h:h + 1, :] = m_scr[h]


def _mlstm_prompt(p, gif, bias, layer, b, seq):
    half = 4 * MIX_WIDTH
    nch = seq // CHUNK
    blocks = (2 * seq * half * 2 + 2 * seq * N_GIF * 4 + 2 * seq * MIX_WIDTH * 2
              + 5 * HEADS * HEAD_DIM * 2 * HEAD_DIM * 4 + 4 * CHUNK * N_REP * V7X_LANES * 4)
    return pl.pallas_call(
        functools.partial(_mlstm_kernel, seq=seq),
        out_shape=(jax.ShapeDtypeStruct((b * seq, MIX_WIDTH), BF16),
                   jax.ShapeDtypeStruct((b, HEADS, HEAD_DIM, HEAD_DIM), F32),
                   jax.ShapeDtypeStruct((b, HEADS, HEAD_DIM), F32),
                   jax.ShapeDtypeStruct((b, HEADS, V7X_LANES), F32)),
        grid=(b,),
        in_specs=[pl.BlockSpec((seq, half), lambda i: (i, 1)),
                  pl.BlockSpec((seq, N_GIF), lambda i: (i, 0)),
                  _resident((None, 2 * HEADS, CHUNK), lambda i: (layer, 0, 0))],
        out_specs=(pl.BlockSpec((seq, MIX_WIDTH), lambda i: (i, 0)),
                   pl.BlockSpec((None, HEADS, HEAD_DIM, HEAD_DIM), lambda i: (i, 0, 0, 0)),
                   pl.BlockSpec((None, HEADS, HEAD_DIM), lambda i: (i, 0, 0)),
                   pl.BlockSpec((None, HEADS, V7X_LANES), lambda i: (i, 0, 0))),
        scratch_shapes=[pltpu.VMEM((HEADS, HEAD_DIM, 2 * HEAD_DIM), F32),
                        pltpu.VMEM((HEADS, 1, V7X_LANES), F32),
                        pltpu.VMEM((2, 8 * nch, CHUNK), F32),
                        pltpu.VMEM((2, CHUNK, N_REP * V7X_LANES), F32)],
        compiler_params=_params(blocks, 1),
        name="mlstm",
    )(p, gif, bias)


SEQ_BLOCK_FIRST = 8
SEQ_BLOCK = 16


def _column_expander(sb):
    er = lax.broadcasted_iota(jnp.int32, (sb, sb * V7X_LANES), 0)
    ec = lax.broadcasted_iota(jnp.int32, (sb, sb * V7X_LANES), 1)
    return jnp.where(ec // V7X_LANES == er, 1.0, 0.0).astype(BF16)


def _columns(x, expander):
    return _dot_tn(x.astype(BF16), expander)


def _ret_step_kernel(p_ref, cos_ref, sin_ref, s_ref, *rest, layer, first):
    o_ref, s_out_ref, row_scr = rest[-3:]
    s_out_ref = _own_layer(s_out_ref, layer, first)
    sb = p_ref.shape[0]
    cs = cos_ref[...]
    sn = sin_ref[...]
    expander = _column_expander(sb)
    for h in range(HEADS):
        lo = h * HEAD_DIM
        gamma = math.exp(LOG_GAMMA[h])
        q = _rope(p_ref[:, lo:lo + HEAD_DIM], cs, sn)
        k = _rope(p_ref[:, MIX_WIDTH + lo:MIX_WIDTH + lo + HEAD_DIM], cs, sn) * QK_SCALE
        v = p_ref[:, 2 * MIX_WIDTH + lo:2 * MIX_WIDTH + lo + HEAD_DIM]
        g = p_ref[:, 3 * MIX_WIDTH + lo:3 * MIX_WIDTH + lo + HEAD_DIM]
        qt = _columns(q * gamma, expander)
        kt = _columns(k, expander)
        for b in range(sb):
            s = s_ref[b, h]
            blk = slice(b * V7X_LANES, (b + 1) * V7X_LANES)
            row_scr[b:b + 1, :] = jnp.sum(qt[:, blk] * s, axis=0, keepdims=True)
            s_out_ref[b, h] = gamma * s + kt[:, blk] * v[b:b + 1, :]
        o = jnp.sum(q * k, axis=-1, keepdims=True) * v + row_scr[...]
        o_ref[:, lo:lo + HEAD_DIM] = _head_norm(o) * (g * jax.nn.sigmoid(g))


def _stacked_state(acc, layer, sb):
    tail = (HEADS, HEAD_DIM, HEAD_DIM)
    if acc is None:
        return (), [], pl.BlockSpec((DEPTH, sb) + tail, lambda i: (0, i, 0, 0, 0))
    return (acc,), [pl.BlockSpec(memory_space=pl.ANY)], pl.BlockSpec((None, sb) + tail, lambda i: (layer, i, 0, 0, 0))


def _own_layer(state_out_ref, layer, first):
    if not first:
        return state_out_ref
    for other in range(DEPTH):
        if other != layer:
            state_out_ref[other] = jnp.zeros(state_out_ref.shape[1:], F32)
    return state_out_ref.at[layer]


def _retention_step(p, cos, sin, state, layer, acc):
    nseq = p.shape[0]
    sb = SEQ_BLOCK_FIRST if acc is None else SEQ_BLOCK
    half = 4 * MIX_WIDTH
    sblk = sb * HEADS * HEAD_DIM * HEAD_DIM * 4
    blocks = (2 + 2 * (DEPTH if acc is None else 1)) * sblk + 2 * sb * half * 4 + 2 * sb * MIX_WIDTH * 4
    state_blk = pl.BlockSpec((None, sb, HEADS, HEAD_DIM, HEAD_DIM), lambda i: (layer, i, 0, 0, 0))
    extra, extra_specs, out_state_blk = _stacked_state(acc, layer, sb)
    return pl.pallas_call(
        functools.partial(_ret_step_kernel, layer=layer, first=acc is None),
        out_shape=(jax.ShapeDtypeStruct((nseq, MIX_WIDTH), F32),
                   jax.ShapeDtypeStruct(state.shape, F32)),
        grid=(nseq // sb,),
        in_specs=[pl.BlockSpec((sb, half), lambda i: (i, 0)),
                  _resident((1, HEAD_DIM), lambda i: (0, 0)),
                  _resident((1, HEAD_DIM), lambda i: (0, 0)),
                  state_blk] + extra_specs,
        out_specs=(pl.BlockSpec((sb, MIX_WIDTH), lambda i: (i, 0)), out_state_blk),
        input_output_aliases={4: 1} if extra else {},
        scratch_shapes=[pltpu.VMEM((sb, HEAD_DIM), F32)],
        compiler_params=_params(blocks, 1),
        name="retention_step",
    )(p, cos, sin, state, *extra)


def _mlstm_step_kernel(p_ref, gif_ref, bias_ref, c_ref, n_ref, m_ref, *rest, layer, first):
    o_ref, c_out_ref, n_out_ref, m_out_ref, row_scr = rest[-5:]
    c_out_ref = _own_layer(c_out_ref, layer, first)
    sb = p_ref.shape[0]
    fb = gif_ref[...] + bias_ref[...]
    lf_all = jax.nn.log_sigmoid(fb)
    lane = lax.broadcasted_iota(jnp.int32, (sb, V7X_LANES), 1)
    m_all = m_ref[...]
    m_acc = jnp.zeros((sb, V7X_LANES), F32)
    expander = _column_expander(sb)
    for h in range(HEADS):
        lo = h * HEAD_DIM
        q = p_ref[:, lo:lo + HEAD_DIM]
        k = p_ref[:, MIX_WIDTH + lo:MIX_WIDTH + lo + HEAD_DIM] * QK_SCALE
        v = p_ref[:, 2 * MIX_WIDTH + lo:2 * MIX_WIDTH + lo + HEAD_DIM]
        og = p_ref[:, 3 * MIX_WIDTH + lo:3 * MIX_WIDTH + lo + HEAD_DIM]
        ig = fb[:, h:h + 1]
        inter = lf_all[:, HEADS + h:HEADS + h + 1] + m_all[:, h:h + 1]
        m_t = jnp.maximum(inter, ig)
        w_intra = jnp.exp(ig - m_t)
        w_inter = jnp.exp(inter - m_t)
        nv = n_ref[:, lo:lo + HEAD_DIM]
        kw = k * w_intra
        qt = _columns(q, expander)
        kt = _columns(kw, expander)
        for b in range(sb):
            cm = c_ref[b, h]
            blk = slice(b * V7X_LANES, (b + 1) * V7X_LANES)
            row_scr[b:b + 1, :] = jnp.sum(qt[:, blk] * cm, axis=0, keepdims=True)
            c_out_ref[b, h] = w_inter[b:b + 1, :] * cm + kt[:, blk] * v[b:b + 1, :]
        s = jnp.sum(q * k, axis=-1, keepdims=True) * w_intra
        num = s * v + w_inter * row_scr[...]
        den = s + w_inter * jnp.sum(q * nv, axis=-1, keepdims=True)
        hh = num / jnp.maximum(jnp.abs(den), jnp.exp(-m_t))
        n_out_ref[:, lo:lo + HEAD_DIM] = w_inter * nv + kw
        m_acc = jnp.where(lane == h, m_t, m_acc)
        o_ref[:, lo:lo + HEAD_DIM] = _head_norm(hh) * jax.nn.sigmoid(og)
    m_out_ref[...] = m_acc


def _mlstm_step(p, gif, bias, c_state, n_state, m_state, layer, acc):
    nseq = p.shape[0]
    sb = SEQ_BLOCK_FIRST if acc is None else SEQ_BLOCK
    half = 4 * MIX_WIDTH
    sblk = sb * HEADS * HEAD_DIM * HEAD_DIM * 4
    blocks = (2 + 2 * (DEPTH if acc is None else 1)) * sblk + 2 * sb * half * 4 + 8 * sb * MIX_WIDTH * 4
    rows = lambda w: pl.BlockSpec((sb, w), lambda i: (i, 0))
    state_blk = pl.BlockSpec((None, sb, HEADS, HEAD_DIM, HEAD_DIM), lambda i: (layer, i, 0, 0, 0))
    extra, extra_specs, out_state_blk = _stacked_state(acc, layer, sb)
    return pl.pallas_call(
        functools.partial(_mlstm_step_kernel, layer=layer, first=acc is None),
        out_shape=(jax.ShapeDtypeStruct((nseq, MIX_WIDTH), F32),
                   jax.ShapeDtypeStruct(c_state.shape, F32),
                   jax.ShapeDtypeStruct((nseq, MIX_WIDTH), F32),
                   jax.ShapeDtypeStruct((nseq, V7X_LANES), F32)),
        grid=(nseq // sb,),
        in_specs=[pl.BlockSpec((sb, half), lambda i: (i, 1)),
                  rows(N_GIF),
                  _resident((None, 1, N_GIF), lambda i: (layer, 0, 0)),
                  state_blk,
                  pl.BlockSpec((None, sb, MIX_WIDTH), lambda i: (layer, i, 0)),
                  pl.BlockSpec((None, sb, HEADS), lambda i: (layer, i, 0))] + extra_specs,
        out_specs=(rows(MIX_WIDTH), out_state_blk, rows(MIX_WIDTH), rows(V7X_LANES)),
        input_output_aliases={6: 1} if extra else {},
        scratch_shapes=[pltpu.VMEM((sb, HEAD_DIM), F32)],
        compiler_params=_params(blocks, 1),
        name="mlstm_step",
    )(p, gif, bias, c_state, n_state, m_state, *extra)


def _rope_tables(pos):
    inv = ROPE_THETA ** (-jnp.arange(0, HEAD_DIM, 2, dtype=F32) / HEAD_DIM)
    ang = pos.astype(F32)[:, None] * inv[None, :]
    cos, sin = jnp.cos(ang), jnp.sin(ang)
    return jnp.concatenate([cos, cos], axis=-1), jnp.concatenate([-sin, sin], axis=-1)


W_IN_BLOCKS = N_IN_PADDED // W_IN_ROWS
W_IN_PER_STEP = 4
W_IN_STEPS = W_IN_BLOCKS // W_IN_PER_STEP


def _w_in_source_row(j):
    o = j * W_IN_ROWS
    off_gif = N_MAIN
    off_su = off_gif + 2 * HEADS
    off_gates = off_su + MIX_WIDTH
    return jnp.where(o < OFF_GATES, o,
                     jnp.where(o < OFF_SU, o - OFF_GATES + off_gates,
                               jnp.where(o < OFF_GIF, o - OFF_SU + off_su, off_gif)))


def _w_in_layout_kernel(w_hbm, o_ref, buf, sem):
    l, j = pl.program_id(0), pl.program_id(1)
    step = l * W_IN_STEPS + j
    total = pl.num_programs(0) * W_IN_STEPS
    parts = range(W_IN_PER_STEP)

    def read(s, part):
        src = pl.multiple_of(_w_in_source_row((s % W_IN_STEPS) * W_IN_PER_STEP + part), 8)
        return pltpu.make_async_copy(w_hbm.at[s // W_IN_STEPS, pl.ds(src, W_IN_ROWS), :],
                                     buf.at[s % 2, part], sem.at[s % 2, part])

    @pl.when(step == 0)
    def _():
        for part in parts:
            read(step, part).start()

    @pl.when(step + 1 < total)
    def _():
        for part in parts:
            read(step + 1, part).start()

    for part in parts:
        read(step, part).wait()
        x = buf[step % 2, part]
        if part == W_IN_PER_STEP - 1:
            row = lax.broadcasted_iota(jnp.int32, x.shape, 0)
            x = jnp.where(jnp.logical_or(j < W_IN_STEPS - 1, row < 2 * HEADS), x, 0.0)
        o_ref[part * W_IN_ROWS:(part + 1) * W_IN_ROWS, :] = x.astype(BF16)


def _w_in_layout(w_in):
    depth = w_in.shape[0]
    w_t = jnp.swapaxes(w_in, 1, 2)
    step_rows = W_IN_PER_STEP * W_IN_ROWS
    blocks = 2 * step_rows * D_MODEL * 4 + 2 * step_rows * D_MODEL * 2 + 2 * W_IN_ROWS * D_MODEL * 4
    return pl.pallas_call(
        _w_in_layout_kernel,
        out_shape=jax.ShapeDtypeStruct((depth, N_IN_PADDED, D_MODEL), BF16),
        grid=(depth, W_IN_STEPS),
        in_specs=[pl.BlockSpec(memory_space=pl.ANY)],
        out_specs=pl.BlockSpec((None, step_rows, D_MODEL), lambda l, j: (l, j, 0)),
        scratch_shapes=[pltpu.VMEM((2, W_IN_PER_STEP, W_IN_ROWS, D_MODEL), F32),
                        pltpu.SemaphoreType.DMA((2, W_IN_PER_STEP))],
        compiler_params=_params(blocks, 2),
        name="w_in_layout",
    )(w_t)


def _prepare_weights(w_ffn1_gu, w_ffn1_down, w_in, w_s5_glu, w_branch, w_out, w_ffn2_gu, w_ffn2_down):
    cast = lambda w: w.astype(BF16)
    return dict(gu1=cast(w_ffn1_gu), d1=cast(w_ffn1_down), w_in=_w_in_layout(w_in), glu=cast(w_s5_glu),
                branch=cast(w_branch), out=cast(w_out), gu2=cast(w_ffn2_gu), d2=cast(w_ffn2_down))


def _run_trunk(x, nseq, seq, states, wts, norms, s5w, bias, ln_final, prompt):
    t = nseq * seq
    tm = 512 if t % 512 == 0 else t
    tm_ffn = 1024 if t % 1024 == 0 else tm
    act_dtype = BF16 if prompt else F32
    if prompt:
        cos, sin = _rope_tables(jnp.arange(seq, dtype=jnp.int32))
    else:
        cos, sin = _rope_tables(PAST_LEN + jnp.arange(seq, dtype=jnp.int32))
    if prompt:
        x0 = jnp.zeros((nseq, 2 * S5_N), F32)
    else:
        st_ret, st_mc, st_mn, st_mm, st_sre, st_sim = states
        st_mn = st_mn.reshape(DEPTH, nseq, MIX_WIDTH)
        x0 = jnp.concatenate([st_sre.reshape(DEPTH, nseq, S5_N), st_sim.reshape(DEPTH, nseq, S5_N)], axis=-1)
    rets, mcs, mns, mms, xls = [], [], [], [], []
    ret_acc = mc_acc = None
    for l in range(DEPTH):
        x = _ffn(x, norms["ffn1"], wts["gu1"], wts["d1"], ln_final, l, tm_ffn, final=False)
        main, gates, su, gif = _inproj(x, norms["mix"], wts["w_in"], l, tm, act_dtype)
        if prompt:
            r_out, new_ret = _retention_prompt(main, cos, sin, nseq, seq)
            m_out, new_mc, new_mn, new_mm = _mlstm_prompt(main, gif, bias, l, nseq, seq)
            s_out, x_last = _s5_prompt(su.reshape(nseq, seq, MIX_WIDTH), x0, s5w, l)
            rets.append(new_ret)
            mcs.append(new_mc)
        else:
            r_out, ret_acc = _retention_step(main, cos, sin, st_ret, l, ret_acc)
            m_out, mc_acc, new_mn, new_mm = _mlstm_step(main, gif, bias, st_mc, st_mn, st_mm, l, mc_acc)
            s_out, x_last = _s5_step(su, x0, s5w, l)
        mns.append(new_mn)
        mms.append(new_mm)
        xls.append(x_last)
        x = _merge_ffn(x, r_out, m_out, s_out, gates, wts["branch"], wts["out"], norms["ffn2"], wts["gu2"], wts["d2"],
                       ln_final, l, tm, final=(l == DEPTH - 1))
    new_ret, new_mc = (jnp.stack(rets), jnp.stack(mcs)) if prompt else (ret_acc, mc_acc)
    new_mn = jnp.stack(mns).reshape(DEPTH, nseq, HEADS, HEAD_DIM)
    new_mm = jnp.stack(mms)[:, :, :, 0] if prompt else jnp.stack(mms)[:, :, :HEADS]
    xl = jnp.stack(xls)
    new_sre = xl[:, :, :S5_N].reshape(DEPTH, nseq, S5_GROUPS, S5_STATE)
    new_sim = xl[:, :, S5_N:].reshape(DEPTH, nseq, S5_GROUPS, S5_STATE)
    return x, (new_ret, new_mc, new_mn, new_mm, new_sre, new_sim)


def kernel(x_prompt, x_sample, state_ret, state_mlstm_c, state_mlstm_n, state_mlstm_m, state_s5_re, state_s5_im, ln_ffn1, w_ffn1_gu, w_ffn1_down, ln_mix, w_in, b_gates, s5_a_re, s5_a_im, s5_b_re, s5_b_im, s5_c_re, s5_c_im, s5_d, s5_log_dt, w_s5_glu, w_branch, w_out, ln_ffn2, w_ffn2_gu, w_ffn2_down, ln_final):
    wts = _prepare_weights(w_ffn1_gu, w_ffn1_down, w_in, w_s5_glu, w_branch, w_out, w_ffn2_gu, w_ffn2_down)
    norms = dict(ffn1=ln_ffn1.reshape(DEPTH, 1, D_MODEL), mix=ln_mix.reshape(DEPTH, 1, D_MODEL),
                 ffn2=ln_ffn2.reshape(DEPTH, 1, D_MODEL))
    lnf = ln_final.reshape(1, D_MODEL)
    a_row, bbr, bbi = _s5_discretise(s5_a_re, s5_a_im, s5_log_dt, s5_b_re, s5_b_im)
    s5w = (a_row, _block_diag_in(bbr), _block_diag_in(bbi), _block_diag_out(s5_c_re), _block_diag_out(-s5_c_im),
           s5_d.reshape(DEPTH, 1, MIX_WIDTH), wts["glu"])
    bias = jnp.pad(b_gates, ((0, 0), (0, N_GIF - 2 * HEADS))).reshape(DEPTH, 1, N_GIF)
    bias_rows = jnp.broadcast_to(b_gates[:, :, None], (DEPTH, 2 * HEADS, CHUNK))

    pb, pl_len, _ = x_prompt.shape
    sb, sl, _ = x_sample.shape
    y_p, st_p = _run_trunk(x_prompt.reshape(pb * pl_len, D_MODEL), pb, pl_len, None, wts, norms, s5w, bias_rows, lnf, True)
    sample_states = (state_ret, state_mlstm_c, state_mlstm_n, state_mlstm_m, state_s5_re, state_s5_im)
    y_s, st_s = _run_trunk(x_sample.reshape(sb * sl, D_MODEL), sb, sl, sample_states, wts, norms, s5w, bias, lnf, False)
    return (y_p.reshape(pb, pl_len, D_MODEL), y_s.reshape(sb, sl, D_MODEL)) + st_p + st_s
```
